```python
import math
import jax
import jax.numpy as jnp
from jax import lax
import numpy as np

D_MODEL = 2048
BATCH = 4
SEQ = 4096
DEPTH = 2

CTX_LEN = 256
GRID_W = 64
GROUP_WIDTH = D_MODEL // 4
MIX_WIDTH = 4 * GROUP_WIDTH
CHUNK = 64
HG_DIM = 128
HG_HEADS = GROUP_WIDTH // HG_DIM
HG_F_MIN = 1e-30
HY_WIDTH = GROUP_WIDTH
HY_ORDER = 2
HY_SHORT = 3
HY_BANDS = 16
HY_EMB = 1 + 2 * HY_BANDS
HY_HIDDEN = 64
HY_FAST_DECAY = 0.3
HY_SLOW_DECAY = 1.5
HY_DECAY_TARGET = 1e-2
HY_MIN_DECAY = math.log(HY_DECAY_TARGET) / HY_SLOW_DECAY
HY_MAX_DECAY = math.log(HY_DECAY_TARGET) / HY_FAST_DECAY
NA_DIM = 64
NA_HEADS = GROUP_WIDTH // NA_DIM
NA_WIN_ROWS = 8
NA_WIN_COLS = 16
NA_QBLK = 16
NA_KBLK = 32
ML_DIM = 128
ML_HEADS = GROUP_WIDTH // ML_DIM
ML_SHORT = 3
ROPE_THETA = 10000.0
PEER_HEADS = 8
PEER_NKEYS = 128
PEER_EXPERTS = PEER_NKEYS * PEER_NKEYS
PEER_DK = 256
PEER_TOPK = 16
PEER_TOKEN_BLOCK = 64
HG_COLS = 5 * GROUP_WIDTH
HY_COLS = 3 * GROUP_WIDTH
NA_COLS = 3 * GROUP_WIDTH
ML_COLS = 4 * GROUP_WIDTH + 4 * ML_HEADS
IN_WIDTH = HG_COLS + HY_COLS + NA_COLS + ML_COLS
DEEPNORM_ALPHA = (2 * DEPTH) ** 0.25
DEEPNORM_BETA = (8 * DEPTH) ** -0.25
MASK_VALUE = -1e30

kernel_name = 'hybrid_peer_diffusion_block'


def _layer_norm(x, g, b, eps=1e-5):
    xf = x.astype(jnp.float32)
    mu = jnp.mean(xf, axis=-1, keepdims=True)
    var = jnp.mean(jnp.square(xf - mu), axis=-1, keepdims=True)
    y = (xf - mu) * lax.rsqrt(var + eps) * g.astype(jnp.float32) + b.astype(jnp.float32)
    return y.astype(x.dtype)


def _rms_norm(x, g, eps=1e-6):
    xf = x.astype(jnp.float32)
    return xf * lax.rsqrt(jnp.mean(jnp.square(xf), axis=-1, keepdims=True) + eps) * g.astype(jnp.float32)


def _centred_dwconv(x, w, b):
    k = w.shape[0]
    y = lax.conv_general_dilated(x, w[:, None, :].astype(x.dtype), window_strides=(1,),
                                 padding=((k // 2, k // 2),), dimension_numbers=('NWC', 'WIO', 'NWC'),
                                 feature_group_count=x.shape[-1])
    return y + b.astype(x.dtype)


def _axial_rope(x):
    L, dh = x.shape[1], x.shape[-1]
    half, quarter = dh // 2, dh // 4
    t = jnp.arange(L)
    inv_freq = ROPE_THETA ** (-jnp.arange(quarter, dtype=jnp.float32) / quarter)

    def rot(xa, pos):
        ang = pos.astype(jnp.float32)[:, None] * inv_freq
        cos = jnp.cos(ang)[None, :, None, :]
        sin = jnp.sin(ang)[None, :, None, :]
        x1, x2 = xa[..., :quarter], xa[..., quarter:]
        return jnp.concatenate([x1 * cos - x2 * sin, x2 * cos + x1 * sin], axis=-1)

    return jnp.concatenate([rot(x[..., :half], t // GRID_W), rot(x[..., half:], t % GRID_W)], axis=-1)


def _chunks(x):
    b, h, L = x.shape[:3]
    return jnp.moveaxis(x.reshape(b, h, L // CHUNK, CHUNK, *x.shape[3:]), 2, 0)


def _unchunks(x):
    x = jnp.moveaxis(x, 0, 2)
    return x.reshape(x.shape[0], x.shape[1], -1, *x.shape[4:])


def _flip(t):
    return jnp.flip(t, axis=2)


def _hgrn2_scan(q, k, v, logf, s0):
    causal = jnp.tril(jnp.ones((CHUNK, CHUNK), dtype=bool))

    def step(s, inp):
        qc, kc, vc, lfc = inp
        b = jnp.cumsum(lfc, axis=2)
        diff = b[:, :, :, None, :] - b[:, :, None, :, :]
        decay = jnp.exp(jnp.where(causal[:, :, None], diff, MASK_VALUE))
        att = jnp.einsum('bhtd,bhtsd,bhsd->bhts', qc, decay, kc)
        o = jnp.einsum('bhts,bhse->bhte', att, vc) + jnp.einsum('bhtd,bhde->bhte', qc * jnp.exp(b), s)
        b_end = b[:, :, -1:, :]
        s_new = (jnp.exp(b_end[:, :, 0, :, None]) * s
                 + jnp.einsum('bhsd,bhse->bhde', kc * jnp.exp(b_end - b), vc))
        return s_new, o

    s_fin, o = lax.scan(step, s0, (_chunks(q), _chunks(k), _chunks(v), _chunks(logf)))
    return _unchunks(o), s_fin


def _hgrn2_group(p_lat, p_ctx, lb, norm_w, need_ctx):
    def prep(p):
        B, L, _ = p.shape
        q, zf, zb, i, g = jnp.split(p.astype(jnp.float32), 5, axis=-1)

        def heads(t):
            return t.reshape(B, L, HG_HEADS, HG_DIM).transpose(0, 2, 1, 3)

        def forget(z, lbd):
            f = jax.nn.sigmoid(z) + lbd * jax.nn.sigmoid(-z)
            return heads(jnp.log(jnp.maximum(f, HG_F_MIN))), heads((1.0 - lbd) * jax.nn.sigmoid(-z))

        lf_f, k_f = forget(zf, lb[0])
        lf_b, k_b = forget(zb, lb[1])
        return heads(q), heads(i), k_f, lf_f, k_b, lf_b, g

    def readout(o, g):
        B, H, L, dv = o.shape
        o = _rms_norm(o, norm_w).transpose(0, 2, 1, 3).reshape(B, L, H * dv)
        return o * jax.nn.silu(g)

    qc, ic, kcf, lcf, kcb, lcb, gc = prep(p_ctx)
    s0 = jnp.zeros(qc.shape[:2] + (HG_DIM, HG_DIM), jnp.float32)
    oc_f, s_f = _hgrn2_scan(qc, kcf, ic, lcf, s0)
    oc_b, s_b = _hgrn2_scan(_flip(qc), _flip(kcb), _flip(ic), _flip(lcb), s0)
    ql, il, klf, llf, klb, llb, gl = prep(p_lat)
    ol_f, _ = _hgrn2_scan(ql, klf, il, llf, s_f)
    ol_b, _ = _hgrn2_scan(_flip(ql), _flip(klb), _flip(il), _flip(llb), s_b)
    y_lat = readout(ol_f + _flip(ol_b), gl)
    y_ctx = readout(oc_f + _flip(oc_b), gc) if need_ctx else None
    return y_lat, y_ctx


def _hyena_filters(L, w1, b1, w2, b2, w3, freq):
    f32 = jnp.float32
    pos = jnp.arange(L, dtype=f32)
    t = pos / (L - 1)
    bands = jnp.linspace(1e-4, HY_BANDS - 1, HY_BANDS, dtype=f32)
    ang = (2.0 * math.pi / L) * pos[:, None] * bands[None, :]
    z = jnp.concatenate([t[:, None], jnp.cos(ang), jnp.sin(ang)], axis=-1)
    freq = freq.astype(f32)
    hid = jnp.sin(freq[0] * (z @ w1.astype(f32) + b1.astype(f32)))
    hid = jnp.sin(freq[1] * (hid @ w2.astype(f32) + b2.astype(f32)))
    h = (hid @ w3.astype(f32)).reshape(L, 2, HY_ORDER, HY_WIDTH)
    deltas = jnp.abs(jnp.linspace(HY_MIN_DECAY, HY_MAX_DECAY, HY_WIDTH, dtype=f32))
    h = h * jnp.exp(-t[:, None] * deltas)[:, None, None, :]
    fwd, bwd = h[:, 0], h[:, 1]
    k = jnp.concatenate([fwd, jnp.zeros_like(fwd[:1]), bwd[:0:-1]], axis=0)
    return k / jnp.sum(jnp.abs(k), axis=0, keepdims=True)


def _fft_long_conv(z, k, d):
    n, L = k.shape[0], z.shape[1]
    spec = jnp.fft.rfft(z, n=n, axis=1) * jnp.fft.rfft(k, axis=0)[None]
    return jnp.fft.irfft(spec, n=n, axis=1)[:, :L] + z * d.astype(jnp.float32)


def _hyena_seq(p, conv_w, conv_b, w1, b1, w2, b2, w3, freq, dbias):
    u = _centred_dwconv(p, conv_w, conv_b).astype(jnp.float32)
    v, x1, x2 = jnp.split(u, 3, axis=-1)
    k = _hyena_filters(p.shape[1], w1, b1, w2, b2, w3, freq)
    z = x1 * _fft_long_conv(v, k[:, 0], dbias[0])
    return x2 * _fft_long_conv(z, k[:, 1], dbias[1])


def _natten_group(p_lat, p_ctx, rpb, need_ctx):
    f32 = jnp.float32
    B, L, _ = p_lat.shape

    def heads(p):
        return [t.reshape(t.shape[0], t.shape[1], NA_HEADS, NA_DIM) for t in jnp.split(p, 3, axis=-1)]

    q, k, v = heads(p_lat)
    qc, kc, vc = heads(p_ctx)
    scale = NA_DIM ** -0.5
    rows = L // GRID_W
    kh = min(NA_WIN_ROWS, rows)
    ncb = GRID_W // NA_QBLK
    nk = kh * NA_KBLK
    r = jnp.arange(rows)
    key_rows = jnp.clip(r - kh // 2, 0, rows - kh)[:, None] + jnp.arange(kh)
    blk = jnp.arange(ncb)
    qcol = blk[:, None] * NA_QBLK + jnp.arange(NA_QBLK)
    key_cols = (jnp.clip(blk * NA_QBLK - NA_WIN_COLS // 2, 0, GRID_W - NA_KBLK)[:, None]
                + jnp.arange(NA_KBLK))
    idx = (key_rows[:, None, :, None] * GRID_W + key_cols[None, :, None, :]).reshape(rows, ncb, nk)
    col_start = jnp.clip(qcol - NA_WIN_COLS // 2, 0, GRID_W - NA_WIN_COLS)[:, :, None, None]
    kcol = key_cols[:, None, None, :]
    valid = jnp.broadcast_to((kcol >= col_start) & (kcol < col_start + NA_WIN_COLS),
                             (ncb, NA_QBLK, kh, NA_KBLK)).reshape(ncb, NA_QBLK, nk)
    dc = jnp.clip(kcol - qcol[:, :, None, None], 1 - NA_WIN_COLS, NA_WIN_COLS - 1)
    dr = key_rows - r[:, None]
    bias = rpb.astype(f32)[:, dr[:, None, None, :, None] + NA_WIN_ROWS - 1, dc[None] + NA_WIN_COLS - 1]
    bias = jnp.moveaxis(bias.reshape(NA_HEADS, rows, ncb, NA_QBLK, nk), 1, 0)
    q_rows = jnp.moveaxis((q * scale).reshape(B, rows, ncb, NA_QBLK, NA_HEADS, NA_DIM), 1, 0)

    def row_block(inp):
        q_r, idx_r, bias_r = inp
        k_r, v_r = k[:, idx_r], v[:, idx_r]
        s_win = jnp.einsum('bjqhd,bjkhd->bhjqk', q_r, k_r).astype(f32)
        s_win = jnp.where(valid, s_win + bias_r, MASK_VALUE)
        s_ctx = jnp.einsum('bjqhd,bchd->bhjqc', q_r, kc).astype(f32)
        prob = jax.nn.softmax(jnp.concatenate([s_win, s_ctx], axis=-1), axis=-1).astype(v.dtype)
        return (jnp.einsum('bhjqk,bjkhd->bjqhd', prob[..., :nk], v_r)
                + jnp.einsum('bhjqc,bchd->bjqhd', prob[..., nk:], vc))

    out = lax.map(row_block, (q_rows, idx, bias))
    y_lat = jnp.moveaxis(out, 0, 1).reshape(B, L, GROUP_WIDTH)
    y_ctx = None
    if need_ctx:
        s = jnp.einsum('bqhd,bkhd->bhqk', qc * scale, kc).astype(f32)
        prob = jax.nn.softmax(s, axis=-1).astype(vc.dtype)
        y_ctx = jnp.einsum('bhqk,bkhd->bqhd', prob, vc).reshape(B, -1, GROUP_WIDTH)
    return y_lat, y_ctx


def _mlstm_scan(q, k, v, ig, lf, state):
    causal = jnp.tril(jnp.ones((CHUNK, CHUNK), dtype=bool))

    def step(carry, inp):
        ckv, n, m = carry
        qc, kc, vc, igc, lfc = inp
        g = jnp.cumsum(lfc, axis=-1)
        logd = jnp.where(causal, g[..., :, None] - g[..., None, :] + igc[..., None, :], MASK_VALUE)
        log_inter = g + m[..., None]
        m_out = jnp.maximum(log_inter, jnp.max(logd, axis=-1))
        dmat = jnp.exp(logd - m_out[..., None])
        w_inter = jnp.exp(log_inter - m_out)
        sc = jnp.einsum('bhtd,bhsd->bhts', qc, kc) * dmat
        num = jnp.einsum('bhts,bhse->bhte', sc, vc) + w_inter[..., None] * jnp.einsum('bhtd,bhde->bhte', qc, ckv)
        den = jnp.sum(sc, axis=-1) + w_inter * jnp.einsum('bhtd,bhd->bht', qc, n)
        h = num / jnp.maximum(jnp.abs(den), jnp.exp(-m_out))[..., None]
        g_end = g[..., -1]
        a = g_end[..., None] - g + igc
        m_new = jnp.maximum(g_end + m, jnp.max(a, axis=-1))
        carry_w = jnp.exp(g_end + m - m_new)
        w = jnp.exp(a - m_new[..., None])
        ckv = carry_w[..., None, None] * ckv + jnp.einsum('bhs,bhsd,bhse->bhde', w, kc, vc)
        n = carry_w[..., None] * n + jnp.einsum('bhs,bhsd->bhd', w, kc)
        return (ckv, n, m_new), h

    state, h = lax.scan(step, state, (_chunks(q), _chunks(k), _chunks(v), _chunks(ig), _chunks(lf)))
    return _unchunks(h), state


def _mlstm_group(p_lat, p_ctx, conv_w, conv_b, gate_b, need_ctx):
    W = GROUP_WIDTH
    f32 = jnp.float32

    def prep(p, rope):
        B, L, _ = p.shape
        qk = jax.nn.silu(_centred_dwconv(p[..., :2 * W], conv_w, conv_b)).astype(f32)
        q = qk[..., :W].reshape(B, L, ML_HEADS, ML_DIM)
        k = qk[..., W:].reshape(B, L, ML_HEADS, ML_DIM)
        if rope:
            q, k = _axial_rope(q), _axial_rope(k)
        v = p[..., 2 * W:3 * W].astype(f32).reshape(B, L, ML_HEADS, ML_DIM)
        og = jax.nn.sigmoid(p[..., 3 * W:4 * W].astype(f32))
        gates = (p[..., 4 * W:].astype(f32) + gate_b.astype(f32)).reshape(B, L, 4, ML_HEADS)
        gates = gates.transpose(2, 0, 3, 1)

        def hf(t):
            return t.transpose(0, 2, 1, 3)

        return (hf(q), hf(k * ML_DIM ** -0.5), hf(v), gates[0], jax.nn.log_sigmoid(gates[1]),
                gates[2], jax.nn.log_sigmoid(gates[3]), og)

    def readout(h, og):
        return og * h.transpose(0, 2, 1, 3).reshape(og.shape)

    qc, kc, vc, icf, lcf, icb, lcb, oc = prep(p_ctx, False)
    B = qc.shape[0]
    s0 = (jnp.zeros((B, ML_HEADS, ML_DIM, ML_DIM), f32), jnp.zeros((B, ML_HEADS, ML_DIM), f32),
          jnp.zeros((B, ML_HEADS), f32))
    hc_f, st_f = _mlstm_scan(qc, kc, vc, icf, lcf, s0)
    hc_b, st_b = _mlstm_scan(_flip(qc), _flip(kc), _flip(vc), _flip(icb), _flip(lcb), s0)
    ql, kl, vl, ilf, llf, ilb, llb, ol = prep(p_lat, True)
    hl_f, _ = _mlstm_scan(ql, kl, vl, ilf, llf, st_f)
    hl_b, _ = _mlstm_scan(_flip(ql), _flip(kl), _flip(vl), _flip(ilb), _flip(llb), st_b)
    y_lat = readout(hl_f + _flip(hl_b), ol)
    y_ctx = readout(hc_f + _flip(hc_b), oc) if need_ctx else None
    return y_lat, y_ctx


def _peer(h, wq, subkeys, u, v):
    B, L, D = h.shape
    n_tok = B * L
    tok = h.reshape(n_tok, D)
    q = (tok @ wq).reshape(n_tok, PEER_HEADS, 2, PEER_DK // 2)
    s = jnp.einsum('thpd,hpkd->thpk', q, subkeys).astype(jnp.float32)
    sv, si = lax.top_k(s, PEER_TOPK)
    cand_s = (sv[:, :, 0, :, None] + sv[:, :, 1, None, :]).reshape(n_tok, PEER_HEADS, -1)
    cand_i = (si[:, :, 0, :, None] * PEER_NKEYS + si[:, :, 1, None, :]).reshape(n_tok, PEER_HEADS, -1)
    best_s, best_p = lax.top_k(cand_s, PEER_TOPK)
    ids = jnp.take_along_axis(cand_i, best_p, axis=-1)
    gate = jax.nn.softmax(best_s, axis=-1)
    nb = n_tok // PEER_TOKEN_BLOCK

    def block(inp):
        tb, ib, gb = inp
        a = jnp.einsum('td,tkd->tk', tb, u[ib])
        return jnp.einsum('tk,tkd->td', gb.astype(tb.dtype) * jax.nn.gelu(a, approximate=False), v[ib])

    out = lax.map(block, (tok.reshape(nb, PEER_TOKEN_BLOCK, D),
                          ids.reshape(nb, PEER_TOKEN_BLOCK, -1),
                          gate.reshape(nb, PEER_TOKEN_BLOCK, -1)))
    return out.reshape(B, L, D)


def setup_inputs(seed: int = 0) -> dict:
    key = jax.random.key(seed)
    ks = iter(jax.random.split(key, 40))
    f32 = jnp.float32
    D = D_MODEL

    def nrm(shape, scale):
        return jax.random.normal(next(ks), shape, f32) * scale

    ig_b = nrm((DEPTH, 2, ML_HEADS), 0.1)
    fg_b = jax.random.uniform(next(ks), (DEPTH, 2, ML_HEADS), f32, 3.0, 6.0)
    ml_gate_b = jnp.stack([ig_b[:, 0], fg_b[:, 0], ig_b[:, 1], fg_b[:, 1]], axis=1).reshape(DEPTH, 4 * ML_HEADS)
    return {
        'x': nrm((BATCH, SEQ, D), 1.0),
        'c': nrm((BATCH, D), 1.0),
        'ctx': nrm((BATCH, CTX_LEN, D), 1.0),
        'c_ctx': nrm((D,), 1.0),
        'w_ada': nrm((DEPTH, D, 6 * D), D ** -0.5),
        'b_ada': nrm((DEPTH, 6 * D), 0.02),
        'w_in': nrm((DEPTH, D, IN_WIDTH), D ** -0.5),
        'w_out': nrm((DEPTH, MIX_WIDTH, D), MIX_WIDTH ** -0.5 * DEEPNORM_BETA),
        'ln_g': 1.0 + nrm((DEPTH, 2, D), 0.02),
        'ln_b': nrm((DEPTH, 2, D), 0.02),
        'hg_lower_bounds': 1.0 + nrm((DEPTH, 2, GROUP_WIDTH), 0.1),
        'hg_norm_w': 1.0 + nrm((DEPTH, HG_DIM), 0.02),
        'hy_conv_w': nrm((DEPTH, HY_SHORT, HY_COLS), HY_SHORT ** -0.5),
        'hy_conv_b': nrm((DEPTH, HY_COLS), 0.02),
        'hy_w1': nrm((DEPTH, HY_EMB, HY_HIDDEN), HY_EMB ** -0.5),
        'hy_b1': nrm((DEPTH, HY_HIDDEN), 0.02),
        'hy_w2': nrm((DEPTH, HY_HIDDEN, HY_HIDDEN), HY_HIDDEN ** -0.5),
        'hy_b2': nrm((DEPTH, HY_HIDDEN), 0.02),
        'hy_w3': nrm((DEPTH, HY_HIDDEN, 2 * HY_ORDER * HY_WIDTH), HY_HIDDEN ** -0.5),
        'hy_freq': 1.0 + nrm((DEPTH, 2, HY_HIDDEN), 0.1),
        'hy_dbias': nrm((DEPTH, HY_ORDER, HY_WIDTH), 1.0),
        'na_rpb': nrm((DEPTH, NA_HEADS, 2 * NA_WIN_ROWS - 1, 2 * NA_WIN_COLS - 1), 0.1),
        'ml_conv_w': nrm((DEPTH, ML_SHORT, 2 * GROUP_WIDTH), ML_SHORT ** -0.5),
        'ml_conv_b': nrm((DEPTH, 2 * GROUP_WIDTH), 0.02),
        'ml_gate_b': ml_gate_b,
        'peer_wq': nrm((DEPTH, D, PEER_HEADS * PEER_DK), D ** -0.5),
        'peer_subkeys': nrm((DEPTH, PEER_HEADS, 2, PEER_NKEYS, PEER_DK // 2), (PEER_DK // 2) ** -0.5),
        'peer_u': nrm((DEPTH, PEER_EXPERTS, D), D ** -0.5),
        'peer_v': nrm((DEPTH, PEER_EXPERTS, D), DEEPNORM_BETA),
    }


def reference(x, c, ctx, c_ctx, w_ada, b_ada, w_in, w_out, ln_g, ln_b, hg_lower_bounds, hg_norm_w,
              hy_conv_w, hy_conv_b, hy_w1, hy_b1, hy_w2, hy_b2, hy_w3, hy_freq, hy_dbias, na_rpb,
              ml_conv_w, ml_conv_b, ml_gate_b, peer_wq, peer_subkeys, peer_u, peer_v):
    lb_soft = jax.nn.softmax(hg_lower_bounds.astype(jnp.float32), axis=0)
    lower_bounds = jnp.cumsum(lb_soft, axis=0) - lb_soft[0]
    cond_lat = jax.nn.silu(c)
    cond_ctx = jax.nn.silu(c_ctx)
    cuts = [HG_COLS, HG_COLS + HY_COLS, HG_COLS + HY_COLS + NA_COLS]
    xc = ctx
    for l in range(DEPTH):
        need_ctx = l < DEPTH - 1
        m_lat = jnp.split((cond_lat @ w_ada[l] + b_ada[l])[:, None, :], 6, axis=-1)
        m_ctx = jnp.split(cond_ctx @ w_ada[l] + b_ada[l], 6, axis=-1)
        pa_l, pb_l, pc_l, pd_l = jnp.split((x * (1 + m_lat[1]) + m_lat[0]) @ w_in[l], cuts, axis=-1)
        pa_c, pb_c, pc_c, pd_c = jnp.split((xc * (1 + m_ctx[1]) + m_ctx[0]) @ w_in[l], cuts, axis=-1)
        hy = (hy_conv_w[l], hy_conv_b[l], hy_w1[l], hy_b1[l], hy_w2[l], hy_b2[l], hy_w3[l], hy_freq[l], hy_dbias[l])
        a_l, a_c = _hgrn2_group(pa_l, pa_c, lower_bounds[l], hg_norm_w[l], need_ctx)
        b_l = _hyena_seq(pb_l, *hy)
        c_l, c_c = _natten_group(pc_l, pc_c, na_rpb[l], need_ctx)
        d_l, d_c = _mlstm_group(pd_l, pd_c, ml_conv_w[l], ml_conv_b[l], ml_gate_b[l], need_ctx)
        y_lat = jnp.concatenate([t.astype(x.dtype) for t in (a_l, b_l, c_l, d_l)], axis=-1) @ w_out[l]
        if need_ctx:
            b_c = _hyena_seq(pb_c, *hy)
            y_ctx = jnp.concatenate([t.astype(xc.dtype) for t in (a_c, b_c, c_c, d_c)], axis=-1) @ w_out[l]
            xc = _layer_norm(DEEPNORM_ALPHA * xc + m_ctx[2] * y_ctx, ln_g[l, 0], ln_b[l, 0])
        x = _layer_norm(DEEPNORM_ALPHA * x + m_lat[2] * y_lat, ln_g[l, 0], ln_b[l, 0])
        y_lat = _peer(x * (1 + m_lat[4]) + m_lat[3], peer_wq[l], peer_subkeys[l], peer_u[l], peer_v[l])
        x = _layer_norm(DEEPNORM_ALPHA * x + m_lat[5] * y_lat, ln_g[l, 1], ln_b[l, 1])
        if need_ctx:
            y_ctx = _peer(xc * (1 + m_ctx[4]) + m_ctx[3], peer_wq[l], peer_subkeys[l], peer_u[l], peer_v[l])
            xc = _layer_norm(DEEPNORM_ALPHA * xc + m_ctx[5] * y_ctx, ln_g[l, 1], ln_b[l, 1])
    return x
```

```python
import math
import jax
import jax.numpy as jnp
from jax import lax
from jax.experimental import pallas as pl
from jax.experimental.pallas import tpu as pltpu

D_MODEL = 2048
BATCH = 4
SEQ = 4096
DEPTH = 2

CTX_LEN = 256
GRID_W = 64
GROUP_WIDTH = D_MODEL // 4
MIX_WIDTH = 4 * GROUP_WIDTH
CHUNK = 64
HG_DIM = 128
HG_HEADS = GROUP_WIDTH // HG_DIM
HG_F_MIN = 1e-30
HY_WIDTH = GROUP_WIDTH
HY_ORDER = 2
HY_SHORT = 3
HY_BANDS = 16
HY_EMB = 1 + 2 * HY_BANDS
HY_HIDDEN = 64
HY_FAST_DECAY = 0.3
HY_SLOW_DECAY = 1.5
HY_DECAY_TARGET = 1e-2
HY_MIN_DECAY = math.log(HY_DECAY_TARGET) / HY_SLOW_DECAY
HY_MAX_DECAY = math.log(HY_DECAY_TARGET) / HY_FAST_DECAY
NA_DIM = 64
NA_HEADS = GROUP_WIDTH // NA_DIM
NA_WIN_ROWS = 8
NA_WIN_COLS = 16
NA_QBLK = 16
NA_KBLK = 32
ML_DIM = 128
ML_HEADS = GROUP_WIDTH // ML_DIM
ML_SHORT = 3
ROPE_THETA = 10000.0
PEER_HEADS = 8
PEER_NKEYS = 128
PEER_EXPERTS = PEER_NKEYS * PEER_NKEYS
PEER_DK = 256
PEER_TOPK = 16
PEER_TOKEN_BLOCK = 64
HG_COLS = 5 * GROUP_WIDTH
HY_COLS = 3 * GROUP_WIDTH
NA_COLS = 3 * GROUP_WIDTH
ML_COLS = 4 * GROUP_WIDTH + 4 * ML_HEADS
IN_WIDTH = HG_COLS + HY_COLS + NA_COLS + ML_COLS
DEEPNORM_ALPHA = (2 * DEPTH) ** 0.25
DEEPNORM_BETA = (8 * DEPTH) ** -0.25
MASK_VALUE = -1e30


def _layer_norm(x, g, b, eps=1e-5):
    xf = x.astype(jnp.float32)
    mu = jnp.mean(xf, axis=-1, keepdims=True)
    var = jnp.mean(jnp.square(xf - mu), axis=-1, keepdims=True)
    y = (xf - mu) * lax.rsqrt(var + eps) * g.astype(jnp.float32) + b.astype(jnp.float32)
    return y.astype(x.dtype)


def _rms_norm(x, g, eps=1e-6):
    xf = x.astype(jnp.float32)
    return xf * lax.rsqrt(jnp.mean(jnp.square(xf), axis=-1, keepdims=True) + eps) * g.astype(jnp.float32)


def _centred_dwconv(x, w, b):
    k = w.shape[0]
    y = lax.conv_general_dilated(x, w[:, None, :].astype(x.dtype), window_strides=(1,),
                                 padding=((k // 2, k // 2),), dimension_numbers=('NWC', 'WIO', 'NWC'),
                                 feature_group_count=x.shape[-1])
    return y + b.astype(x.dtype)


def _axial_rope(x):
    L, dh = x.shape[1], x.shape[-1]
    half, quarter = dh // 2, dh // 4
    t = jnp.arange(L)
    inv_freq = ROPE_THETA ** (-jnp.arange(quarter, dtype=jnp.float32) / quarter)

    def rot(xa, pos):
        ang = pos.astype(jnp.float32)[:, None] * inv_freq
        cos = jnp.cos(ang)[None, :, None, :]
        sin = jnp.sin(ang)[None, :, None, :]
        x1, x2 = xa[..., :quarter], xa[..., quarter:]
        return jnp.concatenate([x1 * cos - x2 * sin, x2 * cos + x1 * sin], axis=-1)

    return jnp.concatenate([rot(x[..., :half], t // GRID_W), rot(x[..., half:], t % GRID_W)], axis=-1)


def _chunks(x):
    b, h, L = x.shape[:3]
    return jnp.moveaxis(x.reshape(b, h, L // CHUNK, CHUNK, *x.shape[3:]), 2, 0)


def _unchunks(x):
    x = jnp.moveaxis(x, 0, 2)
    return x.reshape(x.shape[0], x.shape[1], -1, *x.shape[4:])


def _flip(t):
    return jnp.flip(t, axis=2)


def _hgrn2_scan(q, k, v, logf, s0):
    causal = jnp.tril(jnp.ones((CHUNK, CHUNK), dtype=bool))

    def step(s, inp):
        qc, kc, vc, lfc = inp
        b = jnp.cumsum(lfc, axis=2)
        diff = b[:, :, :, None, :] - b[:, :, None, :, :]
        decay = jnp.exp(jnp.where(causal[:, :, None], diff, MASK_VALUE))
        att = jnp.einsum('bhtd,bhtsd,bhsd->bhts', qc, decay, kc)
        o = jnp.einsum('bhts,bhse->bhte', att, vc) + jnp.einsum('bhtd,bhde->bhte', qc * jnp.exp(b), s)
        b_end = b[:, :, -1:, :]
        s_new = (jnp.exp(b_end[:, :, 0, :, None]) * s
                 + jnp.einsum('bhsd,bhse->bhde', kc * jnp.exp(b_end - b), vc))
        return s_new, o

    s_fin, o = lax.scan(step, s0, (_chunks(q), _chunks(k), _chunks(v), _chunks(logf)))
    return _unchunks(o), s_fin


def _hgrn2_group(p_lat, p_ctx, lb, norm_w, need_ctx):
    def prep(p):
        B, L, _ = p.shape
        q, zf, zb, i, g = jnp.split(p.astype(jnp.float32), 5, axis=-1)

        def heads(t):
            return t.reshape(B, L, HG_HEADS, HG_DIM).transpose(0, 2, 1, 3)

        def forget(z, lbd):
            f = jax.nn.sigmoid(z) + lbd * jax.nn.sigmoid(-z)
            return heads(jnp.log(jnp.maximum(f, HG_F_MIN))), heads((1.0 - lbd) * jax.nn.sigmoid(-z))

        lf_f, k_f = forget(zf, lb[0])
        lf_b, k_b = forget(zb, lb[1])
        return heads(q), heads(i), k_f, lf_f, k_b, lf_b, g

    def readout(o, g):
        B, H, L, dv = o.shape
        o = _rms_norm(o, norm_w).transpose(0, 2, 1, 3).reshape(B, L, H * dv)
        return o * jax.nn.silu(g)

    qc, ic, kcf, lcf, kcb, lcb, gc = prep(p_ctx)
    s0 = jnp.zeros(qc.shape[:2] + (HG_DIM, HG_DIM), jnp.float32)
    oc_f, s_f = _hgrn2_scan(qc, kcf, ic, lcf, s0)
    oc_b, s_b = _hgrn2_scan(_flip(qc), _flip(kcb), _flip(ic), _flip(lcb), s0)
    ql, il, klf, llf, klb, llb, gl = prep(p_lat)
    ol_f, _ = _hgrn2_scan(ql, klf, il, llf, s_f)
    ol_b, _ = _hgrn2_scan(_flip(ql), _flip(klb), _flip(il), _flip(llb), s_b)
    y_lat = readout(ol_f + _flip(ol_b), gl)
    y_ctx = readout(oc_f + _flip(oc_b), gc) if need_ctx else None
    return y_lat, y_ctx


def _hyena_filters(L, w1, b1, w2, b2, w3, freq):
    f32 = jnp.float32
    pos = jnp.arange(L, dtype=f32)
    t = pos / (L - 1)
    bands = jnp.linspace(1e-4, HY_BANDS - 1, HY_BANDS, dtype=f32)
    ang = (2.0 * math.pi / L) * pos[:, None] * bands[None, :]
    z = jnp.concatenate([t[:, None], jnp.cos(ang), jnp.sin(ang)], axis=-1)
    freq = freq.astype(f32)
    hid = jnp.sin(freq[0] * (z @ w1.astype(f32) + b1.astype(f32)))
    hid = jnp.sin(freq[1] * (hid @ w2.astype(f32) + b2.astype(f32)))
    h = (hid @ w3.astype(f32)).reshape(L, 2, HY_ORDER, HY_WIDTH)
    deltas = jnp.abs(jnp.linspace(HY_MIN_DECAY, HY_MAX_DECAY, HY_WIDTH, dtype=f32))
    h = h * jnp.exp(-t[:, None] * deltas)[:, None, None, :]
    fwd, bwd = h[:, 0], h[:, 1]
    k = jnp.concatenate([fwd, jnp.zeros_like(fwd[:1]), bwd[:0:-1]], axis=0)
    return k / jnp.sum(jnp.abs(k), axis=0, keepdims=True)


def _fft_long_conv(z, k, d):
    n, L = k.shape[0], z.shape[1]
    spec = jnp.fft.rfft(z, n=n, axis=1) * jnp.fft.rfft(k, axis=0)[None]
    return jnp.fft.irfft(spec, n=n, axis=1)[:, :L] + z * d.astype(jnp.float32)


def _hyena_seq(p, conv_w, conv_b, w1, b1, w2, b2, w3, freq, dbias):
    u = _centred_dwconv(p, conv_w, conv_b).astype(jnp.float32)
    v, x1, x2 = jnp.split(u, 3, axis=-1)
    k = _hyena_filters(p.shape[1], w1, b1, w2, b2, w3, freq)
    z = x1 * _fft_long_conv(v, k[:, 0], dbias[0])
    return x2 * _fft_long_conv(z, k[:, 1], dbias[1])


def _natten_group(p_lat, p_ctx, rpb, need_ctx):
    f32 = jnp.float32
    B, L, _ = p_lat.shape

    def heads(p):
        return [t.reshape(t.shape[0], t.shape[1], NA_HEADS, NA_DIM) for t in jnp.split(p, 3, axis=-1)]

    q, k, v = heads(p_lat)
    qc, kc, vc = heads(p_ctx)
    scale = NA_DIM ** -0.5
    rows = L // GRID_W
    kh = min(NA_WIN_ROWS, rows)
    ncb = GRID_W // NA_QBLK
    nk = kh * NA_KBLK
    r = jnp.arange(rows)
    key_rows = jnp.clip(r - kh // 2, 0, rows - kh)[:, None] + jnp.arange(kh)
    blk = jnp.arange(ncb)
    qcol = blk[:, None] * NA_QBLK + jnp.arange(NA_QBLK)
    key_cols = (jnp.clip(blk * NA_QBLK - NA_WIN_COLS // 2, 0, GRID_W - NA_KBLK)[:, None]
                + jnp.arange(NA_KBLK))
    idx = (key_rows[:, None, :, None] * GRID_W + key_cols[None, :, None, :]).reshape(rows, ncb, nk)
    col_start = jnp.clip(qcol - NA_WIN_COLS // 2, 0, GRID_W - NA_WIN_COLS)[:, :, None, None]
    kcol = key_cols[:, None, None, :]
    valid = jnp.broadcast_to((kcol >= col_start) & (kcol < col_start + NA_WIN_COLS),
                             (ncb, NA_QBLK, kh, NA_KBLK)).reshape(ncb, NA_QBLK, nk)
    dc = jnp.clip(kcol - qcol[:, :, None, None], 1 - NA_WIN_COLS, NA_WIN_COLS - 1)
    dr = key_rows - r[:, None]
    bias = rpb.astype(f32)[:, dr[:, None, None, :, None] + NA_WIN_ROWS - 1, dc[None] + NA_WIN_COLS - 1]
    bias = jnp.moveaxis(bias.reshape(NA_HEADS, rows, ncb, NA_QBLK, nk), 1, 0)
    q_rows = jnp.moveaxis((q * scale).reshape(B, rows, ncb, NA_QBLK, NA_HEADS, NA_DIM), 1, 0)

    def row_block(inp):
        q_r, idx_r, bias_r = inp
        k_r, v_r = k[:, idx_r], v[:, idx_r]
        s_win = jnp.einsum('bjqhd,bjkhd->bhjqk', q_r, k_r).astype(f32)
        s_win = jnp.where(valid, s_win + bias_r, MASK_VALUE)
        s_ctx = jnp.einsum('bjqhd,bchd->bhjqc', q_r, kc).astype(f32)
        prob = jax.nn.softmax(jnp.concatenate([s_win, s_ctx], axis=-1), axis=-1).astype(v.dtype)
        return (jnp.einsum('bhjqk,bjkhd->bjqhd', prob[..., :nk], v_r)
                + jnp.einsum('bhjqc,bchd->bjqhd', prob[..., nk:], vc))

    out = lax.map(row_block, (q_rows, idx, bias))
    y_lat = jnp.moveaxis(out, 0, 1).reshape(B, L, GROUP_WIDTH)
    y_ctx = None
    if need_ctx:
        s = jnp.einsum('bqhd,bkhd->bhqk', qc * scale, kc).astype(f32)
        prob = jax.nn.softmax(s, axis=-1).astype(vc.dtype)
        y_ctx = jnp.einsum('bhqk,bkhd->bqhd', prob, vc).reshape(B, -1, GROUP_WIDTH)
    return y_lat, y_ctx


def _mlstm_scan(q, k, v, ig, lf, state):
    causal = jnp.tril(jnp.ones((CHUNK, CHUNK), dtype=bool))

    def step(carry, inp):
        ckv, n, m = carry
        qc, kc, vc, igc, lfc = inp
        g = jnp.cumsum(lfc, axis=-1)
        logd = jnp.where(causal, g[..., :, None] - g[..., None, :] + igc[..., None, :], MASK_VALUE)
        log_inter = g + m[..., None]
        m_out = jnp.maximum(log_inter, jnp.max(logd, axis=-1))
        dmat = jnp.exp(logd - m_out[..., None])
        w_inter = jnp.exp(log_inter - m_out)
        sc = jnp.einsum('bhtd,bhsd->bhts', qc, kc) * dmat
        num = jnp.einsum('bhts,bhse->bhte', sc, vc) + w_inter[..., None] * jnp.einsum('bhtd,bhde->bhte', qc, ckv)
        den = jnp.sum(sc, axis=-1) + w_inter * jnp.einsum('bhtd,bhd->bht', qc, n)
        h = num / jnp.maximum(jnp.abs(den), jnp.exp(-m_out))[..., None]
        g_end = g[..., -1]
        a = g_end[..., None] - g + igc
        m_new = jnp.maximum(g_end + m, jnp.max(a, axis=-1))
        carry_w = jnp.exp(g_end + m - m_new)
        w = jnp.exp(a - m_new[..., None])
        ckv = carry_w[..., None, None] * ckv + jnp.einsum('bhs,bhsd,bhse->bhde', w, kc, vc)
        n = carry_w[..., None] * n + jnp.einsum('bhs,bhsd->bhd', w, kc)
        return (ckv, n, m_new), h

    state, h = lax.scan(step, state, (_chunks(q), _chunks(k), _chunks(v), _chunks(ig), _chunks(lf)))
    return _unchunks(h), state


def _mlstm_group(p_lat, p_ctx, conv_w, conv_b, gate_b, need_ctx):
    W = GROUP_WIDTH
    f32 = jnp.float32

    def prep(p, rope):
        B, L, _ = p.shape
        qk = jax.nn.silu(_centred_dwconv(p[..., :2 * W], conv_w, conv_b)).astype(f32)
        q = qk[..., :W].reshape(B, L, ML_HEADS, ML_DIM)
        k = qk[..., W:].reshape(B, L, ML_HEADS, ML_DIM)
        if rope:
            q, k = _axial_rope(q), _axial_rope(k)
        v = p[..., 2 * W:3 * W].astype(f32).reshape(B, L, ML_HEADS, ML_DIM)
        og = jax.nn.sigmoid(p[..., 3 * W:4 * W].astype(f32))
        gates = (p[..., 4 * W:].astype(f32) + gate_b.astype(f32)).reshape(B, L, 4, ML_HEADS)
        gates = gates.transpose(2, 0, 3, 1)

        def hf(t):
            return t.transpose(0, 2, 1, 3)

        return (hf(q), hf(k * ML_DIM ** -0.5), hf(v), gates[0], jax.nn.log_sigmoid(gates[1]),
                gates[2], jax.nn.log_sigmoid(gates[3]), og)

    def readout(h, og):
        return og * h.transpose(0, 2, 1, 3).reshape(og.shape)

    qc, kc, vc, icf, lcf, icb, lcb, oc = prep(p_ctx, False)
    B = qc.shape[0]
    s0 = (jnp.zeros((B, ML_HEADS, ML_DIM, ML_DIM), f32), jnp.zeros((B, ML_HEADS, ML_DIM), f32),
          jnp.zeros((B, ML_HEADS), f32))
    hc_f, st_f = _mlstm_scan(qc, kc, vc, icf, lcf, s0)
    hc_b, st_b = _mlstm_scan(_flip(qc), _flip(kc), _flip(vc), _flip(icb), _flip(lcb), s0)
    ql, kl, vl, ilf, llf, ilb, llb, ol = prep(p_lat, True)
    hl_f, _ = _mlstm_scan(ql, kl, vl, ilf, llf, st_f)
    hl_b, _ = _mlstm_scan(_flip(ql), _flip(kl), _flip(vl), _flip(ilb), _flip(llb), st_b)
    y_lat = readout(hl_f + _flip(hl_b), ol)
    y_ctx = readout(hc_f + _flip(hc_b), oc) if need_ctx else None
    return y_lat, y_ctx


def _peer(h, wq, subkeys, u, v):
    B, L, D = h.shape
    n_tok = B * L
    tok = h.reshape(n_tok, D)
    q = _pmm(tok, wq).reshape(n_tok, PEER_HEADS, 2, PEER_DK // 2)
    s = jnp.einsum('thpd,hpkd->thpk', q, subkeys).astype(jnp.float32)
    sv, si = lax.top_k(s, PEER_TOPK)
    cand_s = (sv[:, :, 0, :, None] + sv[:, :, 1, None, :]).reshape(n_tok, PEER_HEADS, -1)
    cand_i = (si[:, :, 0, :, None] * PEER_NKEYS + si[:, :, 1, None, :]).reshape(n_tok, PEER_HEADS, -1)
    best_s, best_p = lax.top_k(cand_s, PEER_TOPK)
    ids = jnp.take_along_axis(cand_i, best_p, axis=-1)
    gate = jax.nn.softmax(best_s, axis=-1)
    nb = n_tok // PEER_TOKEN_BLOCK

    def block(inp):
        tb, ib, gb = inp
        a = jnp.einsum('td,tkd->tk', tb, u[ib])
        return jnp.einsum('tk,tkd->td', gb.astype(tb.dtype) * jax.nn.gelu(a, approximate=False), v[ib])

    out = lax.map(block, (tok.reshape(nb, PEER_TOKEN_BLOCK, D),
                          ids.reshape(nb, PEER_TOKEN_BLOCK, -1),
                          gate.reshape(nb, PEER_TOKEN_BLOCK, -1)))
    return out.reshape(B, L, D)


def _mm_kernel(a_ref, b_ref, o_ref):
    o_ref[...] = jnp.dot(a_ref[...].astype(jnp.bfloat16), b_ref[...].astype(jnp.bfloat16),
                         preferred_element_type=jnp.float32)


def _pmm(a, b, tm=512, tn=512):
    M, K = a.shape
    N = b.shape[1]
    tm = min(tm, M)
    return pl.pallas_call(
        _mm_kernel,
        grid=(pl.cdiv(M, tm), pl.cdiv(N, tn)),
        in_specs=[pl.BlockSpec((tm, K), lambda i, j: (i, 0)),
                  pl.BlockSpec((K, tn), lambda i, j: (0, j))],
        out_specs=pl.BlockSpec((tm, tn), lambda i, j: (i, j)),
        out_shape=jax.ShapeDtypeStruct((M, N), jnp.float32),
        compiler_params=pltpu.CompilerParams(vmem_limit_bytes=48 * 1024 * 1024),
    )(a, b)


def _proj(t, w):
    lead = t.shape[:-1]
    return _pmm(t.reshape(-1, t.shape[-1]), w).reshape(*lead, w.shape[1])


def kernel(x, c, ctx, c_ctx, w_ada, b_ada, w_in, w_out, ln_g, ln_b, hg_lower_bounds, hg_norm_w,
           hy_conv_w, hy_conv_b, hy_w1, hy_b1, hy_w2, hy_b2, hy_w3, hy_freq, hy_dbias, na_rpb,
           ml_conv_w, ml_conv_b, ml_gate_b, peer_wq, peer_subkeys, peer_u, peer_v):
    lb_soft = jax.nn.softmax(hg_lower_bounds.astype(jnp.float32), axis=0)
    lower_bounds = jnp.cumsum(lb_soft, axis=0) - lb_soft[0]
    cond_lat = jax.nn.silu(c)
    cond_ctx = jax.nn.silu(c_ctx)
    cuts = [HG_COLS, HG_COLS + HY_COLS, HG_COLS + HY_COLS + NA_COLS]
    xc = ctx
    for l in range(DEPTH):
        need_ctx = l < DEPTH - 1
        m_lat = jnp.split((cond_lat @ w_ada[l] + b_ada[l])[:, None, :], 6, axis=-1)
        m_ctx = jnp.split(cond_ctx @ w_ada[l] + b_ada[l], 6, axis=-1)
        pa_l, pb_l, pc_l, pd_l = jnp.split(_proj(x * (1 + m_lat[1]) + m_lat[0], w_in[l]), cuts, axis=-1)
        pa_c, pb_c, pc_c, pd_c = jnp.split(_proj(xc * (1 + m_ctx[1]) + m_ctx[0], w_in[l]), cuts, axis=-1)
        hy = (hy_conv_w[l], hy_conv_b[l], hy_w1[l], hy_b1[l], hy_w2[l], hy_b2[l], hy_w3[l], hy_freq[l], hy_dbias[l])
        a_l, a_c = _hgrn2_group(pa_l, pa_c, lower_bounds[l], hg_norm_w[l], need_ctx)
        b_l = _hyena_seq(pb_l, *hy)
        c_l, c_c = _natten_group(pc_l, pc_c, na_rpb[l], need_ctx)
        d_l, d_c = _mlstm_group(pd_l, pd_c, ml_conv_w[l], ml_conv_b[l], ml_gate_b[l], need_ctx)
        y_lat = _proj(jnp.concatenate([t.astype(x.dtype) for t in (a_l, b_l, c_l, d_l)], axis=-1), w_out[l])
        if need_ctx:
            b_c = _hyena_seq(pb_c, *hy)
            y_ctx = _proj(jnp.concatenate([t.astype(xc.dtype) for t in (a_c, b_c, c_c, d_c)], axis=-1), w_out[l])
            xc = _layer_norm(DEEPNORM_ALPHA * xc + m_ctx[2] * y_ctx, ln_g[l, 0], ln_b[l, 0])
        x = _layer_norm(DEEPNORM_ALPHA * x + m_lat[2] * y_lat, ln_g[l, 0], ln_b[l, 0])
        y_lat = _peer(x * (1 + m_lat[4]) + m_lat[3], peer_wq[l], peer_subkeys[l], peer_u[l], peer_v[l])
        x = _layer_norm(DEEPNORM_ALPHA * x + m_lat[5] * y_lat, ln_g[l, 1], ln_b[l, 1])
        if need_ctx:
            y_ctx = _peer(xc * (1 + m_ctx[4]) + m_ctx[3], peer_wq[l], peer_subkeys[l], peer_u[l], peer_v[l])
            xc = _layer_norm(DEEPNORM_ALPHA * xc + m_ctx[5] * y_ctx, ln_g[l, 1], ln_b[l, 1])
    return x
```

```python
import functools
import math
import jax
import jax.numpy as jnp
from jax import lax
from jax.experimental import pallas as pl
from jax.experimental.pallas import tpu as pltpu

D_MODEL = 2048
BATCH = 4
SEQ = 4096
DEPTH = 2

CTX_LEN = 256
GRID_W = 64
GROUP_WIDTH = D_MODEL // 4
MIX_WIDTH = 4 * GROUP_WIDTH
CHUNK = 64
HG_DIM = 128
HG_HEADS = GROUP_WIDTH // HG_DIM
HG_F_MIN = 1e-30
HY_WIDTH = GROUP_WIDTH
HY_ORDER = 2
HY_SHORT = 3
HY_BANDS = 16
HY_EMB = 1 + 2 * HY_BANDS
HY_HIDDEN = 64
HY_FAST_DECAY = 0.3
HY_SLOW_DECAY = 1.5
HY_DECAY_TARGET = 1e-2
HY_MIN_DECAY = math.log(HY_DECAY_TARGET) / HY_SLOW_DECAY
HY_MAX_DECAY = math.log(HY_DECAY_TARGET) / HY_FAST_DECAY
NA_DIM = 64
NA_HEADS = GROUP_WIDTH // NA_DIM
NA_WIN_ROWS = 8
NA_WIN_COLS = 16
NA_QBLK = 16
NA_KBLK = 32
ML_DIM = 128
ML_HEADS = GROUP_WIDTH // ML_DIM
ML_SHORT = 3
ROPE_THETA = 10000.0
PEER_HEADS = 8
PEER_NKEYS = 128
PEER_EXPERTS = PEER_NKEYS * PEER_NKEYS
PEER_DK = 256
PEER_TOPK = 16
PEER_TOKEN_BLOCK = 64
HG_COLS = 5 * GROUP_WIDTH
HY_COLS = 3 * GROUP_WIDTH
NA_COLS = 3 * GROUP_WIDTH
ML_COLS = 4 * GROUP_WIDTH + 4 * ML_HEADS
IN_WIDTH = HG_COLS + HY_COLS + NA_COLS + ML_COLS
DEEPNORM_ALPHA = (2 * DEPTH) ** 0.25
DEEPNORM_BETA = (8 * DEPTH) ** -0.25
MASK_VALUE = -1e30


def _layer_norm(x, g, b, eps=1e-5):
    xf = x.astype(jnp.float32)
    mu = jnp.mean(xf, axis=-1, keepdims=True)
    var = jnp.mean(jnp.square(xf - mu), axis=-1, keepdims=True)
    y = (xf - mu) * lax.rsqrt(var + eps) * g.astype(jnp.float32) + b.astype(jnp.float32)
    return y.astype(x.dtype)


def _rms_norm(x, g, eps=1e-6):
    xf = x.astype(jnp.float32)
    return xf * lax.rsqrt(jnp.mean(jnp.square(xf), axis=-1, keepdims=True) + eps) * g.astype(jnp.float32)


def _centred_dwconv(x, w, b):
    k = w.shape[0]
    y = lax.conv_general_dilated(x, w[:, None, :].astype(x.dtype), window_strides=(1,),
                                 padding=((k // 2, k // 2),), dimension_numbers=('NWC', 'WIO', 'NWC'),
                                 feature_group_count=x.shape[-1])
    return y + b.astype(x.dtype)


def _axial_rope(x):
    L, dh = x.shape[1], x.shape[-1]
    half, quarter = dh // 2, dh // 4
    t = jnp.arange(L)
    inv_freq = ROPE_THETA ** (-jnp.arange(quarter, dtype=jnp.float32) / quarter)

    def rot(xa, pos):
        ang = pos.astype(jnp.float32)[:, None] * inv_freq
        cos = jnp.cos(ang)[None, :, None, :]
        sin = jnp.sin(ang)[None, :, None, :]
        x1, x2 = xa[..., :quarter], xa[..., quarter:]
        return jnp.concatenate([x1 * cos - x2 * sin, x2 * cos + x1 * sin], axis=-1)

    return jnp.concatenate([rot(x[..., :half], t // GRID_W), rot(x[..., half:], t % GRID_W)], axis=-1)


def _chunks(x):
    b, h, L = x.shape[:3]
    return jnp.moveaxis(x.reshape(b, h, L // CHUNK, CHUNK, *x.shape[3:]), 2, 0)


def _unchunks(x):
    x = jnp.moveaxis(x, 0, 2)
    return x.reshape(x.shape[0], x.shape[1], -1, *x.shape[4:])


def _flip(t):
    return jnp.flip(t, axis=2)


def _hgrn2_scan(q, k, v, logf, s0):
    causal = jnp.tril(jnp.ones((CHUNK, CHUNK), dtype=bool))

    def step(s, inp):
        qc, kc, vc, lfc = inp
        b = jnp.cumsum(lfc, axis=2)
        diff = b[:, :, :, None, :] - b[:, :, None, :, :]
        decay = jnp.exp(jnp.where(causal[:, :, None], diff, MASK_VALUE))
        att = jnp.einsum('bhtd,bhtsd,bhsd->bhts', qc, decay, kc)
        o = jnp.einsum('bhts,bhse->bhte', att, vc) + jnp.einsum('bhtd,bhde->bhte', qc * jnp.exp(b), s)
        b_end = b[:, :, -1:, :]
        s_new = (jnp.exp(b_end[:, :, 0, :, None]) * s
                 + jnp.einsum('bhsd,bhse->bhde', kc * jnp.exp(b_end - b), vc))
        return s_new, o

    s_fin, o = lax.scan(step, s0, (_chunks(q), _chunks(k), _chunks(v), _chunks(logf)))
    return _unchunks(o), s_fin


def _hgrn2_group(p_lat, p_ctx, lb, norm_w, need_ctx):
    def prep(p):
        B, L, _ = p.shape
        q, zf, zb, i, g = jnp.split(p.astype(jnp.float32), 5, axis=-1)

        def heads(t):
            return t.reshape(B, L, HG_HEADS, HG_DIM).transpose(0, 2, 1, 3)

        def forget(z, lbd):
            f = jax.nn.sigmoid(z) + lbd * jax.nn.sigmoid(-z)
            return heads(jnp.log(jnp.maximum(f, HG_F_MIN))), heads((1.0 - lbd) * jax.nn.sigmoid(-z))

        lf_f, k_f = forget(zf, lb[0])
        lf_b, k_b = forget(zb, lb[1])
        return heads(q), heads(i), k_f, lf_f, k_b, lf_b, g

    def readout(o, g):
        B, H, L, dv = o.shape
        o = _rms_norm(o, norm_w).transpose(0, 2, 1, 3).reshape(B, L, H * dv)
        return o * jax.nn.silu(g)

    qc, ic, kcf, lcf, kcb, lcb, gc = prep(p_ctx)
    s0 = jnp.zeros(qc.shape[:2] + (HG_DIM, HG_DIM), jnp.float32)
    oc_f, s_f = _hgrn2_scan(qc, kcf, ic, lcf, s0)
    oc_b, s_b = _hgrn2_scan(_flip(qc), _flip(kcb), _flip(ic), _flip(lcb), s0)
    ql, il, klf, llf, klb, llb, gl = prep(p_lat)
    ol_f, _ = _hgrn2_scan(ql, klf, il, llf, s_f)
    ol_b, _ = _hgrn2_scan(_flip(ql), _flip(klb), _flip(il), _flip(llb), s_b)
    y_lat = readout(ol_f + _flip(ol_b), gl)
    y_ctx = readout(oc_f + _flip(oc_b), gc) if need_ctx else None
    return y_lat, y_ctx


def _hyena_filters(L, w1, b1, w2, b2, w3, freq):
    f32 = jnp.float32
    pos = jnp.arange(L, dtype=f32)
    t = pos / (L - 1)
    bands = jnp.linspace(1e-4, HY_BANDS - 1, HY_BANDS, dtype=f32)
    ang = (2.0 * math.pi / L) * pos[:, None] * bands[None, :]
    z = jnp.concatenate([t[:, None], jnp.cos(ang), jnp.sin(ang)], axis=-1)
    freq = freq.astype(f32)
    hid = jnp.sin(freq[0] * (z @ w1.astype(f32) + b1.astype(f32)))
    hid = jnp.sin(freq[1] * (hid @ w2.astype(f32) + b2.astype(f32)))
    h = (hid @ w3.astype(f32)).reshape(L, 2, HY_ORDER, HY_WIDTH)
    deltas = jnp.abs(jnp.linspace(HY_MIN_DECAY, HY_MAX_DECAY, HY_WIDTH, dtype=f32))
    h = h * jnp.exp(-t[:, None] * deltas)[:, None, None, :]
    fwd, bwd = h[:, 0], h[:, 1]
    k = jnp.concatenate([fwd, jnp.zeros_like(fwd[:1]), bwd[:0:-1]], axis=0)
    return k / jnp.sum(jnp.abs(k), axis=0, keepdims=True)


def _fft_long_conv(z, k, d):
    n, L = k.shape[0], z.shape[1]
    spec = jnp.fft.rfft(z, n=n, axis=1) * jnp.fft.rfft(k, axis=0)[None]
    return jnp.fft.irfft(spec, n=n, axis=1)[:, :L] + z * d.astype(jnp.float32)


def _hyena_seq(p, conv_w, conv_b, w1, b1, w2, b2, w3, freq, dbias):
    u = _centred_dwconv(p, conv_w, conv_b).astype(jnp.float32)
    v, x1, x2 = jnp.split(u, 3, axis=-1)
    k = _hyena_filters(p.shape[1], w1, b1, w2, b2, w3, freq)
    z = x1 * _fft_long_conv(v, k[:, 0], dbias[0])
    return x2 * _fft_long_conv(z, k[:, 1], dbias[1])


def _natten_group(p_lat, p_ctx, rpb, need_ctx):
    f32 = jnp.float32
    B, L, _ = p_lat.shape

    def heads(p):
        return [t.reshape(t.shape[0], t.shape[1], NA_HEADS, NA_DIM) for t in jnp.split(p, 3, axis=-1)]

    q, k, v = heads(p_lat)
    qc, kc, vc = heads(p_ctx)
    scale = NA_DIM ** -0.5
    rows = L // GRID_W
    kh = min(NA_WIN_ROWS, rows)
    ncb = GRID_W // NA_QBLK
    nk = kh * NA_KBLK
    r = jnp.arange(rows)
    key_rows = jnp.clip(r - kh // 2, 0, rows - kh)[:, None] + jnp.arange(kh)
    blk = jnp.arange(ncb)
    qcol = blk[:, None] * NA_QBLK + jnp.arange(NA_QBLK)
    key_cols = (jnp.clip(blk * NA_QBLK - NA_WIN_COLS // 2, 0, GRID_W - NA_KBLK)[:, None]
                + jnp.arange(NA_KBLK))
    idx = (key_rows[:, None, :, None] * GRID_W + key_cols[None, :, None, :]).reshape(rows, ncb, nk)
    col_start = jnp.clip(qcol - NA_WIN_COLS // 2, 0, GRID_W - NA_WIN_COLS)[:, :, None, None]
    kcol = key_cols[:, None, None, :]
    valid = jnp.broadcast_to((kcol >= col_start) & (kcol < col_start + NA_WIN_COLS),
                             (ncb, NA_QBLK, kh, NA_KBLK)).reshape(ncb, NA_QBLK, nk)
    dc = jnp.clip(kcol - qcol[:, :, None, None], 1 - NA_WIN_COLS, NA_WIN_COLS - 1)
    dr = key_rows - r[:, None]
    bias = rpb.astype(f32)[:, dr[:, None, None, :, None] + NA_WIN_ROWS - 1, dc[None] + NA_WIN_COLS - 1]
    bias = jnp.moveaxis(bias.reshape(NA_HEADS, rows, ncb, NA_QBLK, nk), 1, 0)
    q_rows = jnp.moveaxis((q * scale).reshape(B, rows, ncb, NA_QBLK, NA_HEADS, NA_DIM), 1, 0)

    def row_block(inp):
        q_r, idx_r, bias_r = inp
        k_r, v_r = k[:, idx_r], v[:, idx_r]
        s_win = jnp.einsum('bjqhd,bjkhd->bhjqk', q_r, k_r).astype(f32)
        s_win = jnp.where(valid, s_win + bias_r, MASK_VALUE)
        s_ctx = jnp.einsum('bjqhd,bchd->bhjqc', q_r, kc).astype(f32)
        prob = jax.nn.softmax(jnp.concatenate([s_win, s_ctx], axis=-1), axis=-1).astype(v.dtype)
        return (jnp.einsum('bhjqk,bjkhd->bjqhd', prob[..., :nk], v_r)
                + jnp.einsum('bhjqc,bchd->bjqhd', prob[..., nk:], vc))

    out = lax.map(row_block, (q_rows, idx, bias))
    y_lat = jnp.moveaxis(out, 0, 1).reshape(B, L, GROUP_WIDTH)
    y_ctx = None
    if need_ctx:
        s = jnp.einsum('bqhd,bkhd->bhqk', qc * scale, kc).astype(f32)
        prob = jax.nn.softmax(s, axis=-1).astype(vc.dtype)
        y_ctx = jnp.einsum('bhqk,bkhd->bqhd', prob, vc).reshape(B, -1, GROUP_WIDTH)
    return y_lat, y_ctx


def _na_pair_attention(q_pair, key_sets, lane):
    f32 = jnp.float32
    nt = (((1,), (1,)), ((), ()))
    outs = []
    for half in range(2):
        own = (lane >= NA_DIM) if half else (lane < NA_DIM)
        q = jnp.where(own, q_pair, jnp.zeros_like(q_pair))
        scores = []
        for k_pair, _, add, valid in key_sets:
            s = lax.dot_general(q, k_pair, nt, preferred_element_type=f32) * (NA_DIM ** -0.5)
            if add is not None:
                s = jnp.where(valid, s + add[half], MASK_VALUE)
            scores.append(s)
        m = scores[0].max(axis=-1, keepdims=True)
        for s in scores[1:]:
            m = jnp.maximum(m, s.max(axis=-1, keepdims=True))
        den = jnp.zeros_like(m)
        acc = jnp.zeros(q_pair.shape, f32)
        for s, (_, v_pair, _, _) in zip(scores, key_sets):
            p = jnp.exp(s - m)
            den = den + p.sum(axis=-1, keepdims=True)
            acc = acc + jnp.dot(p.astype(jnp.bfloat16), v_pair, preferred_element_type=f32)
        outs.append(acc / den)
    return jnp.where(lane < NA_DIM, outs[0], outs[1])


def _natten_lat_kernel(q_ref, k_ref, v_ref, kc_ref, vc_ref, bias_ref, o_ref, *, rows):
    kh = NA_WIN_ROWS
    r = pl.program_id(1)
    start = pl.multiple_of(jnp.clip(r - kh // 2, 0, rows - kh) * GRID_W, GRID_W)
    nk = kh * GRID_W
    qcol = lax.broadcasted_iota(jnp.int32, (GRID_W, nk), 0)
    kcol = lax.broadcasted_iota(jnp.int32, (GRID_W, nk), 1) % GRID_W
    col_start = jnp.clip(qcol - NA_WIN_COLS // 2, 0, GRID_W - NA_WIN_COLS)
    valid = (kcol >= col_start) & (kcol < col_start + NA_WIN_COLS)
    lane = lax.broadcasted_iota(jnp.int32, (GRID_W, 2 * NA_DIM), 1)
    for hp in range(NA_HEADS // 2):
        cs = slice(hp * 2 * NA_DIM, (hp + 1) * 2 * NA_DIM)
        win = (k_ref[pl.ds(start, nk), cs], v_ref[pl.ds(start, nk), cs],
               (bias_ref[0, 2 * hp], bias_ref[0, 2 * hp + 1]), valid)
        ctx = (kc_ref[:, cs], vc_ref[:, cs], None, None)
        o_ref[:, cs] = _na_pair_attention(q_ref[:, cs], [win, ctx], lane).astype(o_ref.dtype)


def _natten_ctx_kernel(q_ref, k_ref, v_ref, o_ref):
    lane = lax.broadcasted_iota(jnp.int32, (q_ref.shape[0], 2 * NA_DIM), 1)
    for hp in range(NA_HEADS // 2):
        cs = slice(hp * 2 * NA_DIM, (hp + 1) * 2 * NA_DIM)
        o_ref[:, cs] = _na_pair_attention(q_ref[:, cs], [(k_ref[:, cs], v_ref[:, cs], None, None)],
                                          lane).astype(o_ref.dtype)


def _natten_bias_table(rpb, rows):
    kh = NA_WIN_ROWS
    dc = jnp.clip(jnp.arange(GRID_W)[None, :] - jnp.arange(GRID_W)[:, None], 1 - NA_WIN_COLS, NA_WIN_COLS - 1)
    t = rpb.astype(jnp.float32)[:, :, dc + NA_WIN_COLS - 1]
    per_off = [t[:, d:d + kh].transpose(0, 2, 1, 3).reshape(NA_HEADS, GRID_W, kh * GRID_W) for d in range(kh)]
    return jnp.stack(per_off, axis=0)


def _natten_pallas(p_lat, p_ctx, rpb, need_ctx, batch):
    W = GROUP_WIDTH
    L = p_lat.shape[0] // batch
    n_ctx = p_ctx.shape[0] // batch
    rows = L // GRID_W
    kh = NA_WIN_ROWS
    assert rows >= kh and L % GRID_W == 0
    bias = _natten_bias_table(rpb, rows)

    def first_key_row_offset(b, r):
        return (jnp.clip(r - kh // 2, 0, rows - kh) - r + kh - 1, 0, 0, 0)

    y_lat = pl.pallas_call(
        functools.partial(_natten_lat_kernel, rows=rows),
        grid=(batch, rows),
        in_specs=[pl.BlockSpec((GRID_W, W), lambda b, r: (b * rows + r, 0)),
                  pl.BlockSpec((L, W), lambda b, r: (b, 1)),
                  pl.BlockSpec((L, W), lambda b, r: (b, 2)),
                  pl.BlockSpec((n_ctx, W), lambda b, r: (b, 1)),
                  pl.BlockSpec((n_ctx, W), lambda b, r: (b, 2)),
                  pl.BlockSpec((1, NA_HEADS, GRID_W, kh * GRID_W), first_key_row_offset)],
        out_specs=pl.BlockSpec((GRID_W, W), lambda b, r: (b * rows + r, 0)),
        out_shape=jax.ShapeDtypeStruct((batch * L, W), jnp.bfloat16),
        compiler_params=pltpu.CompilerParams(vmem_limit_bytes=40 * 1024 * 1024),
        name="natten_lat",
    )(p_lat, p_lat, p_lat, p_ctx, p_ctx, bias)
    y_ctx = None
    if need_ctx:
        y_ctx = pl.pallas_call(
            _natten_ctx_kernel,
            grid=(batch,),
            in_specs=[pl.BlockSpec((n_ctx, W), lambda b: (b, 0)),
                      pl.BlockSpec((n_ctx, W), lambda b: (b, 1)),
                      pl.BlockSpec((n_ctx, W), lambda b: (b, 2))],
            out_specs=pl.BlockSpec((n_ctx, W), lambda b: (b, 0)),
            out_shape=jax.ShapeDtypeStruct((batch * n_ctx, W), jnp.bfloat16),
            name="natten_ctx",
        )(p_ctx, p_ctx, p_ctx)
    return y_lat, y_ctx


def _mlstm_scan(q, k, v, ig, lf, state):
    causal = jnp.tril(jnp.ones((CHUNK, CHUNK), dtype=bool))

    def step(carry, inp):
        ckv, n, m = carry
        qc, kc, vc, igc, lfc = inp
        g = jnp.cumsum(lfc, axis=-1)
        logd = jnp.where(causal, g[..., :, None] - g[..., None, :] + igc[..., None, :], MASK_VALUE)
        log_inter = g + m[..., None]
        m_out = jnp.maximum(log_inter, jnp.max(logd, axis=-1))
        dmat = jnp.exp(logd - m_out[..., None])
        w_inter = jnp.exp(log_inter - m_out)
        sc = jnp.einsum('bhtd,bhsd->bhts', qc, kc) * dmat
        num = jnp.einsum('bhts,bhse->bhte', sc, vc) + w_inter[..., None] * jnp.einsum('bhtd,bhde->bhte', qc, ckv)
        den = jnp.sum(sc, axis=-1) + w_inter * jnp.einsum('bhtd,bhd->bht', qc, n)
        h = num / jnp.maximum(jnp.abs(den), jnp.exp(-m_out))[..., None]
        g_end = g[..., -1]
        a = g_end[..., None] - g + igc
        m_new = jnp.maximum(g_end + m, jnp.max(a, axis=-1))
        carry_w = jnp.exp(g_end + m - m_new)
        w = jnp.exp(a - m_new[..., None])
        ckv = carry_w[..., None, None] * ckv + jnp.einsum('bhs,bhsd,bhse->bhde', w, kc, vc)
        n = carry_w[..., None] * n + jnp.einsum('bhs,bhsd->bhd', w, kc)
        return (ckv, n, m_new), h

    state, h = lax.scan(step, state, (_chunks(q), _chunks(k), _chunks(v), _chunks(ig), _chunks(lf)))
    return _unchunks(h), state


def _mlstm_group(p_lat, p_ctx, conv_w, conv_b, gate_b, need_ctx):
    W = GROUP_WIDTH
    f32 = jnp.float32

    def prep(p, rope):
        B, L, _ = p.shape
        qk = jax.nn.silu(_centred_dwconv(p[..., :2 * W], conv_w, conv_b)).astype(f32)
        q = qk[..., :W].reshape(B, L, ML_HEADS, ML_DIM)
        k = qk[..., W:].reshape(B, L, ML_HEADS, ML_DIM)
        if rope:
            q, k = _axial_rope(q), _axial_rope(k)
        v = p[..., 2 * W:3 * W].astype(f32).reshape(B, L, ML_HEADS, ML_DIM)
        og = jax.nn.sigmoid(p[..., 3 * W:4 * W].astype(f32))
        gates = (p[..., 4 * W:].astype(f32) + gate_b.astype(f32)).reshape(B, L, 4, ML_HEADS)
        gates = gates.transpose(2, 0, 3, 1)

        def hf(t):
            return t.transpose(0, 2, 1, 3)

        return (hf(q), hf(k * ML_DIM ** -0.5), hf(v), gates[0], jax.nn.log_sigmoid(gates[1]),
                gates[2], jax.nn.log_sigmoid(gates[3]), og)

    def readout(h, og):
        return og * h.transpose(0, 2, 1, 3).reshape(og.shape)

    qc, kc, vc, icf, lcf, icb, lcb, oc = prep(p_ctx, False)
    B = qc.shape[0]
    s0 = (jnp.zeros((B, ML_HEADS, ML_DIM, ML_DIM), f32), jnp.zeros((B, ML_HEADS, ML_DIM), f32),
          jnp.zeros((B, ML_HEADS), f32))
    hc_f, st_f = _mlstm_scan(qc, kc, vc, icf, lcf, s0)
    hc_b, st_b = _mlstm_scan(_flip(qc), _flip(kc), _flip(vc), _flip(icb), _flip(lcb), s0)
    ql, kl, vl, ilf, llf, ilb, llb, ol = prep(p_lat, True)
    hl_f, _ = _mlstm_scan(ql, kl, vl, ilf, llf, st_f)
    hl_b, _ = _mlstm_scan(_flip(ql), _flip(kl), _flip(vl), _flip(ilb), _flip(llb), st_b)
    y_lat = readout(hl_f + _flip(hl_b), ol)
    y_ctx = readout(hc_f + _flip(hc_b), oc) if need_ctx else None
    return y_lat, y_ctx


PEER_SUB_DK = PEER_DK // 2
PEER_CAND_ROWS = 16 + 8 + 6 * 8 + 8


def _topk_rows(s, order, payload, k):
    sentinel = jnp.int32(2 ** 30)
    vals, picked = [], []
    for _ in range(k):
        m = s.max(axis=0, keepdims=True)
        first = jnp.min(jnp.where(s == m, order, sentinel), axis=0, keepdims=True)
        hit = order == first
        vals.append(m)
        picked.append(first if payload is None else jnp.max(jnp.where(hit, payload, -1), axis=0, keepdims=True))
        s = jnp.where(hit, -jnp.inf, s)
    return jnp.concatenate(vals, axis=0), jnp.concatenate(picked, axis=0)


def _peer_candidates(a0, a1, combine):
    pieces = [combine(a0[0:1], a1), combine(a0[1:2], a1[0:8])]
    pieces += [combine(a0[i:i + 1], a1[0:8]) for i in range(2, 8)]
    pieces.append(combine(a0[8:16], a1[0:1]))
    return jnp.concatenate(pieces, axis=0)


def _peer_topk_kernel(x_ref, sc_ref, sh_ref, wq_ref, sk_ref, h_ref, ids_ref, gate_ref):
    f32 = jnp.float32
    tb = x_ref.shape[0]
    h = x_ref[...] * sc_ref[0] + sh_ref[0]
    h_ref[...] = h.astype(h_ref.dtype)
    q = jnp.dot(h.astype(jnp.bfloat16), wq_ref[...], preferred_element_type=f32).astype(jnp.bfloat16)
    key_order = lax.broadcasted_iota(jnp.int32, (PEER_NKEYS, tb), 0)
    rho = lax.broadcasted_iota(jnp.int32, (PEER_CAND_ROWS, tb), 0)
    ci = jnp.where(rho < 16, 0, jnp.where(rho < 24, 1, jnp.where(rho < 72, 2 + ((rho - 24) >> 3), rho - 64)))
    cj = jnp.where(rho < 16, rho, jnp.where(rho < 24, rho - 16, jnp.where(rho < 72, (rho - 24) & 7, 0)))
    cand_order = ci * PEER_TOPK + cj
    cand_valid = (ci + 1) * (cj + 1) <= PEER_TOPK
    nt = (((1,), (1,)), ((), ()))
    for head in range(PEER_HEADS):
        sv, si = [], []
        for half in range(2):
            c0 = (head * 2 + half) * PEER_SUB_DK
            s_t = lax.dot_general(sk_ref[head * 2 + half], q[:, c0:c0 + PEER_SUB_DK], nt,
                                  preferred_element_type=f32)
            v, i = _topk_rows(s_t, key_order, None, PEER_TOPK)
            sv.append(v)
            si.append(i)
        cand_s = _peer_candidates(sv[0], sv[1], lambda a, b: a + b)
        cand_s = jnp.where(cand_valid, cand_s, -jnp.inf)
        cand_id = _peer_candidates(si[0], si[1], lambda a, b: a * PEER_NKEYS + b)
        best_s, ids = _topk_rows(cand_s, cand_order, cand_id, PEER_TOPK)
        e = jnp.exp(best_s - best_s[0:1])
        rs = slice(head * PEER_TOPK, (head + 1) * PEER_TOPK)
        ids_ref[rs, :] = ids
        gate_ref[rs, :] = e / e.sum(axis=0, keepdims=True)


def _peer_route(x2d, scale, shift, rows_per_mod, wq, subkeys, tb=256):
    T, D = x2d.shape
    nk = PEER_HEADS * PEER_TOPK
    sk = subkeys.reshape(PEER_HEADS * 2, PEER_NKEYS, PEER_SUB_DK).astype(jnp.bfloat16)
    mod = lambda i: (i * tb // rows_per_mod, 0, 0)
    h, ids_t, gate_t = pl.pallas_call(
        _peer_topk_kernel,
        grid=(T // tb,),
        in_specs=[pl.BlockSpec((tb, D), lambda i: (i, 0)),
                  pl.BlockSpec((1, 1, D), mod),
                  pl.BlockSpec((1, 1, D), mod),
                  pl.BlockSpec(wq.shape, lambda i: (0, 0)),
                  pl.BlockSpec(sk.shape, lambda i: (0, 0, 0))],
        out_specs=[pl.BlockSpec((tb, D), lambda i: (i, 0)),
                   pl.BlockSpec((nk, tb), lambda i: (0, i)),
                   pl.BlockSpec((nk, tb), lambda i: (0, i))],
        out_shape=[jax.ShapeDtypeStruct((T, D), jnp.float32),
                   jax.ShapeDtypeStruct((nk, T), jnp.int32),
                   jax.ShapeDtypeStruct((nk, T), jnp.float32)],
        compiler_params=pltpu.CompilerParams(vmem_limit_bytes=48 * 1024 * 1024),
        name="peer_route",
    )(x2d, scale, shift, wq.astype(jnp.bfloat16), sk)
    return h, ids_t.T, gate_t.T


def _peer(x2d, scale, shift, rows_per_mod, wq, subkeys, u, v):
    tok, ids, gate = _peer_route(x2d, scale, shift, rows_per_mod, wq, subkeys)
    n_tok, D = tok.shape
    nb = n_tok // PEER_TOKEN_BLOCK

    def block(inp):
        tb, ib, gb = inp
        a = jnp.einsum('td,tkd->tk', tb, u[ib])
        return jnp.einsum('tk,tkd->td', gb.astype(tb.dtype) * jax.nn.gelu(a, approximate=False), v[ib])

    out = lax.map(block, (tok.reshape(nb, PEER_TOKEN_BLOCK, D),
                          ids.reshape(nb, PEER_TOKEN_BLOCK, -1),
                          gate.reshape(nb, PEER_TOKEN_BLOCK, -1)))
    return out.reshape(n_tok, D)


def _in_proj_kernel(x_ref, sc_ref, sh_ref, w_ref, o_ref, a_scr):
    @pl.when(pl.program_id(1) == 0)
    def _():
        a_scr[...] = (x_ref[...] * sc_ref[0] + sh_ref[0]).astype(a_scr.dtype)

    o_ref[...] = jnp.dot(a_scr[...], w_ref[...], preferred_element_type=jnp.float32).astype(o_ref.dtype)


def _in_proj(x2d, scale, shift, rows_per_mod, w, col0, ncols, out_dtype, tm=1024, tn=512):
    M, K = x2d.shape
    tm = min(tm, M)
    assert col0 % tn == 0 and M % tm == 0
    mod = lambda i, j: (i * tm // rows_per_mod, 0, 0)
    return pl.pallas_call(
        _in_proj_kernel,
        grid=(M // tm, pl.cdiv(ncols, tn)),
        in_specs=[pl.BlockSpec((tm, K), lambda i, j: (i, 0)),
                  pl.BlockSpec((1, 1, K), mod),
                  pl.BlockSpec((1, 1, K), mod),
                  pl.BlockSpec((K, tn), lambda i, j: (0, col0 // tn + j))],
        out_specs=pl.BlockSpec((tm, tn), lambda i, j: (i, j)),
        out_shape=jax.ShapeDtypeStruct((M, ncols), out_dtype),
        scratch_shapes=[pltpu.VMEM((tm, K), jnp.bfloat16)],
        compiler_params=pltpu.CompilerParams(dimension_semantics=("arbitrary", "arbitrary"),
                                             vmem_limit_bytes=48 * 1024 * 1024),
        name="in_proj",
    )(x2d, scale, shift, w)


def _out_proj_ln_kernel(a_ref, b_ref, c_ref, d_ref, w_ref, x_ref, gt_ref, g_ref, bt_ref, o_ref):
    f32 = jnp.float32
    W = GROUP_WIDTH
    y = jnp.zeros(x_ref.shape, f32)
    for g, r in enumerate((a_ref, b_ref, c_ref, d_ref)):
        y = y + jnp.dot(r[...].astype(jnp.bfloat16), w_ref[g * W:(g + 1) * W, :], preferred_element_type=f32)
    z = DEEPNORM_ALPHA * x_ref[...] + gt_ref[0] * y
    mu = jnp.mean(z, axis=-1, keepdims=True)
    var = jnp.mean(jnp.square(z - mu), axis=-1, keepdims=True)
    o_ref[...] = (z - mu) * lax.rsqrt(var + 1e-5) * g_ref[...] + bt_ref[...]


def _out_proj_ln(parts, w, x2d, gate, rows_per_mod, ln_g, ln_b, tm=256):
    M, D = x2d.shape
    W = GROUP_WIDTH
    tm = min(tm, M)
    row = lambda i: (i, 0)
    return pl.pallas_call(
        _out_proj_ln_kernel,
        grid=(M // tm,),
        in_specs=[pl.BlockSpec((tm, W), row)] * 4
                 + [pl.BlockSpec(w.shape, lambda i: (0, 0)),
                    pl.BlockSpec((tm, D), row),
                    pl.BlockSpec((1, 1, D), lambda i: (i * tm // rows_per_mod, 0, 0)),
                    pl.BlockSpec((1, D), lambda i: (0, 0)),
                    pl.BlockSpec((1, D), lambda i: (0, 0))],
        out_specs=pl.BlockSpec((tm, D), row),
        out_shape=jax.ShapeDtypeStruct((M, D), jnp.float32),
        compiler_params=pltpu.CompilerParams(vmem_limit_bytes=48 * 1024 * 1024),
        name="out_proj_ln",
    )(*parts, w, x2d, gate, ln_g.reshape(1, D), ln_b.reshape(1, D))


def kernel(x, c, ctx, c_ctx, w_ada, b_ada, w_in, w_out, ln_g, ln_b, hg_lower_bounds, hg_norm_w,
           hy_conv_w, hy_conv_b, hy_w1, hy_b1, hy_w2, hy_b2, hy_w3, hy_freq, hy_dbias, na_rpb,
           ml_conv_w, ml_conv_b, ml_gate_b, peer_wq, peer_subkeys, peer_u, peer_v):
    f32, bf16 = jnp.float32, jnp.bfloat16
    B, L, D = x.shape
    n_ctx = ctx.shape[1]
    lb_soft = jax.nn.softmax(hg_lower_bounds.astype(f32), axis=0)
    lower_bounds = jnp.cumsum(lb_soft, axis=0) - lb_soft[0]
    cond_lat = jax.nn.silu(c)
    cond_ctx = jax.nn.silu(c_ctx)
    c0_na = HG_COLS + HY_COLS
    c0_ml = c0_na + NA_COLS
    x2 = x.reshape(B * L, D)
    xc2 = ctx.reshape(B * n_ctx, D)
    for l in range(DEPTH):
        need_ctx = l < DEPTH - 1
        m_lat = (cond_lat @ w_ada[l] + b_ada[l]).reshape(B, 1, 6, D)
        m_ctx = (cond_ctx @ w_ada[l] + b_ada[l]).reshape(1, 1, 6, D)
        ml = [m_lat[:, :, k] for k in range(6)]
        mc = [m_ctx[:, :, k] for k in range(6)]
        w_in_l = w_in[l].astype(bf16)
        w_out_l = w_out[l].astype(bf16)

        def in_proj(t, m, rows_per_mod):
            sc, sh = 1 + m[1], m[0]
            return (_in_proj(t, sc, sh, rows_per_mod, w_in_l, 0, c0_na, f32),
                    _in_proj(t, sc, sh, rows_per_mod, w_in_l, c0_na, NA_COLS, bf16),
                    _in_proj(t, sc, sh, rows_per_mod, w_in_l, c0_ml, ML_COLS, f32))

        ph_l, pn_l, pm_l = in_proj(x2, ml, L)
        ph_c, pn_c, pm_c = in_proj(xc2, mc, B * n_ctx)
        pa_l, pb_l, pd_l = (ph_l[:, :HG_COLS].reshape(B, L, -1), ph_l[:, HG_COLS:].reshape(B, L, -1),
                            pm_l.reshape(B, L, -1))
        pa_c, pb_c, pd_c = (ph_c[:, :HG_COLS].reshape(B, n_ctx, -1), ph_c[:, HG_COLS:].reshape(B, n_ctx, -1),
                            pm_c.reshape(B, n_ctx, -1))
        hy = (hy_conv_w[l], hy_conv_b[l], hy_w1[l], hy_b1[l], hy_w2[l], hy_b2[l], hy_w3[l], hy_freq[l], hy_dbias[l])
        a_l, a_c = _hgrn2_group(pa_l, pa_c, lower_bounds[l], hg_norm_w[l], need_ctx)
        b_l = _hyena_seq(pb_l, *hy)
        c_l, c_c = _natten_pallas(pn_l, pn_c, na_rpb[l], need_ctx, B)
        d_l, d_c = _mlstm_group(pd_l, pd_c, ml_conv_w[l], ml_conv_b[l], ml_gate_b[l], need_ctx)
        parts_l = (a_l.reshape(B * L, -1), b_l.reshape(B * L, -1), c_l, d_l.reshape(B * L, -1))
        if need_ctx:
            b_c = _hyena_seq(pb_c, *hy)
            parts_c = (a_c.reshape(B * n_ctx, -1), b_c.reshape(B * n_ctx, -1), c_c, d_c.reshape(B * n_ctx, -1))
            xc2 = _out_proj_ln(parts_c, w_out_l, xc2, mc[2], B * n_ctx, ln_g[l, 0], ln_b[l, 0])
        x2 = _out_proj_ln(parts_l, w_out_l, x2, ml[2], L, ln_g[l, 0], ln_b[l, 0])
        y_lat = _peer(x2, 1 + ml[4], ml[3], L, peer_wq[l], peer_subkeys[l], peer_u[l], peer_v[l])
        x2 = _layer_norm(DEEPNORM_ALPHA * x2.reshape(B, L, D) + ml[5] * y_lat.reshape(B, L, D),
                         ln_g[l, 1], ln_b[l, 1]).reshape(B * L, D)
        if need_ctx:
            y_ctx = _peer(xc2, 1 + mc[4], mc[3], B * n_ctx, peer_wq[l], peer_subkeys[l], peer_u[l], peer_v[l])
            xc2 = _layer_norm(DEEPNORM_ALPHA * xc2 + mc[5][0] * y_ctx, ln_g[l, 1], ln_b[l, 1])
    return x2.reshape(B, L, D)
```

```python
import functools
import math
import jax
import jax.numpy as jnp
from jax import lax
from jax.experimental import pallas as pl
from jax.experimental.pallas import tpu as pltpu

D_MODEL = 2048
BATCH = 4
SEQ = 4096
DEPTH = 2

CTX_LEN = 256
GRID_W = 64
GROUP_WIDTH = D_MODEL // 4
MIX_WIDTH = 4 * GROUP_WIDTH
CHUNK = 64
HG_DIM = 128
HG_HEADS = GROUP_WIDTH // HG_DIM
HG_F_MIN = 1e-30
HY_WIDTH = GROUP_WIDTH
HY_ORDER = 2
HY_SHORT = 3
HY_BANDS = 16
HY_EMB = 1 + 2 * HY_BANDS
HY_HIDDEN = 64
HY_FAST_DECAY = 0.3
HY_SLOW_DECAY = 1.5
HY_DECAY_TARGET = 1e-2
HY_MIN_DECAY = math.log(HY_DECAY_TARGET) / HY_SLOW_DECAY
HY_MAX_DECAY = math.log(HY_DECAY_TARGET) / HY_FAST_DECAY
NA_DIM = 64
NA_HEADS = GROUP_WIDTH // NA_DIM
NA_WIN_ROWS = 8
NA_WIN_COLS = 16
NA_QBLK = 16
NA_KBLK = 32
ML_DIM = 128
ML_HEADS = GROUP_WIDTH // ML_DIM
ML_SHORT = 3
ROPE_THETA = 10000.0
PEER_HEADS = 8
PEER_NKEYS = 128
PEER_EXPERTS = PEER_NKEYS * PEER_NKEYS
PEER_DK = 256
PEER_TOPK = 16
PEER_TOKEN_BLOCK = 64
HG_COLS = 5 * GROUP_WIDTH
HY_COLS = 3 * GROUP_WIDTH
NA_COLS = 3 * GROUP_WIDTH
ML_COLS = 4 * GROUP_WIDTH + 4 * ML_HEADS
IN_WIDTH = HG_COLS + HY_COLS + NA_COLS + ML_COLS
DEEPNORM_ALPHA = (2 * DEPTH) ** 0.25
DEEPNORM_BETA = (8 * DEPTH) ** -0.25
MASK_VALUE = -1e30


def _layer_norm(x, g, b, eps=1e-5):
    xf = x.astype(jnp.float32)
    mu = jnp.mean(xf, axis=-1, keepdims=True)
    var = jnp.mean(jnp.square(xf - mu), axis=-1, keepdims=True)
    y = (xf - mu) * lax.rsqrt(var + eps) * g.astype(jnp.float32) + b.astype(jnp.float32)
    return y.astype(x.dtype)


def _rms_norm(x, g, eps=1e-6):
    xf = x.astype(jnp.float32)
    return xf * lax.rsqrt(jnp.mean(jnp.square(xf), axis=-1, keepdims=True) + eps) * g.astype(jnp.float32)


def _centred_dwconv(x, w, b):
    k, L = w.shape[0], x.shape[1]
    xp = jnp.pad(x, ((0, 0), (k // 2, k // 2), (0, 0)))
    y = sum(xp[:, j:j + L] * w[j].astype(x.dtype) for j in range(k))
    return y + b.astype(x.dtype)


def _axial_rope(x):
    L, dh = x.shape[1], x.shape[-1]
    half, quarter = dh // 2, dh // 4
    t = jnp.arange(L)
    inv_freq = ROPE_THETA ** (-jnp.arange(quarter, dtype=jnp.float32) / quarter)

    def rot(xa, pos):
        ang = pos.astype(jnp.float32)[:, None] * inv_freq
        cos = jnp.cos(ang)[None, :, None, :]
        sin = jnp.sin(ang)[None, :, None, :]
        x1, x2 = xa[..., :quarter], xa[..., quarter:]
        return jnp.concatenate([x1 * cos - x2 * sin, x2 * cos + x1 * sin], axis=-1)

    return jnp.concatenate([rot(x[..., :half], t // GRID_W), rot(x[..., half:], t % GRID_W)], axis=-1)


def _chunks(x):
    b, h, L = x.shape[:3]
    return jnp.moveaxis(x.reshape(b, h, L // CHUNK, CHUNK, *x.shape[3:]), 2, 0)


def _unchunks(x):
    x = jnp.moveaxis(x, 0, 2)
    return x.reshape(x.shape[0], x.shape[1], -1, *x.shape[4:])


def _flip(t):
    return jnp.flip(t, axis=2)


def _hgrn2_scan(q, k, v, logf, s0):
    causal = jnp.tril(jnp.ones((CHUNK, CHUNK), dtype=bool))

    def step(s, inp):
        qc, kc, vc, lfc = inp
        b = jnp.cumsum(lfc, axis=2)
        diff = b[:, :, :, None, :] - b[:, :, None, :, :]
        decay = jnp.exp(jnp.where(causal[:, :, None], diff, MASK_VALUE))
        att = jnp.einsum('bhtd,bhtsd,bhsd->bhts', qc, decay, kc)
        o = jnp.einsum('bhts,bhse->bhte', att, vc) + jnp.einsum('bhtd,bhde->bhte', qc * jnp.exp(b), s)
        b_end = b[:, :, -1:, :]
        s_new = (jnp.exp(b_end[:, :, 0, :, None]) * s
                 + jnp.einsum('bhsd,bhse->bhde', kc * jnp.exp(b_end - b), vc))
        return s_new, o

    s_fin, o = lax.scan(step, s0, (_chunks(q), _chunks(k), _chunks(v), _chunks(logf)))
    return _unchunks(o), s_fin


def _hgrn2_group(p_lat, p_ctx, lb, norm_w, need_ctx):
    def prep(p):
        B, L, _ = p.shape
        q, zf, zb, i, g = jnp.split(p.astype(jnp.float32), 5, axis=-1)

        def heads(t):
            return t.reshape(B, L, HG_HEADS, HG_DIM).transpose(0, 2, 1, 3)

        def forget(z, lbd):
            f = jax.nn.sigmoid(z) + lbd * jax.nn.sigmoid(-z)
            return heads(jnp.log(jnp.maximum(f, HG_F_MIN))), heads((1.0 - lbd) * jax.nn.sigmoid(-z))

        lf_f, k_f = forget(zf, lb[0])
        lf_b, k_b = forget(zb, lb[1])
        return heads(q), heads(i), k_f, lf_f, k_b, lf_b, g

    def readout(o, g):
        B, H, L, dv = o.shape
        o = _rms_norm(o, norm_w).transpose(0, 2, 1, 3).reshape(B, L, H * dv)
        return o * jax.nn.silu(g)

    qc, ic, kcf, lcf, kcb, lcb, gc = prep(p_ctx)
    s0 = jnp.zeros(qc.shape[:2] + (HG_DIM, HG_DIM), jnp.float32)
    oc_f, s_f = _hgrn2_scan(qc, kcf, ic, lcf, s0)
    oc_b, s_b = _hgrn2_scan(_flip(qc), _flip(kcb), _flip(ic), _flip(lcb), s0)
    ql, il, klf, llf, klb, llb, gl = prep(p_lat)
    ol_f, _ = _hgrn2_scan(ql, klf, il, llf, s_f)
    ol_b, _ = _hgrn2_scan(_flip(ql), _flip(klb), _flip(il), _flip(llb), s_b)
    y_lat = readout(ol_f + _flip(ol_b), gl)
    y_ctx = readout(oc_f + _flip(oc_b), gc) if need_ctx else None
    return y_lat, y_ctx


def _hyena_filters(L, w1, b1, w2, b2, w3, freq):
    f32 = jnp.float32
    pos = jnp.arange(L, dtype=f32)
    t = pos / (L - 1)
    bands = jnp.linspace(1e-4, HY_BANDS - 1, HY_BANDS, dtype=f32)
    ang = (2.0 * math.pi / L) * pos[:, None] * bands[None, :]
    z = jnp.concatenate([t[:, None], jnp.cos(ang), jnp.sin(ang)], axis=-1)
    freq = freq.astype(f32)
    hid = jnp.sin(freq[0] * (z @ w1.astype(f32) + b1.astype(f32)))
    hid = jnp.sin(freq[1] * (hid @ w2.astype(f32) + b2.astype(f32)))
    h = (hid @ w3.astype(f32)).reshape(L, 2, HY_ORDER, HY_WIDTH)
    deltas = jnp.abs(jnp.linspace(HY_MIN_DECAY, HY_MAX_DECAY, HY_WIDTH, dtype=f32))
    h = h * jnp.exp(-t[:, None] * deltas)[:, None, None, :]
    fwd, bwd = h[:, 0], h[:, 1]
    k = jnp.concatenate([fwd, jnp.zeros_like(fwd[:1]), bwd[:0:-1]], axis=0)
    return k / jnp.sum(jnp.abs(k), axis=0, keepdims=True)


def _fft_long_conv(z, k, d):
    n, L = k.shape[0], z.shape[1]
    spec = jnp.fft.rfft(z, n=n, axis=1) * jnp.fft.rfft(k, axis=0)[None]
    return jnp.fft.irfft(spec, n=n, axis=1)[:, :L] + z * d.astype(jnp.float32)


def _hyena_seq(p, conv_w, conv_b, w1, b1, w2, b2, w3, freq, dbias):
    u = _centred_dwconv(p, conv_w, conv_b).astype(jnp.float32)
    v, x1, x2 = jnp.split(u, 3, axis=-1)
    k = _hyena_filters(p.shape[1], w1, b1, w2, b2, w3, freq)
    z = x1 * _fft_long_conv(v, k[:, 0], dbias[0])
    return x2 * _fft_long_conv(z, k[:, 1], dbias[1])


def _natten_group(p_lat, p_ctx, rpb, need_ctx):
    f32 = jnp.float32
    B, L, _ = p_lat.shape

    def heads(p):
        return [t.reshape(t.shape[0], t.shape[1], NA_HEADS, NA_DIM) for t in jnp.split(p, 3, axis=-1)]

    q, k, v = heads(p_lat)
    qc, kc, vc = heads(p_ctx)
    scale = NA_DIM ** -0.5
    rows = L // GRID_W
    kh = min(NA_WIN_ROWS, rows)
    ncb = GRID_W // NA_QBLK
    nk = kh * NA_KBLK
    r = jnp.arange(rows)
    key_rows = jnp.clip(r - kh // 2, 0, rows - kh)[:, None] + jnp.arange(kh)
    blk = jnp.arange(ncb)
    qcol = blk[:, None] * NA_QBLK + jnp.arange(NA_QBLK)
    key_cols = (jnp.clip(blk * NA_QBLK - NA_WIN_COLS // 2, 0, GRID_W - NA_KBLK)[:, None]
                + jnp.arange(NA_KBLK))
    idx = (key_rows[:, None, :, None] * GRID_W + key_cols[None, :, None, :]).reshape(rows, ncb, nk)
    col_start = jnp.clip(qcol - NA_WIN_COLS // 2, 0, GRID_W - NA_WIN_COLS)[:, :, None, None]
    kcol = key_cols[:, None, None, :]
    valid = jnp.broadcast_to((kcol >= col_start) & (kcol < col_start + NA_WIN_COLS),
                             (ncb, NA_QBLK, kh, NA_KBLK)).reshape(ncb, NA_QBLK, nk)
    dc = jnp.clip(kcol - qcol[:, :, None, None], 1 - NA_WIN_COLS, NA_WIN_COLS - 1)
    dr = key_rows - r[:, None]
    bias = rpb.astype(f32)[:, dr[:, None, None, :, None] + NA_WIN_ROWS - 1, dc[None] + NA_WIN_COLS - 1]
    bias = jnp.moveaxis(bias.reshape(NA_HEADS, rows, ncb, NA_QBLK, nk), 1, 0)
    q_rows = jnp.moveaxis((q * scale).reshape(B, rows, ncb, NA_QBLK, NA_HEADS, NA_DIM), 1, 0)

    def row_block(inp):
        q_r, idx_r, bias_r = inp
        k_r, v_r = k[:, idx_r], v[:, idx_r]
        s_win = jnp.einsum('bjqhd,bjkhd->bhjqk', q_r, k_r).astype(f32)
        s_win = jnp.where(valid, s_win + bias_r, MASK_VALUE)
        s_ctx = jnp.einsum('bjqhd,bchd->bhjqc', q_r, kc).astype(f32)
        prob = jax.nn.softmax(jnp.concatenate([s_win, s_ctx], axis=-1), axis=-1).astype(v.dtype)
        return (jnp.einsum('bhjqk,bjkhd->bjqhd', prob[..., :nk], v_r)
                + jnp.einsum('bhjqc,bchd->bjqhd', prob[..., nk:], vc))

    out = lax.map(row_block, (q_rows, idx, bias))
    y_lat = jnp.moveaxis(out, 0, 1).reshape(B, L, GROUP_WIDTH)
    y_ctx = None
    if need_ctx:
        s = jnp.einsum('bqhd,bkhd->bhqk', qc * scale, kc).astype(f32)
        prob = jax.nn.softmax(s, axis=-1).astype(vc.dtype)
        y_ctx = jnp.einsum('bhqk,bkhd->bqhd', prob, vc).reshape(B, -1, GROUP_WIDTH)
    return y_lat, y_ctx


def _na_pair_attention(q_pair, key_sets, lane):
    f32 = jnp.float32
    nt = (((1,), (1,)), ((), ()))
    outs = []
    for half in range(2):
        own = (lane >= NA_DIM) if half else (lane < NA_DIM)
        q = jnp.where(own, q_pair, jnp.zeros_like(q_pair))
        scores = []
        for k_pair, _, add, valid in key_sets:
            s = lax.dot_general(q, k_pair, nt, preferred_element_type=f32) * (NA_DIM ** -0.5)
            if add is not None:
                s = jnp.where(valid, s + add[half], MASK_VALUE)
            scores.append(s)
        m = scores[0].max(axis=-1, keepdims=True)
        for s in scores[1:]:
            m = jnp.maximum(m, s.max(axis=-1, keepdims=True))
        den = jnp.zeros_like(m)
        acc = jnp.zeros(q_pair.shape, f32)
        for s, (_, v_pair, _, _) in zip(scores, key_sets):
            p = jnp.exp(s - m)
            den = den + p.sum(axis=-1, keepdims=True)
            acc = acc + jnp.dot(p.astype(jnp.bfloat16), v_pair, preferred_element_type=f32)
        outs.append(acc / den)
    return jnp.where(lane < NA_DIM, outs[0], outs[1])


def _natten_lat_kernel(q_ref, k_ref, v_ref, kc_ref, vc_ref, bias_ref, o_ref, *, rows):
    kh = NA_WIN_ROWS
    r = pl.program_id(1)
    start = pl.multiple_of(jnp.clip(r - kh // 2, 0, rows - kh) * GRID_W, GRID_W)
    nk = kh * GRID_W
    qcol = lax.broadcasted_iota(jnp.int32, (GRID_W, nk), 0)
    kcol = lax.broadcasted_iota(jnp.int32, (GRID_W, nk), 1) % GRID_W
    col_start = jnp.clip(qcol - NA_WIN_COLS // 2, 0, GRID_W - NA_WIN_COLS)
    valid = (kcol >= col_start) & (kcol < col_start + NA_WIN_COLS)
    lane = lax.broadcasted_iota(jnp.int32, (GRID_W, 2 * NA_DIM), 1)
    for hp in range(NA_HEADS // 2):
        cs = slice(hp * 2 * NA_DIM, (hp + 1) * 2 * NA_DIM)
        win = (k_ref[pl.ds(start, nk), cs], v_ref[pl.ds(start, nk), cs],
               (bias_ref[0, 2 * hp], bias_ref[0, 2 * hp + 1]), valid)
        ctx = (kc_ref[:, cs], vc_ref[:, cs], None, None)
        o_ref[:, cs] = _na_pair_attention(q_ref[:, cs], [win, ctx], lane).astype(o_ref.dtype)


def _natten_ctx_kernel(q_ref, k_ref, v_ref, o_ref):
    lane = lax.broadcasted_iota(jnp.int32, (q_ref.shape[0], 2 * NA_DIM), 1)
    for hp in range(NA_HEADS // 2):
        cs = slice(hp * 2 * NA_DIM, (hp + 1) * 2 * NA_DIM)
        o_ref[:, cs] = _na_pair_attention(q_ref[:, cs], [(k_ref[:, cs], v_ref[:, cs], None, None)],
                                          lane).astype(o_ref.dtype)


def _natten_bias_table(rpb, rows):
    kh = NA_WIN_ROWS
    dc = jnp.clip(jnp.arange(GRID_W)[None, :] - jnp.arange(GRID_W)[:, None], 1 - NA_WIN_COLS, NA_WIN_COLS - 1)
    t = rpb.astype(jnp.float32)[:, :, dc + NA_WIN_COLS - 1]
    per_off = [t[:, d:d + kh].transpose(0, 2, 1, 3).reshape(NA_HEADS, GRID_W, kh * GRID_W) for d in range(kh)]
    return jnp.stack(per_off, axis=0)


def _natten_pallas(p_lat, p_ctx, rpb, need_ctx, batch):
    W = GROUP_WIDTH
    L = p_lat.shape[0] // batch
    n_ctx = p_ctx.shape[0] // batch
    rows = L // GRID_W
    kh = NA_WIN_ROWS
    assert rows >= kh and L % GRID_W == 0
    bias = _natten_bias_table(rpb, rows)

    def first_key_row_offset(b, r):
        return (jnp.clip(r - kh // 2, 0, rows - kh) - r + kh - 1, 0, 0, 0)

    y_lat = pl.pallas_call(
        functools.partial(_natten_lat_kernel, rows=rows),
        grid=(batch, rows),
        in_specs=[pl.BlockSpec((GRID_W, W), lambda b, r: (b * rows + r, 0)),
                  pl.BlockSpec((L, W), lambda b, r: (b, 1)),
                  pl.BlockSpec((L, W), lambda b, r: (b, 2)),
                  pl.BlockSpec((n_ctx, W), lambda b, r: (b, 1)),
                  pl.BlockSpec((n_ctx, W), lambda b, r: (b, 2)),
                  pl.BlockSpec((1, NA_HEADS, GRID_W, kh * GRID_W), first_key_row_offset)],
        out_specs=pl.BlockSpec((GRID_W, W), lambda b, r: (b * rows + r, 0)),
        out_shape=jax.ShapeDtypeStruct((batch * L, W), jnp.bfloat16),
        compiler_params=pltpu.CompilerParams(vmem_limit_bytes=40 * 1024 * 1024),
        name="natten_lat",
    )(p_lat, p_lat, p_lat, p_ctx, p_ctx, bias)
    y_ctx = None
    if need_ctx:
        y_ctx = pl.pallas_call(
            _natten_ctx_kernel,
            grid=(batch,),
            in_specs=[pl.BlockSpec((n_ctx, W), lambda b: (b, 0)),
                      pl.BlockSpec((n_ctx, W), lambda b: (b, 1)),
                      pl.BlockSpec((n_ctx, W), lambda b: (b, 2))],
            out_specs=pl.BlockSpec((n_ctx, W), lambda b: (b, 0)),
            out_shape=jax.ShapeDtypeStruct((batch * n_ctx, W), jnp.bfloat16),
            name="natten_ctx",
        )(p_ctx, p_ctx, p_ctx)
    return y_lat, y_ctx


def _mlstm_scan(q, k, v, ig, lf, state):
    causal = jnp.tril(jnp.ones((CHUNK, CHUNK), dtype=bool))

    def step(carry, inp):
        ckv, n, m = carry
        qc, kc, vc, igc, lfc = inp
        g = jnp.cumsum(lfc, axis=-1)
        logd = jnp.where(causal, g[..., :, None] - g[..., None, :] + igc[..., None, :], MASK_VALUE)
        log_inter = g + m[..., None]
        m_out = jnp.maximum(log_inter, jnp.max(logd, axis=-1))
        dmat = jnp.exp(logd - m_out[..., None])
        w_inter = jnp.exp(log_inter - m_out)
        sc = jnp.einsum('bhtd,bhsd->bhts', qc, kc) * dmat
        num = jnp.einsum('bhts,bhse->bhte', sc, vc) + w_inter[..., None] * jnp.einsum('bhtd,bhde->bhte', qc, ckv)
        den = jnp.sum(sc, axis=-1) + w_inter * jnp.einsum('bhtd,bhd->bht', qc, n)
        h = num / jnp.maximum(jnp.abs(den), jnp.exp(-m_out))[..., None]
        g_end = g[..., -1]
        a = g_end[..., None] - g + igc
        m_new = jnp.maximum(g_end + m, jnp.max(a, axis=-1))
        carry_w = jnp.exp(g_end + m - m_new)
        w = jnp.exp(a - m_new[..., None])
        ckv = carry_w[..., None, None] * ckv + jnp.einsum('bhs,bhsd,bhse->bhde', w, kc, vc)
        n = carry_w[..., None] * n + jnp.einsum('bhs,bhsd->bhd', w, kc)
        return (ckv, n, m_new), h

    state, h = lax.scan(step, state, (_chunks(q), _chunks(k), _chunks(v), _chunks(ig), _chunks(lf)))
    return _unchunks(h), state


def _mlstm_group(p_lat, p_ctx, conv_w, conv_b, gate_b, need_ctx):
    W = GROUP_WIDTH
    f32 = jnp.float32

    def prep(p, rope):
        B, L, _ = p.shape
        qk = jax.nn.silu(_centred_dwconv(p[..., :2 * W], conv_w, conv_b)).astype(f32)
        q = qk[..., :W].reshape(B, L, ML_HEADS, ML_DIM)
        k = qk[..., W:].reshape(B, L, ML_HEADS, ML_DIM)
        if rope:
            q, k = _axial_rope(q), _axial_rope(k)
        v = p[..., 2 * W:3 * W].astype(f32).reshape(B, L, ML_HEADS, ML_DIM)
        og = jax.nn.sigmoid(p[..., 3 * W:4 * W].astype(f32))
        gates = (p[..., 4 * W:].astype(f32) + gate_b.astype(f32)).reshape(B, L, 4, ML_HEADS)
        gates = gates.transpose(2, 0, 3, 1)

        def hf(t):
            return t.transpose(0, 2, 1, 3)

        return (hf(q), hf(k * ML_DIM ** -0.5), hf(v), gates[0], jax.nn.log_sigmoid(gates[1]),
                gates[2], jax.nn.log_sigmoid(gates[3]), og)

    def readout(h, og):
        return og * h.transpose(0, 2, 1, 3).reshape(og.shape)

    qc, kc, vc, icf, lcf, icb, lcb, oc = prep(p_ctx, False)
    B = qc.shape[0]
    s0 = (jnp.zeros((B, ML_HEADS, ML_DIM, ML_DIM), f32), jnp.zeros((B, ML_HEADS, ML_DIM), f32),
          jnp.zeros((B, ML_HEADS), f32))
    hc_f, st_f = _mlstm_scan(qc, kc, vc, icf, lcf, s0)
    hc_b, st_b = _mlstm_scan(_flip(qc), _flip(kc), _flip(vc), _flip(icb), _flip(lcb), s0)
    ql, kl, vl, ilf, llf, ilb, llb, ol = prep(p_lat, True)
    hl_f, _ = _mlstm_scan(ql, kl, vl, ilf, llf, st_f)
    hl_b, _ = _mlstm_scan(_flip(ql), _flip(kl), _flip(vl), _flip(ilb), _flip(llb), st_b)
    y_lat = readout(hl_f + _flip(hl_b), ol)
    y_ctx = readout(hc_f + _flip(hc_b), oc) if need_ctx else None
    return y_lat, y_ctx


PEER_SUB_DK = PEER_DK // 2
PEER_CAND_ROWS = 16 + 8 + 6 * 8 + 8


def _topk_rows(s, order, payload, k):
    sentinel = jnp.int32(2 ** 30)
    vals, picked = [], []
    for _ in range(k):
        m = s.max(axis=0, keepdims=True)
        first = jnp.min(jnp.where(s == m, order, sentinel), axis=0, keepdims=True)
        hit = order == first
        vals.append(m)
        picked.append(first if payload is None else jnp.max(jnp.where(hit, payload, -1), axis=0, keepdims=True))
        s = jnp.where(hit, -jnp.inf, s)
    return jnp.concatenate(vals, axis=0), jnp.concatenate(picked, axis=0)


def _peer_candidates(a0, a1, combine):
    pieces = [combine(a0[0:1], a1), combine(a0[1:2], a1[0:8])]
    pieces += [combine(a0[i:i + 1], a1[0:8]) for i in range(2, 8)]
    pieces.append(combine(a0[8:16], a1[0:1]))
    return jnp.concatenate(pieces, axis=0)


def _peer_topk_kernel(x_ref, sc_ref, sh_ref, wq_ref, sk_ref, h_ref, ids_ref, gate_ref):
    f32 = jnp.float32
    tb = x_ref.shape[0]
    h = x_ref[...] * sc_ref[0] + sh_ref[0]
    h_ref[...] = h.astype(h_ref.dtype)
    q = jnp.dot(h.astype(jnp.bfloat16), wq_ref[...], preferred_element_type=f32).astype(jnp.bfloat16)
    key_order = lax.broadcasted_iota(jnp.int32, (PEER_NKEYS, tb), 0)
    rho = lax.broadcasted_iota(jnp.int32, (PEER_CAND_ROWS, tb), 0)
    ci = jnp.where(rho < 16, 0, jnp.where(rho < 24, 1, jnp.where(rho < 72, 2 + ((rho - 24) >> 3), rho - 64)))
    cj = jnp.where(rho < 16, rho, jnp.where(rho < 24, rho - 16, jnp.where(rho < 72, (rho - 24) & 7, 0)))
    cand_order = ci * PEER_TOPK + cj
    cand_valid = (ci + 1) * (cj + 1) <= PEER_TOPK
    nt = (((1,), (1,)), ((), ()))
    for head in range(PEER_HEADS):
        sv, si = [], []
        for half in range(2):
            c0 = (head * 2 + half) * PEER_SUB_DK
            s_t = lax.dot_general(sk_ref[head * 2 + half], q[:, c0:c0 + PEER_SUB_DK], nt,
                                  preferred_element_type=f32)
            v, i = _topk_rows(s_t, key_order, None, PEER_TOPK)
            sv.append(v)
            si.append(i)
        cand_s = _peer_candidates(sv[0], sv[1], lambda a, b: a + b)
        cand_s = jnp.where(cand_valid, cand_s, -jnp.inf)
        cand_id = _peer_candidates(si[0], si[1], lambda a, b: a * PEER_NKEYS + b)
        best_s, ids = _topk_rows(cand_s, cand_order, cand_id, PEER_TOPK)
        e = jnp.exp(best_s - best_s[0:1])
        rs = slice(head * PEER_TOPK, (head + 1) * PEER_TOPK)
        ids_ref[rs, :] = ids
        gate_ref[rs, :] = e / e.sum(axis=0, keepdims=True)


def _peer_route(x2d, scale, shift, rows_per_mod, wq, subkeys, tb=256):
    T, D = x2d.shape
    nk = PEER_HEADS * PEER_TOPK
    sk = subkeys.reshape(PEER_HEADS * 2, PEER_NKEYS, PEER_SUB_DK).astype(jnp.bfloat16)
    mod = lambda i: (i * tb // rows_per_mod, 0, 0)
    h, ids_t, gate_t = pl.pallas_call(
        _peer_topk_kernel,
        grid=(T // tb,),
        in_specs=[pl.BlockSpec((tb, D), lambda i: (i, 0)),
                  pl.BlockSpec((1, 1, D), mod),
                  pl.BlockSpec((1, 1, D), mod),
                  pl.BlockSpec(wq.shape, lambda i: (0, 0)),
                  pl.BlockSpec(sk.shape, lambda i: (0, 0, 0))],
        out_specs=[pl.BlockSpec((tb, D), lambda i: (i, 0)),
                   pl.BlockSpec((nk, tb), lambda i: (0, i)),
                   pl.BlockSpec((nk, tb), lambda i: (0, i))],
        out_shape=[jax.ShapeDtypeStruct((T, D), jnp.bfloat16),
                   jax.ShapeDtypeStruct((nk, T), jnp.int32),
                   jax.ShapeDtypeStruct((nk, T), jnp.float32)],
        compiler_params=pltpu.CompilerParams(vmem_limit_bytes=48 * 1024 * 1024),
        name="peer_route",
    )(x2d, scale, shift, wq.astype(jnp.bfloat16), sk)
    return h, ids_t.T, gate_t.T


PEER_MASK_SLAB = PEER_NKEYS + 8
PEER_MASK_UNROLL = 8
PEER_KEYS_PER_STEP = 8


def _peer_expert_kernel(x_ref, ids_ref, gate_ref, u_ref, v_ref, o_ref, w_scr):
    f32, bf16 = jnp.float32, jnp.bfloat16
    j = pl.program_id(1)
    tm = x_ref.shape[0]
    nt = (((1,), (1,)), ((), ()))

    @pl.when(j == 0)
    def _():
        o_ref[...] = jnp.zeros_like(o_ref)
        key_iota = lax.broadcasted_iota(jnp.int32, (PEER_NKEYS, ids_ref.shape[1]), 0)

        def token_mask(t):
            ids = ids_ref[pl.ds(t, 1), :]
            g = gate_ref[pl.ds(t, 1), :]
            g_hi = g.astype(bf16).astype(f32)
            g_lo = g - g_hi
            first = jnp.where((ids >> 7) == key_iota, 1.0, 0.0).astype(bf16)
            second = (ids & (PEER_NKEYS - 1)) == key_iota
            w_t = (lax.dot_general(first, jnp.where(second, g_hi, 0.0).astype(bf16), nt, preferred_element_type=f32)
                   + lax.dot_general(first, jnp.where(second, g_lo, 0.0).astype(bf16), nt,
                                     preferred_element_type=f32))
            w_scr[pl.ds(pl.multiple_of(t * PEER_MASK_SLAB, 8), PEER_NKEYS), :] = w_t

        def token_group(i, carry):
            for s in range(PEER_MASK_UNROLL):
                token_mask(i * PEER_MASK_UNROLL + s)
            return carry

        lax.fori_loop(0, tm // PEER_MASK_UNROLL, token_group, 0)

    a = lax.dot_general(x_ref[...], u_ref[...], nt, preferred_element_type=f32)
    w = jnp.concatenate([w_scr[pl.ds(j * PEER_KEYS_PER_STEP + s, tm, stride=PEER_MASK_SLAB), :]
                         for s in range(PEER_KEYS_PER_STEP)], axis=1)
    gelu = 0.5 * a * (1.0 + lax.erf(a * (2.0 ** -0.5)))
    c = (w * gelu).astype(bf16)
    o_ref[...] += jnp.dot(c, v_ref[...], preferred_element_type=f32)


def _peer_expert(tok, ids, gate, u, v, tm=256):
    T, D = tok.shape
    te = PEER_KEYS_PER_STEP * PEER_NKEYS
    nk = ids.shape[1]
    assert PEER_NKEYS == 128 and T % tm == 0 and PEER_EXPERTS % te == 0
    return pl.pallas_call(
        _peer_expert_kernel,
        grid=(T // tm, PEER_EXPERTS // te),
        in_specs=[pl.BlockSpec((tm, D), lambda i, j: (i, 0)),
                  pl.BlockSpec((tm, nk), lambda i, j: (i, 0)),
                  pl.BlockSpec((tm, nk), lambda i, j: (i, 0)),
                  pl.BlockSpec((te, D), lambda i, j: (j, 0)),
                  pl.BlockSpec((te, D), lambda i, j: (j, 0))],
        out_specs=pl.BlockSpec((tm, D), lambda i, j: (i, 0)),
        out_shape=jax.ShapeDtypeStruct((T, D), jnp.float32),
        scratch_shapes=[pltpu.VMEM((tm * PEER_MASK_SLAB, PEER_NKEYS), jnp.float32)],
        compiler_params=pltpu.CompilerParams(dimension_semantics=("arbitrary", "arbitrary"),
                                             vmem_limit_bytes=48 * 1024 * 1024),
        name="peer_expert",
    )(tok, ids, gate, u, v)


def _peer(x2d, scale, shift, rows_per_mod, wq, subkeys, u, v):
    tok, ids, gate = _peer_route(x2d, scale, shift, rows_per_mod, wq, subkeys)
    return _peer_expert(tok, ids, gate, u, v)


def _in_proj_kernel(x_ref, sc_ref, sh_ref, w_ref, o_ref, a_scr):
    @pl.when(pl.program_id(1) == 0)
    def _():
        a_scr[...] = (x_ref[...] * sc_ref[0] + sh_ref[0]).astype(a_scr.dtype)

    o_ref[...] = jnp.dot(a_scr[...], w_ref[...], preferred_element_type=jnp.float32).astype(o_ref.dtype)


def _in_proj(x2d, scale, shift, rows_per_mod, w, col0, ncols, out_dtype, tm=1024, tn=512):
    M, K = x2d.shape
    tm = min(tm, M)
    assert col0 % tn == 0 and M % tm == 0
    mod = lambda i, j: (i * tm // rows_per_mod, 0, 0)
    return pl.pallas_call(
        _in_proj_kernel,
        grid=(M // tm, pl.cdiv(ncols, tn)),
        in_specs=[pl.BlockSpec((tm, K), lambda i, j: (i, 0)),
                  pl.BlockSpec((1, 1, K), mod),
                  pl.BlockSpec((1, 1, K), mod),
                  pl.BlockSpec((K, tn), lambda i, j: (0, col0 // tn + j))],
        out_specs=pl.BlockSpec((tm, tn), lambda i, j: (i, j)),
        out_shape=jax.ShapeDtypeStruct((M, ncols), out_dtype),
        scratch_shapes=[pltpu.VMEM((tm, K), jnp.bfloat16)],
        compiler_params=pltpu.CompilerParams(dimension_semantics=("arbitrary", "arbitrary"),
                                             vmem_limit_bytes=48 * 1024 * 1024),
        name="in_proj",
    )(x2d, scale, shift, w)


def _out_proj_ln_kernel(a_ref, b_ref, c_ref, d_ref, w_ref, x_ref, gt_ref, g_ref, bt_ref, o_ref):
    f32 = jnp.float32
    W = GROUP_WIDTH
    y = jnp.zeros(x_ref.shape, f32)
    for g, r in enumerate((a_ref, b_ref, c_ref, d_ref)):
        y = y + jnp.dot(r[...].astype(jnp.bfloat16), w_ref[g * W:(g + 1) * W, :], preferred_element_type=f32)
    z = DEEPNORM_ALPHA * x_ref[...] + gt_ref[0] * y
    mu = jnp.mean(z, axis=-1, keepdims=True)
    var = jnp.mean(jnp.square(z - mu), axis=-1, keepdims=True)
    o_ref[...] = (z - mu) * lax.rsqrt(var + 1e-5) * g_ref[...] + bt_ref[...]


def _out_proj_ln(parts, w, x2d, gate, rows_per_mod, ln_g, ln_b, tm=256):
    M, D = x2d.shape
    W = GROUP_WIDTH
    tm = min(tm, M)
    row = lambda i: (i, 0)
    return pl.pallas_call(
        _out_proj_ln_kernel,
        grid=(M // tm,),
        in_specs=[pl.BlockSpec((tm, W), row)] * 4
                 + [pl.BlockSpec(w.shape, lambda i: (0, 0)),
                    pl.BlockSpec((tm, D), row),
                    pl.BlockSpec((1, 1, D), lambda i: (i * tm // rows_per_mod, 0, 0)),
                    pl.BlockSpec((1, D), lambda i: (0, 0)),
                    pl.BlockSpec((1, D), lambda i: (0, 0))],
        out_specs=pl.BlockSpec((tm, D), row),
        out_shape=jax.ShapeDtypeStruct((M, D), jnp.float32),
        compiler_params=pltpu.CompilerParams(vmem_limit_bytes=48 * 1024 * 1024),
        name="out_proj_ln",
    )(*parts, w, x2d, gate, ln_g.reshape(1, D), ln_b.reshape(1, D))


def kernel(x, c, ctx, c_ctx, w_ada, b_ada, w_in, w_out, ln_g, ln_b, hg_lower_bounds, hg_norm_w,
           hy_conv_w, hy_conv_b, hy_w1, hy_b1, hy_w2, hy_b2, hy_w3, hy_freq, hy_dbias, na_rpb,
           ml_conv_w, ml_conv_b, ml_gate_b, peer_wq, peer_subkeys, peer_u, peer_v):
    f32, bf16 = jnp.float32, jnp.bfloat16
    B, L, D = x.shape
    n_ctx = ctx.shape[1]
    lb_soft = jax.nn.softmax(hg_lower_bounds.astype(f32), axis=0)
    lower_bounds = jnp.cumsum(lb_soft, axis=0) - lb_soft[0]
    cond_lat = jax.nn.silu(c)
    cond_ctx = jax.nn.silu(c_ctx)
    c0_na = HG_COLS + HY_COLS
    c0_ml = c0_na + NA_COLS
    x2 = x.reshape(B * L, D)
    xc2 = ctx.reshape(B * n_ctx, D)
    for l in range(DEPTH):
        need_ctx = l < DEPTH - 1
        m_lat = (cond_lat @ w_ada[l] + b_ada[l]).reshape(B, 1, 6, D)
        m_ctx = (cond_ctx @ w_ada[l] + b_ada[l]).reshape(1, 1, 6, D)
        ml = [m_lat[:, :, k] for k in range(6)]
        mc = [m_ctx[:, :, k] for k in range(6)]
        w_in_l = w_in[l].astype(bf16)
        w_out_l = w_out[l].astype(bf16)

        def in_proj(t, m, rows_per_mod):
            sc, sh = 1 + m[1], m[0]
            return (_in_proj(t, sc, sh, rows_per_mod, w_in_l, 0, c0_na, f32),
                    _in_proj(t, sc, sh, rows_per_mod, w_in_l, c0_na, NA_COLS, bf16),
                    _in_proj(t, sc, sh, rows_per_mod, w_in_l, c0_ml, ML_COLS, f32))

        ph_l, pn_l, pm_l = in_proj(x2, ml, L)
        ph_c, pn_c, pm_c = in_proj(xc2, mc, B * n_ctx)
        pa_l, pb_l, pd_l = (ph_l[:, :HG_COLS].reshape(B, L, -1), ph_l[:, HG_COLS:].reshape(B, L, -1),
                            pm_l.reshape(B, L, -1))
        pa_c, pb_c, pd_c = (ph_c[:, :HG_COLS].reshape(B, n_ctx, -1), ph_c[:, HG_COLS:].reshape(B, n_ctx, -1),
                            pm_c.reshape(B, n_ctx, -1))
        hy = (hy_conv_w[l], hy_conv_b[l], hy_w1[l], hy_b1[l], hy_w2[l], hy_b2[l], hy_w3[l], hy_freq[l], hy_dbias[l])
        a_l, a_c = _hgrn2_group(pa_l, pa_c, lower_bounds[l], hg_norm_w[l], need_ctx)
        b_l = _hyena_seq(pb_l, *hy)
        c_l, c_c = _natten_pallas(pn_l, pn_c, na_rpb[l], need_ctx, B)
        d_l, d_c = _mlstm_group(pd_l, pd_c, ml_conv_w[l], ml_conv_b[l], ml_gate_b[l], need_ctx)
        parts_l = (a_l.reshape(B * L, -1), b_l.reshape(B * L, -1), c_l, d_l.reshape(B * L, -1))
        if need_ctx:
            b_c = _hyena_seq(pb_c, *hy)
            parts_c = (a_c.reshape(B * n_ctx, -1), b_c.reshape(B * n_ctx, -1), c_c, d_c.reshape(B * n_ctx, -1))
            xc2 = _out_proj_ln(parts_c, w_out_l, xc2, mc[2], B * n_ctx, ln_g[l, 0], ln_b[l, 0])
        x2 = _out_proj_ln(parts_l, w_out_l, x2, ml[2], L, ln_g[l, 0], ln_b[l, 0])
        u_l, v_l = peer_u[l].astype(bf16), peer_v[l].astype(bf16)
        y_lat = _peer(x2, 1 + ml[4], ml[3], L, peer_wq[l], peer_subkeys[l], u_l, v_l)
        x2 = _layer_norm(DEEPNORM_ALPHA * x2.reshape(B, L, D) + ml[5] * y_lat.reshape(B, L, D),
                         ln_g[l, 1], ln_b[l, 1]).reshape(B * L, D)
        if need_ctx:
            y_ctx = _peer(xc2, 1 + mc[4], mc[3], B * n_ctx, peer_wq[l], peer_subkeys[l], u_l, v_l)
            xc2 = _layer_norm(DEEPNORM_ALPHA * xc2 + mc[5][0] * y_ctx, ln_g[l, 1], ln_b[l, 1])
    return x2.reshape(B, L, D)
```

```python
import functools
import math
import jax
import jax.numpy as jnp
from jax import lax
from jax.experimental import pallas as pl
from jax.experimental.pallas import tpu as pltpu

D_MODEL = 2048
BATCH = 4
SEQ = 4096
DEPTH = 2

CTX_LEN = 256
GRID_W = 64
GROUP_WIDTH = D_MODEL // 4
MIX_WIDTH = 4 * GROUP_WIDTH
CHUNK = 64
HG_DIM = 128
HG_HEADS = GROUP_WIDTH // HG_DIM
HG_F_MIN = 1e-30
HY_WIDTH = GROUP_WIDTH
HY_ORDER = 2
HY_SHORT = 3
HY_BANDS = 16
HY_EMB = 1 + 2 * HY_BANDS
HY_HIDDEN = 64
HY_FAST_DECAY = 0.3
HY_SLOW_DECAY = 1.5
HY_DECAY_TARGET = 1e-2
HY_MIN_DECAY = math.log(HY_DECAY_TARGET) / HY_SLOW_DECAY
HY_MAX_DECAY = math.log(HY_DECAY_TARGET) / HY_FAST_DECAY
NA_DIM = 64
NA_HEADS = GROUP_WIDTH // NA_DIM
NA_WIN_ROWS = 8
NA_WIN_COLS = 16
NA_QBLK = 16
NA_KBLK = 32
ML_DIM = 128
ML_HEADS = GROUP_WIDTH // ML_DIM
ML_SHORT = 3
ROPE_THETA = 10000.0
PEER_HEADS = 8
PEER_NKEYS = 128
PEER_EXPERTS = PEER_NKEYS * PEER_NKEYS
PEER_DK = 256
PEER_TOPK = 16
PEER_TOKEN_BLOCK = 64
HG_COLS = 5 * GROUP_WIDTH
HY_COLS = 3 * GROUP_WIDTH
NA_COLS = 3 * GROUP_WIDTH
ML_COLS = 4 * GROUP_WIDTH + 4 * ML_HEADS
IN_WIDTH = HG_COLS + HY_COLS + NA_COLS + ML_COLS
DEEPNORM_ALPHA = (2 * DEPTH) ** 0.25
DEEPNORM_BETA = (8 * DEPTH) ** -0.25
MASK_VALUE = -1e30


def _layer_norm(x, g, b, eps=1e-5):
    xf = x.astype(jnp.float32)
    mu = jnp.mean(xf, axis=-1, keepdims=True)
    var = jnp.mean(jnp.square(xf - mu), axis=-1, keepdims=True)
    y = (xf - mu) * lax.rsqrt(var + eps) * g.astype(jnp.float32) + b.astype(jnp.float32)
    return y.astype(x.dtype)


def _rms_norm(x, g, eps=1e-6):
    xf = x.astype(jnp.float32)
    return xf * lax.rsqrt(jnp.mean(jnp.square(xf), axis=-1, keepdims=True) + eps) * g.astype(jnp.float32)


def _centred_dwconv(x, w, b):
    k, L = w.shape[0], x.shape[1]
    xp = jnp.pad(x, ((0, 0), (k // 2, k // 2), (0, 0)))
    y = sum(xp[:, j:j + L] * w[j].astype(x.dtype) for j in range(k))
    return y + b.astype(x.dtype)


def _axial_rope(x):
    L, dh = x.shape[1], x.shape[-1]
    half, quarter = dh // 2, dh // 4
    t = jnp.arange(L)
    inv_freq = ROPE_THETA ** (-jnp.arange(quarter, dtype=jnp.float32) / quarter)

    def rot(xa, pos):
        ang = pos.astype(jnp.float32)[:, None] * inv_freq
        cos = jnp.cos(ang)[None, :, None, :]
        sin = jnp.sin(ang)[None, :, None, :]
        x1, x2 = xa[..., :quarter], xa[..., quarter:]
        return jnp.concatenate([x1 * cos - x2 * sin, x2 * cos + x1 * sin], axis=-1)

    return jnp.concatenate([rot(x[..., :half], t // GRID_W), rot(x[..., half:], t % GRID_W)], axis=-1)


def _chunks(x):
    b, h, L = x.shape[:3]
    return jnp.moveaxis(x.reshape(b, h, L // CHUNK, CHUNK, *x.shape[3:]), 2, 0)


def _unchunks(x):
    x = jnp.moveaxis(x, 0, 2)
    return x.reshape(x.shape[0], x.shape[1], -1, *x.shape[4:])


def _flip(t):
    return jnp.flip(t, axis=2)


def _hgrn2_scan(q, k, v, logf, s0):
    causal = jnp.tril(jnp.ones((CHUNK, CHUNK), dtype=bool))

    def step(s, inp):
        qc, kc, vc, lfc = inp
        b = jnp.cumsum(lfc, axis=2)
        diff = b[:, :, :, None, :] - b[:, :, None, :, :]
        decay = jnp.exp(jnp.where(causal[:, :, None], diff, MASK_VALUE))
        att = jnp.einsum('bhtd,bhtsd,bhsd->bhts', qc, decay, kc)
        o = jnp.einsum('bhts,bhse->bhte', att, vc) + jnp.einsum('bhtd,bhde->bhte', qc * jnp.exp(b), s)
        b_end = b[:, :, -1:, :]
        s_new = (jnp.exp(b_end[:, :, 0, :, None]) * s
                 + jnp.einsum('bhsd,bhse->bhde', kc * jnp.exp(b_end - b), vc))
        return s_new, o

    s_fin, o = lax.scan(step, s0, (_chunks(q), _chunks(k), _chunks(v), _chunks(logf)))
    return _unchunks(o), s_fin


def _hgrn2_group(p_lat, p_ctx, lb, norm_w, need_ctx):
    def prep(p):
        B, L, _ = p.shape
        q, zf, zb, i, g = jnp.split(p.astype(jnp.float32), 5, axis=-1)

        def heads(t):
            return t.reshape(B, L, HG_HEADS, HG_DIM).transpose(0, 2, 1, 3)

        def forget(z, lbd):
            f = jax.nn.sigmoid(z) + lbd * jax.nn.sigmoid(-z)
            return heads(jnp.log(jnp.maximum(f, HG_F_MIN))), heads((1.0 - lbd) * jax.nn.sigmoid(-z))

        lf_f, k_f = forget(zf, lb[0])
        lf_b, k_b = forget(zb, lb[1])
        return heads(q), heads(i), k_f, lf_f, k_b, lf_b, g

    def readout(o, g):
        B, H, L, dv = o.shape
        o = _rms_norm(o, norm_w).transpose(0, 2, 1, 3).reshape(B, L, H * dv)
        return o * jax.nn.silu(g)

    qc, ic, kcf, lcf, kcb, lcb, gc = prep(p_ctx)
    s0 = jnp.zeros(qc.shape[:2] + (HG_DIM, HG_DIM), jnp.float32)
    oc_f, s_f = _hgrn2_scan(qc, kcf, ic, lcf, s0)
    oc_b, s_b = _hgrn2_scan(_flip(qc), _flip(kcb), _flip(ic), _flip(lcb), s0)
    ql, il, klf, llf, klb, llb, gl = prep(p_lat)
    ol_f, _ = _hgrn2_scan(ql, klf, il, llf, s_f)
    ol_b, _ = _hgrn2_scan(_flip(ql), _flip(klb), _flip(il), _flip(llb), s_b)
    y_lat = readout(ol_f + _flip(ol_b), gl)
    y_ctx = readout(oc_f + _flip(oc_b), gc) if need_ctx else None
    return y_lat, y_ctx


def _hyena_filters(L, w1, b1, w2, b2, w3, freq):
    f32 = jnp.float32
    pos = jnp.arange(L, dtype=f32)
    t = pos / (L - 1)
    bands = jnp.linspace(1e-4, HY_BANDS - 1, HY_BANDS, dtype=f32)
    ang = (2.0 * math.pi / L) * pos[:, None] * bands[None, :]
    z = jnp.concatenate([t[:, None], jnp.cos(ang), jnp.sin(ang)], axis=-1)
    freq = freq.astype(f32)
    hid = jnp.sin(freq[0] * (z @ w1.astype(f32) + b1.astype(f32)))
    hid = jnp.sin(freq[1] * (hid @ w2.astype(f32) + b2.astype(f32)))
    h = (hid @ w3.astype(f32)).reshape(L, 2, HY_ORDER, HY_WIDTH)
    deltas = jnp.abs(jnp.linspace(HY_MIN_DECAY, HY_MAX_DECAY, HY_WIDTH, dtype=f32))
    h = h * jnp.exp(-t[:, None] * deltas)[:, None, None, :]
    fwd, bwd = h[:, 0], h[:, 1]
    bwd = bwd.at[0].set(0.0)
    norm = jnp.sum(jnp.abs(fwd), axis=0, keepdims=True) + jnp.sum(jnp.abs(bwd), axis=0, keepdims=True)
    return fwd / norm, bwd / norm


HY_FREQ_TILE = 384
HY_TIME_TILE = 256
HY_ANGLE_SPLIT = 64


def _dft_angle_tables(rows, cols, period):
    f32 = jnp.float32
    r = jnp.arange(rows, dtype=jnp.int32)[:, None]
    ch = jnp.arange(cols // HY_ANGLE_SPLIT, dtype=jnp.int32)[None, :] * HY_ANGLE_SPLIT
    cl = jnp.arange(HY_ANGLE_SPLIT, dtype=jnp.int32)[None, :]
    ah = ((r * ch) % period).astype(f32) * (2.0 * math.pi / period)
    al = ((r * cl) % period).astype(f32) * (2.0 * math.pi / period)
    ca, sa, cb, sb = jnp.cos(ah)[:, :, None], jnp.sin(ah)[:, :, None], jnp.cos(al)[:, None, :], jnp.sin(al)[:, None, :]
    return (ca * cb - sa * sb).reshape(rows, cols), (sa * cb + ca * sb).reshape(rows, cols)


def _hyena_dft_matrices(L):
    N = 2 * L
    mp = -(-(L + 1) // HY_FREQ_TILE) * HY_FREQ_TILE
    bf16 = jnp.bfloat16
    k = jnp.arange(mp)
    live = (k <= L).astype(jnp.float32)
    cf, sf = _dft_angle_tables(mp, L, N)
    ci, si = _dft_angle_tables(L, mp, N)
    w = jnp.where((k == 0) | (k == L), 1.0, 2.0) * live / N
    return ((cf * live[:, None]).astype(bf16), (sf * live[:, None]).astype(bf16),
            (ci * w[None, :]).astype(bf16), (-si * w[None, :]).astype(bf16))


def _dft_pair_kernel(c_ref, s_ref, a_ref, b_ref, oc_ref, os_ref):
    oc_ref[...] = jnp.dot(c_ref[...], a_ref[...], preferred_element_type=jnp.float32)
    os_ref[...] = jnp.dot(s_ref[...], b_ref[...], preferred_element_type=jnp.float32)


def _hyena_filter_spectrum(cf, sf, fwd, bwd):
    mp, L = cf.shape
    C = fwd.shape[1]
    bf16 = jnp.bfloat16
    col = lambda i: (0, 0)
    re, s = pl.pallas_call(
        _dft_pair_kernel,
        grid=(mp // HY_FREQ_TILE,),
        in_specs=[pl.BlockSpec((HY_FREQ_TILE, L), lambda i: (i, 0))] * 2 + [pl.BlockSpec((L, C), col)] * 2,
        out_specs=[pl.BlockSpec((HY_FREQ_TILE, C), lambda i: (i, 0))] * 2,
        out_shape=[jax.ShapeDtypeStruct((mp, C), jnp.float32)] * 2,
        compiler_params=pltpu.CompilerParams(vmem_limit_bytes=48 * 1024 * 1024),
        name="hyena_filter_spectrum",
    )(cf, sf, (fwd + bwd).astype(bf16), (fwd - bwd).astype(bf16))
    return re, -s


def _hyena_fwd_kernel(c_ref, s_ref, z_ref, kre_ref, kim_ref, pre_ref, pim_ref):
    f32 = jnp.float32
    xc = jnp.dot(c_ref[...], z_ref[0], preferred_element_type=f32)
    xs = jnp.dot(s_ref[...], z_ref[0], preferred_element_type=f32)
    pre_ref[0] = (xc * kre_ref[...] + xs * kim_ref[...]).astype(pre_ref.dtype)
    pim_ref[0] = (xc * kim_ref[...] - xs * kre_ref[...]).astype(pim_ref.dtype)


def _hyena_inv_kernel(ci_ref, si_ref, pre_ref, pim_ref, z_ref, d_ref, m_ref, o_ref, ob_ref):
    f32 = jnp.float32
    y = (jnp.dot(ci_ref[...], pre_ref[0], preferred_element_type=f32)
         + jnp.dot(si_ref[...], pim_ref[0], preferred_element_type=f32))
    out = m_ref[0] * (y + z_ref[0] * d_ref[...])
    o_ref[0] = out
    ob_ref[0] = out.astype(ob_ref.dtype)


def _hyena_long_conv(mats, z, z_b, kre, kim, d, mult):
    cf, sf, ci, si = mats
    B, L, C = z.shape
    mp = cf.shape[0]
    tt = min(HY_TIME_TILE, L)
    bf16 = jnp.bfloat16
    pre, pim = pl.pallas_call(
        _hyena_fwd_kernel,
        grid=(mp // HY_FREQ_TILE, B),
        in_specs=[pl.BlockSpec((HY_FREQ_TILE, L), lambda i, b: (i, 0))] * 2
                 + [pl.BlockSpec((1, L, C), lambda i, b: (b, 0, 0))]
                 + [pl.BlockSpec((HY_FREQ_TILE, C), lambda i, b: (i, 0))] * 2,
        out_specs=[pl.BlockSpec((1, HY_FREQ_TILE, C), lambda i, b: (b, i, 0))] * 2,
        out_shape=[jax.ShapeDtypeStruct((B, mp, C), bf16)] * 2,
        compiler_params=pltpu.CompilerParams(vmem_limit_bytes=48 * 1024 * 1024),
        name="hyena_fwd",
    )(cf, sf, z_b, kre, kim)
    row = lambda b, t: (b, t, 0)
    return pl.pallas_call(
        _hyena_inv_kernel,
        grid=(B, L // tt),
        in_specs=[pl.BlockSpec((tt, mp), lambda b, t: (t, 0))] * 2
                 + [pl.BlockSpec((1, mp, C), lambda b, t: (b, 0, 0))] * 2
                 + [pl.BlockSpec((1, tt, C), row), pl.BlockSpec((1, C), lambda b, t: (0, 0)),
                    pl.BlockSpec((1, tt, C), row)],
        out_specs=[pl.BlockSpec((1, tt, C), row)] * 2,
        out_shape=[jax.ShapeDtypeStruct((B, L, C), jnp.float32), jax.ShapeDtypeStruct((B, L, C), bf16)],
        compiler_params=pltpu.CompilerParams(vmem_limit_bytes=48 * 1024 * 1024),
        name="hyena_inv",
    )(ci, si, pre, pim, z, d.reshape(1, C).astype(jnp.float32), mult)


def _hyena_seq(p, mats, conv_w, conv_b, w1, b1, w2, b2, w3, freq, dbias):
    u = _centred_dwconv(p, conv_w, conv_b).astype(jnp.float32)
    v, x1, x2 = jnp.split(u, 3, axis=-1)
    L = p.shape[1]
    fwd, bwd = _hyena_filters(L, w1, b1, w2, b2, w3, freq)
    kre, kim = _hyena_filter_spectrum(mats[0], mats[1], fwd.reshape(L, -1), bwd.reshape(L, -1))
    W = HY_WIDTH
    z, z_b = _hyena_long_conv(mats, v, v.astype(jnp.bfloat16), kre[:, :W], kim[:, :W], dbias[0], x1)
    return _hyena_long_conv(mats, z, z_b, kre[:, W:], kim[:, W:], dbias[1], x2)[1]


def _natten_group(p_lat, p_ctx, rpb, need_ctx):
    f32 = jnp.float32
    B, L, _ = p_lat.shape

    def heads(p):
        return [t.reshape(t.shape[0], t.shape[1], NA_HEADS, NA_DIM) for t in jnp.split(p, 3, axis=-1)]

    q, k, v = heads(p_lat)
    qc, kc, vc = heads(p_ctx)
    scale = NA_DIM ** -0.5
    rows = L // GRID_W
    kh = min(NA_WIN_ROWS, rows)
    ncb = GRID_W // NA_QBLK
    nk = kh * NA_KBLK
    r = jnp.arange(rows)
    key_rows = jnp.clip(r - kh // 2, 0, rows - kh)[:, None] + jnp.arange(kh)
    blk = jnp.arange(ncb)
    qcol = blk[:, None] * NA_QBLK + jnp.arange(NA_QBLK)
    key_cols = (jnp.clip(blk * NA_QBLK - NA_WIN_COLS // 2, 0, GRID_W - NA_KBLK)[:, None]
                + jnp.arange(NA_KBLK))
    idx = (key_rows[:, None, :, None] * GRID_W + key_cols[None, :, None, :]).reshape(rows, ncb, nk)
    col_start = jnp.clip(qcol - NA_WIN_COLS // 2, 0, GRID_W - NA_WIN_COLS)[:, :, None, None]
    kcol = key_cols[:, None, None, :]
    valid = jnp.broadcast_to((kcol >= col_start) & (kcol < col_start + NA_WIN_COLS),
                             (ncb, NA_QBLK, kh, NA_KBLK)).reshape(ncb, NA_QBLK, nk)
    dc = jnp.clip(kcol - qcol[:, :, None, None], 1 - NA_WIN_COLS, NA_WIN_COLS - 1)
    dr = key_rows - r[:, None]
    bias = rpb.astype(f32)[:, dr[:, None, None, :, None] + NA_WIN_ROWS - 1, dc[None] + NA_WIN_COLS - 1]
    bias = jnp.moveaxis(bias.reshape(NA_HEADS, rows, ncb, NA_QBLK, nk), 1, 0)
    q_rows = jnp.moveaxis((q * scale).reshape(B, rows, ncb, NA_QBLK, NA_HEADS, NA_DIM), 1, 0)

    def row_block(inp):
        q_r, idx_r, bias_r = inp
        k_r, v_r = k[:, idx_r], v[:, idx_r]
        s_win = jnp.einsum('bjqhd,bjkhd->bhjqk', q_r, k_r).astype(f32)
        s_win = jnp.where(valid, s_win + bias_r, MASK_VALUE)
        s_ctx = jnp.einsum('bjqhd,bchd->bhjqc', q_r, kc).astype(f32)
        prob = jax.nn.softmax(jnp.concatenate([s_win, s_ctx], axis=-1), axis=-1).astype(v.dtype)
        return (jnp.einsum('bhjqk,bjkhd->bjqhd', prob[..., :nk], v_r)
                + jnp.einsum('bhjqc,bchd->bjqhd', prob[..., nk:], vc))

    out = lax.map(row_block, (q_rows, idx, bias))
    y_lat = jnp.moveaxis(out, 0, 1).reshape(B, L, GROUP_WIDTH)
    y_ctx = None
    if need_ctx:
        s = jnp.einsum('bqhd,bkhd->bhqk', qc * scale, kc).astype(f32)
        prob = jax.nn.softmax(s, axis=-1).astype(vc.dtype)
        y_ctx = jnp.einsum('bhqk,bkhd->bqhd', prob, vc).reshape(B, -1, GROUP_WIDTH)
    return y_lat, y_ctx


def _na_pair_attention(q_pair, key_sets, lane):
    f32 = jnp.float32
    nt = (((1,), (1,)), ((), ()))
    outs = []
    for half in range(2):
        own = (lane >= NA_DIM) if half else (lane < NA_DIM)
        q = jnp.where(own, q_pair, jnp.zeros_like(q_pair))
        scores = []
        for k_pair, _, add, valid in key_sets:
            s = lax.dot_general(q, k_pair, nt, preferred_element_type=f32) * (NA_DIM ** -0.5)
            if add is not None:
                s = jnp.where(valid, s + add[half], MASK_VALUE)
            scores.append(s)
        m = scores[0].max(axis=-1, keepdims=True)
        for s in scores[1:]:
            m = jnp.maximum(m, s.max(axis=-1, keepdims=True))
        den = jnp.zeros_like(m)
        acc = jnp.zeros(q_pair.shape, f32)
        for s, (_, v_pair, _, _) in zip(scores, key_sets):
            p = jnp.exp(s - m)
            den = den + p.sum(axis=-1, keepdims=True)
            acc = acc + jnp.dot(p.astype(jnp.bfloat16), v_pair, preferred_element_type=f32)
        outs.append(acc / den)
    return jnp.where(lane < NA_DIM, outs[0], outs[1])


def _natten_lat_kernel(q_ref, k_ref, v_ref, kc_ref, vc_ref, bias_ref, o_ref, *, rows):
    kh = NA_WIN_ROWS
    r = pl.program_id(1)
    start = pl.multiple_of(jnp.clip(r - kh // 2, 0, rows - kh) * GRID_W, GRID_W)
    nk = kh * GRID_W
    qcol = lax.broadcasted_iota(jnp.int32, (GRID_W, nk), 0)
    kcol = lax.broadcasted_iota(jnp.int32, (GRID_W, nk), 1) % GRID_W
    col_start = jnp.clip(qcol - NA_WIN_COLS // 2, 0, GRID_W - NA_WIN_COLS)
    valid = (kcol >= col_start) & (kcol < col_start + NA_WIN_COLS)
    lane = lax.broadcasted_iota(jnp.int32, (GRID_W, 2 * NA_DIM), 1)
    for hp in range(NA_HEADS // 2):
        cs = slice(hp * 2 * NA_DIM, (hp + 1) * 2 * NA_DIM)
        win = (k_ref[pl.ds(start, nk), cs], v_ref[pl.ds(start, nk), cs],
               (bias_ref[0, 2 * hp], bias_ref[0, 2 * hp + 1]), valid)
        ctx = (kc_ref[:, cs], vc_ref[:, cs], None, None)
        o_ref[:, cs] = _na_pair_attention(q_ref[:, cs], [win, ctx], lane).astype(o_ref.dtype)


def _natten_ctx_kernel(q_ref, k_ref, v_ref, o_ref):
    lane = lax.broadcasted_iota(jnp.int32, (q_ref.shape[0], 2 * NA_DIM), 1)
    for hp in range(NA_HEADS // 2):
        cs = slice(hp * 2 * NA_DIM, (hp + 1) * 2 * NA_DIM)
        o_ref[:, cs] = _na_pair_attention(q_ref[:, cs], [(k_ref[:, cs], v_ref[:, cs], None, None)],
                                          lane).astype(o_ref.dtype)


def _natten_bias_table(rpb, rows):
    kh = NA_WIN_ROWS
    dc = jnp.clip(jnp.arange(GRID_W)[None, :] - jnp.arange(GRID_W)[:, None], 1 - NA_WIN_COLS, NA_WIN_COLS - 1)
    t = rpb.astype(jnp.float32)[:, :, dc + NA_WIN_COLS - 1]
    per_off = [t[:, d:d + kh].transpose(0, 2, 1, 3).reshape(NA_HEADS, GRID_W, kh * GRID_W) for d in range(kh)]
    return jnp.stack(per_off, axis=0)


def _natten_pallas(p_lat, p_ctx, rpb, need_ctx, batch):
    W = GROUP_WIDTH
    L = p_lat.shape[0] // batch
    n_ctx = p_ctx.shape[0] // batch
    rows = L // GRID_W
    kh = NA_WIN_ROWS
    assert rows >= kh and L % GRID_W == 0
    bias = _natten_bias_table(rpb, rows)

    def first_key_row_offset(b, r):
        return (jnp.clip(r - kh // 2, 0, rows - kh) - r + kh - 1, 0, 0, 0)

    y_lat = pl.pallas_call(
        functools.partial(_natten_lat_kernel, rows=rows),
        grid=(batch, rows),
        in_specs=[pl.BlockSpec((GRID_W, W), lambda b, r: (b * rows + r, 0)),
                  pl.BlockSpec((L, W), lambda b, r: (b, 1)),
                  pl.BlockSpec((L, W), lambda b, r: (b, 2)),
                  pl.BlockSpec((n_ctx, W), lambda b, r: (b, 1)),
                  pl.BlockSpec((n_ctx, W), lambda b, r: (b, 2)),
                  pl.BlockSpec((1, NA_HEADS, GRID_W, kh * GRID_W), first_key_row_offset)],
        out_specs=pl.BlockSpec((GRID_W, W), lambda b, r: (b * rows + r, 0)),
        out_shape=jax.ShapeDtypeStruct((batch * L, W), jnp.bfloat16),
        compiler_params=pltpu.CompilerParams(vmem_limit_bytes=40 * 1024 * 1024),
        name="natten_lat",
    )(p_lat, p_lat, p_lat, p_ctx, p_ctx, bias)
    y_ctx = None
    if need_ctx:
        y_ctx = pl.pallas_call(
            _natten_ctx_kernel,
            grid=(batch,),
            in_specs=[pl.BlockSpec((n_ctx, W), lambda b: (b, 0)),
                      pl.BlockSpec((n_ctx, W), lambda b: (b, 1)),
                      pl.BlockSpec((n_ctx, W), lambda b: (b, 2))],
            out_specs=pl.BlockSpec((n_ctx, W), lambda b: (b, 0)),
            out_shape=jax.ShapeDtypeStruct((batch * n_ctx, W), jnp.bfloat16),
            name="natten_ctx",
        )(p_ctx, p_ctx, p_ctx)
    return y_lat, y_ctx


def _mlstm_scan(q, k, v, ig, lf, state):
    causal = jnp.tril(jnp.ones((CHUNK, CHUNK), dtype=bool))

    def step(carry, inp):
        ckv, n, m = carry
        qc, kc, vc, igc, lfc = inp
        g = jnp.cumsum(lfc, axis=-1)
        logd = jnp.where(causal, g[..., :, None] - g[..., None, :] + igc[..., None, :], MASK_VALUE)
        log_inter = g + m[..., None]
        m_out = jnp.maximum(log_inter, jnp.max(logd, axis=-1))
        dmat = jnp.exp(logd - m_out[..., None])
        w_inter = jnp.exp(log_inter - m_out)
        sc = jnp.einsum('bhtd,bhsd->bhts', qc, kc) * dmat
        num = jnp.einsum('bhts,bhse->bhte', sc, vc) + w_inter[..., None] * jnp.einsum('bhtd,bhde->bhte', qc, ckv)
        den = jnp.sum(sc, axis=-1) + w_inter * jnp.einsum('bhtd,bhd->bht', qc, n)
        h = num / jnp.maximum(jnp.abs(den), jnp.exp(-m_out))[..., None]
        g_end = g[..., -1]
        a = g_end[..., None] - g + igc
        m_new = jnp.maximum(g_end + m, jnp.max(a, axis=-1))
        carry_w = jnp.exp(g_end + m - m_new)
        w = jnp.exp(a - m_new[..., None])
        ckv = carry_w[..., None, None] * ckv + jnp.einsum('bhs,bhsd,bhse->bhde', w, kc, vc)
        n = carry_w[..., None] * n + jnp.einsum('bhs,bhsd->bhd', w, kc)
        return (ckv, n, m_new), h

    state, h = lax.scan(step, state, (_chunks(q), _chunks(k), _chunks(v), _chunks(ig), _chunks(lf)))
    return _unchunks(h), state


def _mlstm_group(p_lat, p_ctx, conv_w, conv_b, gate_b, need_ctx):
    W = GROUP_WIDTH
    f32 = jnp.float32

    def prep(p, rope):
        B, L, _ = p.shape
        qk = jax.nn.silu(_centred_dwconv(p[..., :2 * W], conv_w, conv_b)).astype(f32)
        q = qk[..., :W].reshape(B, L, ML_HEADS, ML_DIM)
        k = qk[..., W:].reshape(B, L, ML_HEADS, ML_DIM)
        if rope:
            q, k = _axial_rope(q), _axial_rope(k)
        v = p[..., 2 * W:3 * W].astype(f32).reshape(B, L, ML_HEADS, ML_DIM)
        og = jax.nn.sigmoid(p[..., 3 * W:4 * W].astype(f32))
        gates = (p[..., 4 * W:].astype(f32) + gate_b.astype(f32)).reshape(B, L, 4, ML_HEADS)
        gates = gates.transpose(2, 0, 3, 1)

        def hf(t):
            return t.transpose(0, 2, 1, 3)

        return (hf(q), hf(k * ML_DIM ** -0.5), hf(v), gates[0], jax.nn.log_sigmoid(gates[1]),
                gates[2], jax.nn.log_sigmoid(gates[3]), og)

    def readout(h, og):
        return og * h.transpose(0, 2, 1, 3).reshape(og.shape)

    qc, kc, vc, icf, lcf, icb, lcb, oc = prep(p_ctx, False)
    B = qc.shape[0]
    s0 = (jnp.zeros((B, ML_HEADS, ML_DIM, ML_DIM), f32), jnp.zeros((B, ML_HEADS, ML_DIM), f32),
          jnp.zeros((B, ML_HEADS), f32))
    hc_f, st_f = _mlstm_scan(qc, kc, vc, icf, lcf, s0)
    hc_b, st_b = _mlstm_scan(_flip(qc), _flip(kc), _flip(vc), _flip(icb), _flip(lcb), s0)
    ql, kl, vl, ilf, llf, ilb, llb, ol = prep(p_lat, True)
    hl_f, _ = _mlstm_scan(ql, kl, vl, ilf, llf, st_f)
    hl_b, _ = _mlstm_scan(_flip(ql), _flip(kl), _flip(vl), _flip(ilb), _flip(llb), st_b)
    y_lat = readout(hl_f + _flip(hl_b), ol)
    y_ctx = readout(hc_f + _flip(hc_b), oc) if need_ctx else None
    return y_lat, y_ctx


PEER_SUB_DK = PEER_DK // 2
PEER_CAND_ROWS = 16 + 8 + 6 * 8 + 8


def _topk_rows(s, order, payload, k):
    sentinel = jnp.int32(2 ** 30)
    vals, picked = [], []
    for _ in range(k):
        m = s.max(axis=0, keepdims=True)
        first = jnp.min(jnp.where(s == m, order, sentinel), axis=0, keepdims=True)
        hit = order == first
        vals.append(m)
        picked.append(first if payload is None else jnp.max(jnp.where(hit, payload, -1), axis=0, keepdims=True))
        s = jnp.where(hit, -jnp.inf, s)
    return jnp.concatenate(vals, axis=0), jnp.concatenate(picked, axis=0)


def _peer_candidates(a0, a1, combine):
    pieces = [combine(a0[0:1], a1), combine(a0[1:2], a1[0:8])]
    pieces += [combine(a0[i:i + 1], a1[0:8]) for i in range(2, 8)]
    pieces.append(combine(a0[8:16], a1[0:1]))
    return jnp.concatenate(pieces, axis=0)


def _peer_topk_kernel(x_ref, sc_ref, sh_ref, wq_ref, sk_ref, h_ref, ids_ref, gate_ref):
    f32 = jnp.float32
    tb = x_ref.shape[0]
    h = x_ref[...] * sc_ref[0] + sh_ref[0]
    h_ref[...] = h.astype(h_ref.dtype)
    q = jnp.dot(h.astype(jnp.bfloat16), wq_ref[...], preferred_element_type=f32).astype(jnp.bfloat16)
    key_order = lax.broadcasted_iota(jnp.int32, (PEER_NKEYS, tb), 0)
    rho = lax.broadcasted_iota(jnp.int32, (PEER_CAND_ROWS, tb), 0)
    ci = jnp.where(rho < 16, 0, jnp.where(rho < 24, 1, jnp.where(rho < 72, 2 + ((rho - 24) >> 3), rho - 64)))
    cj = jnp.where(rho < 16, rho, jnp.where(rho < 24, rho - 16, jnp.where(rho < 72, (rho - 24) & 7, 0)))
    cand_order = ci * PEER_TOPK + cj
    cand_valid = (ci + 1) * (cj + 1) <= PEER_TOPK
    nt = (((1,), (1,)), ((), ()))
    for head in range(PEER_HEADS):
        sv, si = [], []
        for half in range(2):
            c0 = (head * 2 + half) * PEER_SUB_DK
            s_t = lax.dot_general(sk_ref[head * 2 + half], q[:, c0:c0 + PEER_SUB_DK], nt,
                                  preferred_element_type=f32)
            v, i = _topk_rows(s_t, key_order, None, PEER_TOPK)
            sv.append(v)
            si.append(i)
        cand_s = _peer_candidates(sv[0], sv[1], lambda a, b: a + b)
        cand_s = jnp.where(cand_valid, cand_s, -jnp.inf)
        cand_id = _peer_candidates(si[0], si[1], lambda a, b: a * PEER_NKEYS + b)
        best_s, ids = _topk_rows(cand_s, cand_order, cand_id, PEER_TOPK)
        e = jnp.exp(best_s - best_s[0:1])
        rs = slice(head * PEER_TOPK, (head + 1) * PEER_TOPK)
        ids_ref[rs, :] = ids
        gate_ref[rs, :] = e / e.sum(axis=0, keepdims=True)


def _peer_route(x2d, scale, shift, rows_per_mod, wq, subkeys, tb=256):
    T, D = x2d.shape
    nk = PEER_HEADS * PEER_TOPK
    sk = subkeys.reshape(PEER_HEADS * 2, PEER_NKEYS, PEER_SUB_DK).astype(jnp.bfloat16)
    mod = lambda i: (i * tb // rows_per_mod, 0, 0)
    h, ids_t, gate_t = pl.pallas_call(
        _peer_topk_kernel,
        grid=(T // tb,),
        in_specs=[pl.BlockSpec((tb, D), lambda i: (i, 0)),
                  pl.BlockSpec((1, 1, D), mod),
                  pl.BlockSpec((1, 1, D), mod),
                  pl.BlockSpec(wq.shape, lambda i: (0, 0)),
                  pl.BlockSpec(sk.shape, lambda i: (0, 0, 0))],
        out_specs=[pl.BlockSpec((tb, D), lambda i: (i, 0)),
                   pl.BlockSpec((nk, tb), lambda i: (0, i)),
                   pl.BlockSpec((nk, tb), lambda i: (0, i))],
        out_shape=[jax.ShapeDtypeStruct((T, D), jnp.bfloat16),
                   jax.ShapeDtypeStruct((nk, T), jnp.int32),
                   jax.ShapeDtypeStruct((nk, T), jnp.float32)],
        compiler_params=pltpu.CompilerParams(vmem_limit_bytes=48 * 1024 * 1024),
        name="peer_route",
    )(x2d, scale, shift, wq.astype(jnp.bfloat16), sk)
    return h, ids_t.T, gate_t.T


PEER_MASK_SLAB = PEER_NKEYS + 8
PEER_MASK_UNROLL = 8
PEER_KEYS_PER_STEP = 8


def _peer_expert_kernel(x_ref, ids_ref, gate_ref, u_ref, v_ref, o_ref, w_scr):
    f32, bf16 = jnp.float32, jnp.bfloat16
    j = pl.program_id(1)
    tm = x_ref.shape[0]
    nt = (((1,), (1,)), ((), ()))

    @pl.when(j == 0)
    def _():
        o_ref[...] = jnp.zeros_like(o_ref)
        key_iota = lax.broadcasted_iota(jnp.int32, (PEER_NKEYS, ids_ref.shape[1]), 0)

        def token_mask(t):
            ids = ids_ref[pl.ds(t, 1), :]
            g = gate_ref[pl.ds(t, 1), :]
            g_hi = g.astype(bf16).astype(f32)
            g_lo = g - g_hi
            first = jnp.where((ids >> 7) == key_iota, 1.0, 0.0).astype(bf16)
            second = (ids & (PEER_NKEYS - 1)) == key_iota
            w_t = (lax.dot_general(first, jnp.where(second, g_hi, 0.0).astype(bf16), nt, preferred_element_type=f32)
                   + lax.dot_general(first, jnp.where(second, g_lo, 0.0).astype(bf16), nt,
                                     preferred_element_type=f32))
            w_scr[pl.ds(pl.multiple_of(t * PEER_MASK_SLAB, 8), PEER_NKEYS), :] = w_t

        def token_group(i, carry):
            for s in range(PEER_MASK_UNROLL):
                token_mask(i * PEER_MASK_UNROLL + s)
            return carry

        lax.fori_loop(0, tm // PEER_MASK_UNROLL, token_group, 0)

    a = lax.dot_general(x_ref[...], u_ref[...], nt, preferred_element_type=f32)
    w = jnp.concatenate([w_scr[pl.ds(j * PEER_KEYS_PER_STEP + s, tm, stride=PEER_MASK_SLAB), :]
                         for s in range(PEER_KEYS_PER_STEP)], axis=1)
    gelu = 0.5 * a * (1.0 + lax.erf(a * (2.0 ** -0.5)))
    c = (w * gelu).astype(bf16)
    o_ref[...] += jnp.dot(c, v_ref[...], preferred_element_type=f32)


def _peer_expert(tok, ids, gate, u, v, tm=256):
    T, D = tok.shape
    te = PEER_KEYS_PER_STEP * PEER_NKEYS
    nk = ids.shape[1]
    assert PEER_NKEYS == 128 and T % tm == 0 and PEER_EXPERTS % te == 0
    return pl.pallas_call(
        _peer_expert_kernel,
        grid=(T // tm, PEER_EXPERTS // te),
        in_specs=[pl.BlockSpec((tm, D), lambda i, j: (i, 0)),
                  pl.BlockSpec((tm, nk), lambda i, j: (i, 0)),
                  pl.BlockSpec((tm, nk), lambda i, j: (i, 0)),
                  pl.BlockSpec((te, D), lambda i, j: (j, 0)),
                  pl.BlockSpec((te, D), lambda i, j: (j, 0))],
        out_specs=pl.BlockSpec((tm, D), lambda i, j: (i, 0)),
        out_shape=jax.ShapeDtypeStruct((T, D), jnp.float32),
        scratch_shapes=[pltpu.VMEM((tm * PEER_MASK_SLAB, PEER_NKEYS), jnp.float32)],
        compiler_params=pltpu.CompilerParams(dimension_semantics=("arbitrary", "arbitrary"),
                                             vmem_limit_bytes=48 * 1024 * 1024),
        name="peer_expert",
    )(tok, ids, gate, u, v)


def _peer(x2d, scale, shift, rows_per_mod, wq, subkeys, u, v):
    tok, ids, gate = _peer_route(x2d, scale, shift, rows_per_mod, wq, subkeys)
    return _peer_expert(tok, ids, gate, u, v)


def _in_proj_kernel(x_ref, sc_ref, sh_ref, w_ref, o_ref, a_scr):
    @pl.when(pl.program_id(1) == 0)
    def _():
        a_scr[...] = (x_ref[...] * sc_ref[0] + sh_ref[0]).astype(a_scr.dtype)

    o_ref[...] = jnp.dot(a_scr[...], w_ref[...], preferred_element_type=jnp.float32).astype(o_ref.dtype)


def _in_proj(x2d, scale, shift, rows_per_mod, w, col0, ncols, out_dtype, tm=1024, tn=512):
    M, K = x2d.shape
    tm = min(tm, M)
    assert col0 % tn == 0 and M % tm == 0
    mod = lambda i, j: (i * tm // rows_per_mod, 0, 0)
    return pl.pallas_call(
        _in_proj_kernel,
        grid=(M // tm, pl.cdiv(ncols, tn)),
        in_specs=[pl.BlockSpec((tm, K), lambda i, j: (i, 0)),
                  pl.BlockSpec((1, 1, K), mod),
                  pl.BlockSpec((1, 1, K), mod),
                  pl.BlockSpec((K, tn), lambda i, j: (0, col0 // tn + j))],
        out_specs=pl.BlockSpec((tm, tn), lambda i, j: (i, j)),
        out_shape=jax.ShapeDtypeStruct((M, ncols), out_dtype),
        scratch_shapes=[pltpu.VMEM((tm, K), jnp.bfloat16)],
        compiler_params=pltpu.CompilerParams(dimension_semantics=("arbitrary", "arbitrary"),
                                             vmem_limit_bytes=48 * 1024 * 1024),
        name="in_proj",
    )(x2d, scale, shift, w)


def _out_proj_ln_kernel(a_ref, b_ref, c_ref, d_ref, w_ref, x_ref, gt_ref, g_ref, bt_ref, o_ref):
    f32 = jnp.float32
    W = GROUP_WIDTH
    y = jnp.zeros(x_ref.shape, f32)
    for g, r in enumerate((a_ref, b_ref, c_ref, d_ref)):
        y = y + jnp.dot(r[...].astype(jnp.bfloat16), w_ref[g * W:(g + 1) * W, :], preferred_element_type=f32)
    z = DEEPNORM_ALPHA * x_ref[...] + gt_ref[0] * y
    mu = jnp.mean(z, axis=-1, keepdims=True)
    var = jnp.mean(jnp.square(z - mu), axis=-1, keepdims=True)
    o_ref[...] = (z - mu) * lax.rsqrt(var + 1e-5) * g_ref[...] + bt_ref[...]


def _out_proj_ln(parts, w, x2d, gate, rows_per_mod, ln_g, ln_b, tm=256):
    M, D = x2d.shape
    W = GROUP_WIDTH
    tm = min(tm, M)
    row = lambda i: (i, 0)
    return pl.pallas_call(
        _out_proj_ln_kernel,
        grid=(M // tm,),
        in_specs=[pl.BlockSpec((tm, W), row)] * 4
                 + [pl.BlockSpec(w.shape, lambda i: (0, 0)),
                    pl.BlockSpec((tm, D), row),
                    pl.BlockSpec((1, 1, D), lambda i: (i * tm // rows_per_mod, 0, 0)),
                    pl.BlockSpec((1, D), lambda i: (0, 0)),
                    pl.BlockSpec((1, D), lambda i: (0, 0))],
        out_specs=pl.BlockSpec((tm, D), row),
        out_shape=jax.ShapeDtypeStruct((M, D), jnp.float32),
        compiler_params=pltpu.CompilerParams(vmem_limit_bytes=48 * 1024 * 1024),
        name="out_proj_ln",
    )(*parts, w, x2d, gate, ln_g.reshape(1, D), ln_b.reshape(1, D))


def kernel(x, c, ctx, c_ctx, w_ada, b_ada, w_in, w_out, ln_g, ln_b, hg_lower_bounds, hg_norm_w,
           hy_conv_w, hy_conv_b, hy_w1, hy_b1, hy_w2, hy_b2, hy_w3, hy_freq, hy_dbias, na_rpb,
           ml_conv_w, ml_conv_b, ml_gate_b, peer_wq, peer_subkeys, peer_u, peer_v):
    f32, bf16 = jnp.float32, jnp.bfloat16
    B, L, D = x.shape
    n_ctx = ctx.shape[1]
    lb_soft = jax.nn.softmax(hg_lower_bounds.astype(f32), axis=0)
    lower_bounds = jnp.cumsum(lb_soft, axis=0) - lb_soft[0]
    cond_lat = jax.nn.silu(c)
    cond_ctx = jax.nn.silu(c_ctx)
    c0_na = HG_COLS + HY_COLS
    c0_ml = c0_na + NA_COLS
    x2 = x.reshape(B * L, D)
    dft_lat = _hyena_dft_matrices(L)
    xc2 = ctx.reshape(B * n_ctx, D)
    for l in range(DEPTH):
        need_ctx = l < DEPTH - 1
        m_lat = (cond_lat @ w_ada[l] + b_ada[l]).reshape(B, 1, 6, D)
        m_ctx = (cond_ctx @ w_ada[l] + b_ada[l]).reshape(1, 1, 6, D)
        ml = [m_lat[:, :, k] for k in range(6)]
        mc = [m_ctx[:, :, k] for k in range(6)]
        w_in_l = w_in[l].astype(bf16)
        w_out_l = w_out[l].astype(bf16)

        def in_proj(t, m, rows_per_mod):
            sc, sh = 1 + m[1], m[0]
            return (_in_proj(t, sc, sh, rows_per_mod, w_in_l, 0, c0_na, f32),
                    _in_proj(t, sc, sh, rows_per_mod, w_in_l, c0_na, NA_COLS, bf16),
                    _in_proj(t, sc, sh, rows_per_mod, w_in_l, c0_ml, ML_COLS, f32))

        ph_l, pn_l, pm_l = in_proj(x2, ml, L)
        ph_c, pn_c, pm_c = in_proj(xc2, mc, B * n_ctx)
        pa_l, pb_l, pd_l = (ph_l[:, :HG_COLS].reshape(B, L, -1), ph_l[:, HG_COLS:].reshape(B, L, -1),
                            pm_l.reshape(B, L, -1))
        pa_c, pb_c, pd_c = (ph_c[:, :HG_COLS].reshape(B, n_ctx, -1), ph_c[:, HG_COLS:].reshape(B, n_ctx, -1),
                            pm_c.reshape(B, n_ctx, -1))
        hy = (hy_conv_w[l], hy_conv_b[l], hy_w1[l], hy_b1[l], hy_w2[l], hy_b2[l], hy_w3[l], hy_freq[l], hy_dbias[l])
        a_l, a_c = _hgrn2_group(pa_l, pa_c, lower_bounds[l], hg_norm_w[l], need_ctx)
        b_l = _hyena_seq(pb_l, dft_lat, *hy)
        c_l, c_c = _natten_pallas(pn_l, pn_c, na_rpb[l], need_ctx, B)
        d_l, d_c = _mlstm_group(pd_l, pd_c, ml_conv_w[l], ml_conv_b[l], ml_gate_b[l], need_ctx)
        parts_l = (a_l.reshape(B * L, -1), b_l.reshape(B * L, -1), c_l, d_l.reshape(B * L, -1))
        if need_ctx:
            b_c = _hyena_seq(pb_c, _hyena_dft_matrices(n_ctx), *hy)
            parts_c = (a_c.reshape(B * n_ctx, -1), b_c.reshape(B * n_ctx, -1), c_c, d_c.reshape(B * n_ctx, -1))
            xc2 = _out_proj_ln(parts_c, w_out_l, xc2, mc[2], B * n_ctx, ln_g[l, 0], ln_b[l, 0])
        x2 = _out_proj_ln(parts_l, w_out_l, x2, ml[2], L, ln_g[l, 0], ln_b[l, 0])
        u_l, v_l = peer_u[l].astype(bf16), peer_v[l].astype(bf16)
        y_lat = _peer(x2, 1 + ml[4], ml[3], L, peer_wq[l], peer_subkeys[l], u_l, v_l)
        x2 = _layer_norm(DEEPNORM_ALPHA * x2.reshape(B, L, D) + ml[5] * y_lat.reshape(B, L, D),
                         ln_g[l, 1], ln_b[l, 1]).reshape(B * L, D)
        if need_ctx:
            y_ctx = _peer(xc2, 1 + mc[4], mc[3], B * n_ctx, peer_wq[l], peer_subkeys[l], u_l, v_l)
            xc2 = _layer_norm(DEEPNORM_ALPHA * xc2 + mc[5][0] * y_ctx, ln_g[l, 1], ln_b[l, 1])
    return x2.reshape(B, L, D)
```

```python
import functools
import math
import jax
import jax.numpy as jnp
from jax import lax
from jax.experimental import pallas as pl
from jax.experimental.pallas import tpu as pltpu

D_MODEL = 2048
BATCH = 4
SEQ = 4096
DEPTH = 2

CTX_LEN = 256
GRID_W = 64
GROUP_WIDTH = D_MODEL // 4
MIX_WIDTH = 4 * GROUP_WIDTH
CHUNK = 64
HG_DIM = 128
HG_HEADS = GROUP_WIDTH // HG_DIM
HG_F_MIN = 1e-30
HY_WIDTH = GROUP_WIDTH
HY_ORDER = 2
HY_SHORT = 3
HY_BANDS = 16
HY_EMB = 1 + 2 * HY_BANDS
HY_HIDDEN = 64
HY_FAST_DECAY = 0.3
HY_SLOW_DECAY = 1.5
HY_DECAY_TARGET = 1e-2
HY_MIN_DECAY = math.log(HY_DECAY_TARGET) / HY_SLOW_DECAY
HY_MAX_DECAY = math.log(HY_DECAY_TARGET) / HY_FAST_DECAY
NA_DIM = 64
NA_HEADS = GROUP_WIDTH // NA_DIM
NA_WIN_ROWS = 8
NA_WIN_COLS = 16
NA_QBLK = 16
NA_KBLK = 32
ML_DIM = 128
ML_HEADS = GROUP_WIDTH // ML_DIM
ML_SHORT = 3
ROPE_THETA = 10000.0
PEER_HEADS = 8
PEER_NKEYS = 128
PEER_EXPERTS = PEER_NKEYS * PEER_NKEYS
PEER_DK = 256
PEER_TOPK = 16
PEER_TOKEN_BLOCK = 64
HG_COLS = 5 * GROUP_WIDTH
HY_COLS = 3 * GROUP_WIDTH
NA_COLS = 3 * GROUP_WIDTH
ML_COLS = 4 * GROUP_WIDTH + 4 * ML_HEADS
IN_WIDTH = HG_COLS + HY_COLS + NA_COLS + ML_COLS
DEEPNORM_ALPHA = (2 * DEPTH) ** 0.25
DEEPNORM_BETA = (8 * DEPTH) ** -0.25
MASK_VALUE = -1e30


def _layer_norm(x, g, b, eps=1e-5):
    xf = x.astype(jnp.float32)
    mu = jnp.mean(xf, axis=-1, keepdims=True)
    var = jnp.mean(jnp.square(xf - mu), axis=-1, keepdims=True)
    y = (xf - mu) * lax.rsqrt(var + eps) * g.astype(jnp.float32) + b.astype(jnp.float32)
    return y.astype(x.dtype)


def _rms_norm(x, g, eps=1e-6):
    xf = x.astype(jnp.float32)
    return xf * lax.rsqrt(jnp.mean(jnp.square(xf), axis=-1, keepdims=True) + eps) * g.astype(jnp.float32)


def _centred_dwconv(x, w, b):
    k, L = w.shape[0], x.shape[1]
    xp = jnp.pad(x, ((0, 0), (k // 2, k // 2), (0, 0)))
    y = sum(xp[:, j:j + L] * w[j].astype(x.dtype) for j in range(k))
    return y + b.astype(x.dtype)


def _axial_rope(x):
    L, dh = x.shape[1], x.shape[-1]
    half, quarter = dh // 2, dh // 4
    t = jnp.arange(L)
    inv_freq = ROPE_THETA ** (-jnp.arange(quarter, dtype=jnp.float32) / quarter)

    def rot(xa, pos):
        ang = pos.astype(jnp.float32)[:, None] * inv_freq
        cos = jnp.cos(ang)[None, :, None, :]
        sin = jnp.sin(ang)[None, :, None, :]
        x1, x2 = xa[..., :quarter], xa[..., quarter:]
        return jnp.concatenate([x1 * cos - x2 * sin, x2 * cos + x1 * sin], axis=-1)

    return jnp.concatenate([rot(x[..., :half], t // GRID_W), rot(x[..., half:], t % GRID_W)], axis=-1)


def _chunks(x):
    b, h, L = x.shape[:3]
    return jnp.moveaxis(x.reshape(b, h, L // CHUNK, CHUNK, *x.shape[3:]), 2, 0)


def _unchunks(x):
    x = jnp.moveaxis(x, 0, 2)
    return x.reshape(x.shape[0], x.shape[1], -1, *x.shape[4:])


def _flip(t):
    return jnp.flip(t, axis=2)


HG_SUB = 16
HG_BLOCK = 256


def _hgrn2_kernel(*refs, reverse, readout):
    if readout:
        q_ref, z_ref, v_ref, lb_ref, s0_ref, g_ref, nw_ref, of_ref, o_ref, sfin_ref, st_scr = refs
    else:
        q_ref, z_ref, v_ref, lb_ref, s0_ref, o_ref, sfin_ref, st_scr = refs
    f32, bf16 = jnp.float32, jnp.bfloat16
    hi = lax.Precision.HIGHEST
    C, S = CHUNK, HG_SUB
    nsub = C // S
    n_chunks = q_ref.shape[0] // C
    step = pl.program_id(2)

    @pl.when(step == 0)
    def _():
        st_scr[...] = s0_ref[0, 0]

    nt = (((1,), (1,)), ((), ()))
    tn = (((0,), (0,)), ((), ()))
    r_i = lax.broadcasted_iota(jnp.int32, (C, C), 0)
    c_i = lax.broadcasted_iota(jnp.int32, (C, C), 1)
    tri = jnp.where((c_i >= r_i) if reverse else (c_i <= r_i), 1.0, 0.0).astype(f32)
    row_in_sub = lax.broadcasted_iota(jnp.int32, (S, HG_DIM), 0)
    lb = lb_ref[...]

    def chunk(ci, st):
        c = (n_chunks - 1 - ci) if reverse else ci
        rs = slice(c * C, (c + 1) * C)
        z, q, v = z_ref[rs, :], q_ref[rs, :], v_ref[rs, :]
        neg = jax.nn.sigmoid(-z)
        f = jax.nn.sigmoid(z) + lb * neg
        lf = jnp.log(jnp.maximum(f, HG_F_MIN))
        k = (1.0 - lb) * neg
        cum = jnp.dot(tri, lf, precision=hi, preferred_element_type=f32)
        total = cum[0:1] if reverse else cum[C - 1:C]
        o = lax.dot_general((q * jnp.exp(cum)).astype(bf16), st.astype(bf16), nt, preferred_element_type=f32)
        kd_end = k * jnp.exp(total - cum)
        st_new = jnp.exp(total) * st + lax.dot_general(v.astype(bf16), kd_end.astype(bf16), tn,
                                                       preferred_element_type=f32)
        parts = []
        for i in range(nsub):
            ts = slice(i * S, (i + 1) * S)
            q_i, cum_i = q[ts], cum[ts]
            acc = o[ts]
            ps = slice((i + 1) * S, C) if reverse else slice(0, i * S)
            if ps.stop > ps.start:
                ref = cum[(i + 1) * S:(i + 1) * S + 1] if reverse else cum[i * S - 1:i * S]
                a = lax.dot_general(q_i * jnp.exp(cum_i - ref), k[ps] * jnp.exp(ref - cum[ps]), nt,
                                    precision=hi, preferred_element_type=f32)
                acc = acc + jnp.dot(a.astype(bf16), v[ps].astype(bf16), preferred_element_type=f32)
            for s in range(S):
                r = i * S + s
                seen = (row_in_sub <= s) if reverse else (row_in_sub >= s)
                e = jnp.exp(jnp.where(seen, cum_i - cum[r:r + 1], MASK_VALUE))
                w = jnp.sum(q_i * k[r:r + 1] * e, axis=-1, keepdims=True)
                acc = acc + w * v[r:r + 1]
            parts.append(acc)
        o_c = jnp.concatenate(parts, axis=0)
        if readout:
            x = o_c + of_ref[rs, :]
            y = x * lax.rsqrt(jnp.mean(jnp.square(x), axis=-1, keepdims=True) + 1e-6) * nw_ref[...]
            o_ref[rs, :] = (y * jax.nn.silu(g_ref[rs, :])).astype(o_ref.dtype)
        else:
            o_ref[rs, :] = o_c.astype(o_ref.dtype)
        return st_new

    st = st_scr[...]
    for ci in range(n_chunks):
        st = chunk(ci, st)
    st_scr[...] = st

    @pl.when(step == pl.num_programs(2) - 1)
    def _():
        sfin_ref[0, 0] = st


def _hgrn2_scan_call(p, batch, lb, s0, reverse, z_group, readout_args=None):
    L = p.shape[0] // batch
    tb = min(HG_BLOCK, L)
    n_t = L // tb
    assert L % tb == 0 and tb % CHUNK == 0
    H = HG_HEADS

    def rows(b, h, t):
        return b * n_t + ((n_t - 1 - t) if reverse else t)

    def group(g):
        return pl.BlockSpec((tb, HG_DIM), lambda b, h, t: (rows(b, h, t), g * H + h))

    head_cols = pl.BlockSpec((1, HG_DIM), lambda b, h, t: (0, h))
    state = pl.BlockSpec((1, 1, HG_DIM, HG_DIM), lambda b, h, t: (b, h, 0, 0))
    out_rows = pl.BlockSpec((tb, HG_DIM), lambda b, h, t: (rows(b, h, t), h))
    in_specs = [group(0), group(z_group), group(3), head_cols, state]
    args = [p, p, p, lb, s0]
    readout = readout_args is not None
    if readout:
        norm_w, o_other = readout_args
        in_specs += [group(4), pl.BlockSpec((1, HG_DIM), lambda b, h, t: (0, 0)), out_rows]
        args += [p, norm_w.reshape(1, HG_DIM), o_other]
    return pl.pallas_call(
        functools.partial(_hgrn2_kernel, reverse=reverse, readout=readout),
        grid=(batch, H, n_t),
        in_specs=in_specs,
        out_specs=[out_rows, state],
        out_shape=[jax.ShapeDtypeStruct((batch * L, GROUP_WIDTH), jnp.bfloat16 if readout else jnp.float32),
                   jax.ShapeDtypeStruct((batch, H, HG_DIM, HG_DIM), jnp.float32)],
        scratch_shapes=[pltpu.VMEM((HG_DIM, HG_DIM), jnp.float32)],
        compiler_params=pltpu.CompilerParams(dimension_semantics=("arbitrary", "arbitrary", "arbitrary")),
        name="hgrn2_bwd" if reverse else "hgrn2_fwd",
    )(*args)


def _hgrn2_pallas(p_lat, p_ctx, batch, lb, norm_w, need_ctx):
    zero = jnp.zeros((batch, HG_HEADS, HG_DIM, HG_DIM), jnp.float32)
    lb_f, lb_b = lb[0:1], lb[1:2]
    oc_f, s_f = _hgrn2_scan_call(p_ctx, batch, lb_f, zero, False, 1)
    y_ctx, s_b = _hgrn2_scan_call(p_ctx, batch, lb_b, zero, True, 2, (norm_w, oc_f) if need_ctx else None)
    ol_f, _ = _hgrn2_scan_call(p_lat, batch, lb_f, s_f, False, 1)
    y_lat, _ = _hgrn2_scan_call(p_lat, batch, lb_b, s_b, True, 2, (norm_w, ol_f))
    return y_lat, (y_ctx if need_ctx else None)


def _hyena_filters(L, w1, b1, w2, b2, w3, freq):
    f32 = jnp.float32
    pos = jnp.arange(L, dtype=f32)
    t = pos / (L - 1)
    bands = jnp.linspace(1e-4, HY_BANDS - 1, HY_BANDS, dtype=f32)
    ang = (2.0 * math.pi / L) * pos[:, None] * bands[None, :]
    z = jnp.concatenate([t[:, None], jnp.cos(ang), jnp.sin(ang)], axis=-1)
    freq = freq.astype(f32)
    hid = jnp.sin(freq[0] * (z @ w1.astype(f32) + b1.astype(f32)))
    hid = jnp.sin(freq[1] * (hid @ w2.astype(f32) + b2.astype(f32)))
    h = (hid @ w3.astype(f32)).reshape(L, 2, HY_ORDER, HY_WIDTH)
    deltas = jnp.abs(jnp.linspace(HY_MIN_DECAY, HY_MAX_DECAY, HY_WIDTH, dtype=f32))
    h = h * jnp.exp(-t[:, None] * deltas)[:, None, None, :]
    fwd, bwd = h[:, 0], h[:, 1]
    bwd = bwd.at[0].set(0.0)
    norm = jnp.sum(jnp.abs(fwd), axis=0, keepdims=True) + jnp.sum(jnp.abs(bwd), axis=0, keepdims=True)
    return fwd / norm, bwd / norm


HY_FREQ_TILE = 384
HY_TIME_TILE = 256
HY_ANGLE_SPLIT = 64


def _dft_angle_tables(rows, cols, period):
    f32 = jnp.float32
    r = jnp.arange(rows, dtype=jnp.int32)[:, None]
    ch = jnp.arange(cols // HY_ANGLE_SPLIT, dtype=jnp.int32)[None, :] * HY_ANGLE_SPLIT
    cl = jnp.arange(HY_ANGLE_SPLIT, dtype=jnp.int32)[None, :]
    ah = ((r * ch) % period).astype(f32) * (2.0 * math.pi / period)
    al = ((r * cl) % period).astype(f32) * (2.0 * math.pi / period)
    ca, sa, cb, sb = jnp.cos(ah)[:, :, None], jnp.sin(ah)[:, :, None], jnp.cos(al)[:, None, :], jnp.sin(al)[:, None, :]
    return (ca * cb - sa * sb).reshape(rows, cols), (sa * cb + ca * sb).reshape(rows, cols)


def _hyena_dft_matrices(L):
    N = 2 * L
    mp = -(-(L + 1) // HY_FREQ_TILE) * HY_FREQ_TILE
    bf16 = jnp.bfloat16
    k = jnp.arange(mp)
    live = (k <= L).astype(jnp.float32)
    cf, sf = _dft_angle_tables(mp, L, N)
    ci, si = _dft_angle_tables(L, mp, N)
    w = jnp.where((k == 0) | (k == L), 1.0, 2.0) * live / N
    return ((cf * live[:, None]).astype(bf16), (sf * live[:, None]).astype(bf16),
            (ci * w[None, :]).astype(bf16), (-si * w[None, :]).astype(bf16))


def _dft_pair_kernel(c_ref, s_ref, a_ref, b_ref, oc_ref, os_ref):
    oc_ref[...] = jnp.dot(c_ref[...], a_ref[...], preferred_element_type=jnp.float32)
    os_ref[...] = jnp.dot(s_ref[...], b_ref[...], preferred_element_type=jnp.float32)


def _hyena_filter_spectrum(cf, sf, fwd, bwd):
    mp, L = cf.shape
    C = fwd.shape[1]
    bf16 = jnp.bfloat16
    col = lambda i: (0, 0)
    re, s = pl.pallas_call(
        _dft_pair_kernel,
        grid=(mp // HY_FREQ_TILE,),
        in_specs=[pl.BlockSpec((HY_FREQ_TILE, L), lambda i: (i, 0))] * 2 + [pl.BlockSpec((L, C), col)] * 2,
        out_specs=[pl.BlockSpec((HY_FREQ_TILE, C), lambda i: (i, 0))] * 2,
        out_shape=[jax.ShapeDtypeStruct((mp, C), jnp.float32)] * 2,
        compiler_params=pltpu.CompilerParams(vmem_limit_bytes=48 * 1024 * 1024),
        name="hyena_filter_spectrum",
    )(cf, sf, (fwd + bwd).astype(bf16), (fwd - bwd).astype(bf16))
    return re, -s


def _hyena_fwd_kernel(c_ref, s_ref, z_ref, kre_ref, kim_ref, pre_ref, pim_ref):
    f32 = jnp.float32
    xc = jnp.dot(c_ref[...], z_ref[0], preferred_element_type=f32)
    xs = jnp.dot(s_ref[...], z_ref[0], preferred_element_type=f32)
    pre_ref[0] = (xc * kre_ref[...] + xs * kim_ref[...]).astype(pre_ref.dtype)
    pim_ref[0] = (xc * kim_ref[...] - xs * kre_ref[...]).astype(pim_ref.dtype)


def _hyena_inv_kernel(ci_ref, si_ref, pre_ref, pim_ref, z_ref, d_ref, m_ref, o_ref, ob_ref):
    f32 = jnp.float32
    y = (jnp.dot(ci_ref[...], pre_ref[0], preferred_element_type=f32)
         + jnp.dot(si_ref[...], pim_ref[0], preferred_element_type=f32))
    out = m_ref[0] * (y + z_ref[0] * d_ref[...])
    o_ref[0] = out
    ob_ref[0] = out.astype(ob_ref.dtype)


def _hyena_long_conv(mats, z, z_b, kre, kim, d, mult):
    cf, sf, ci, si = mats
    B, L, C = z.shape
    mp = cf.shape[0]
    tt = min(HY_TIME_TILE, L)
    bf16 = jnp.bfloat16
    pre, pim = pl.pallas_call(
        _hyena_fwd_kernel,
        grid=(mp // HY_FREQ_TILE, B),
        in_specs=[pl.BlockSpec((HY_FREQ_TILE, L), lambda i, b: (i, 0))] * 2
                 + [pl.BlockSpec((1, L, C), lambda i, b: (b, 0, 0))]
                 + [pl.BlockSpec((HY_FREQ_TILE, C), lambda i, b: (i, 0))] * 2,
        out_specs=[pl.BlockSpec((1, HY_FREQ_TILE, C), lambda i, b: (b, i, 0))] * 2,
        out_shape=[jax.ShapeDtypeStruct((B, mp, C), bf16)] * 2,
        compiler_params=pltpu.CompilerParams(vmem_limit_bytes=48 * 1024 * 1024),
        name="hyena_fwd",
    )(cf, sf, z_b, kre, kim)
    row = lambda b, t: (b, t, 0)
    return pl.pallas_call(
        _hyena_inv_kernel,
        grid=(B, L // tt),
        in_specs=[pl.BlockSpec((tt, mp), lambda b, t: (t, 0))] * 2
                 + [pl.BlockSpec((1, mp, C), lambda b, t: (b, 0, 0))] * 2
                 + [pl.BlockSpec((1, tt, C), row), pl.BlockSpec((1, C), lambda b, t: (0, 0)),
                    pl.BlockSpec((1, tt, C), row)],
        out_specs=[pl.BlockSpec((1, tt, C), row)] * 2,
        out_shape=[jax.ShapeDtypeStruct((B, L, C), jnp.float32), jax.ShapeDtypeStruct((B, L, C), bf16)],
        compiler_params=pltpu.CompilerParams(vmem_limit_bytes=48 * 1024 * 1024),
        name="hyena_inv",
    )(ci, si, pre, pim, z, d.reshape(1, C).astype(jnp.float32), mult)


def _hyena_seq(p, mats, conv_w, conv_b, w1, b1, w2, b2, w3, freq, dbias):
    u = _centred_dwconv(p, conv_w, conv_b).astype(jnp.float32)
    v, x1, x2 = jnp.split(u, 3, axis=-1)
    L = p.shape[1]
    fwd, bwd = _hyena_filters(L, w1, b1, w2, b2, w3, freq)
    kre, kim = _hyena_filter_spectrum(mats[0], mats[1], fwd.reshape(L, -1), bwd.reshape(L, -1))
    W = HY_WIDTH
    z, z_b = _hyena_long_conv(mats, v, v.astype(jnp.bfloat16), kre[:, :W], kim[:, :W], dbias[0], x1)
    return _hyena_long_conv(mats, z, z_b, kre[:, W:], kim[:, W:], dbias[1], x2)[1]


def _natten_group(p_lat, p_ctx, rpb, need_ctx):
    f32 = jnp.float32
    B, L, _ = p_lat.shape

    def heads(p):
        return [t.reshape(t.shape[0], t.shape[1], NA_HEADS, NA_DIM) for t in jnp.split(p, 3, axis=-1)]

    q, k, v = heads(p_lat)
    qc, kc, vc = heads(p_ctx)
    scale = NA_DIM ** -0.5
    rows = L // GRID_W
    kh = min(NA_WIN_ROWS, rows)
    ncb = GRID_W // NA_QBLK
    nk = kh * NA_KBLK
    r = jnp.arange(rows)
    key_rows = jnp.clip(r - kh // 2, 0, rows - kh)[:, None] + jnp.arange(kh)
    blk = jnp.arange(ncb)
    qcol = blk[:, None] * NA_QBLK + jnp.arange(NA_QBLK)
    key_cols = (jnp.clip(blk * NA_QBLK - NA_WIN_COLS // 2, 0, GRID_W - NA_KBLK)[:, None]
                + jnp.arange(NA_KBLK))
    idx = (key_rows[:, None, :, None] * GRID_W + key_cols[None, :, None, :]).reshape(rows, ncb, nk)
    col_start = jnp.clip(qcol - NA_WIN_COLS // 2, 0, GRID_W - NA_WIN_COLS)[:, :, None, None]
    kcol = key_cols[:, None, None, :]
    valid = jnp.broadcast_to((kcol >= col_start) & (kcol < col_start + NA_WIN_COLS),
                             (ncb, NA_QBLK, kh, NA_KBLK)).reshape(ncb, NA_QBLK, nk)
    dc = jnp.clip(kcol - qcol[:, :, None, None], 1 - NA_WIN_COLS, NA_WIN_COLS - 1)
    dr = key_rows - r[:, None]
    bias = rpb.astype(f32)[:, dr[:, None, None, :, None] + NA_WIN_ROWS - 1, dc[None] + NA_WIN_COLS - 1]
    bias = jnp.moveaxis(bias.reshape(NA_HEADS, rows, ncb, NA_QBLK, nk), 1, 0)
    q_rows = jnp.moveaxis((q * scale).reshape(B, rows, ncb, NA_QBLK, NA_HEADS, NA_DIM), 1, 0)

    def row_block(inp):
        q_r, idx_r, bias_r = inp
        k_r, v_r = k[:, idx_r], v[:, idx_r]
        s_win = jnp.einsum('bjqhd,bjkhd->bhjqk', q_r, k_r).astype(f32)
        s_win = jnp.where(valid, s_win + bias_r, MASK_VALUE)
        s_ctx = jnp.einsum('bjqhd,bchd->bhjqc', q_r, kc).astype(f32)
        prob = jax.nn.softmax(jnp.concatenate([s_win, s_ctx], axis=-1), axis=-1).astype(v.dtype)
        return (jnp.einsum('bhjqk,bjkhd->bjqhd', prob[..., :nk], v_r)
                + jnp.einsum('bhjqc,bchd->bjqhd', prob[..., nk:], vc))

    out = lax.map(row_block, (q_rows, idx, bias))
    y_lat = jnp.moveaxis(out, 0, 1).reshape(B, L, GROUP_WIDTH)
    y_ctx = None
    if need_ctx:
        s = jnp.einsum('bqhd,bkhd->bhqk', qc * scale, kc).astype(f32)
        prob = jax.nn.softmax(s, axis=-1).astype(vc.dtype)
        y_ctx = jnp.einsum('bhqk,bkhd->bqhd', prob, vc).reshape(B, -1, GROUP_WIDTH)
    return y_lat, y_ctx


def _na_pair_attention(q_pair, key_sets, lane):
    f32 = jnp.float32
    nt = (((1,), (1,)), ((), ()))
    outs = []
    for half in range(2):
        own = (lane >= NA_DIM) if half else (lane < NA_DIM)
        q = jnp.where(own, q_pair, jnp.zeros_like(q_pair))
        scores = []
        for k_pair, _, add, valid in key_sets:
            s = lax.dot_general(q, k_pair, nt, preferred_element_type=f32) * (NA_DIM ** -0.5)
            if add is not None:
                s = jnp.where(valid, s + add[half], MASK_VALUE)
            scores.append(s)
        m = scores[0].max(axis=-1, keepdims=True)
        for s in scores[1:]:
            m = jnp.maximum(m, s.max(axis=-1, keepdims=True))
        den = jnp.zeros_like(m)
        acc = jnp.zeros(q_pair.shape, f32)
        for s, (_, v_pair, _, _) in zip(scores, key_sets):
            p = jnp.exp(s - m)
            den = den + p.sum(axis=-1, keepdims=True)
            acc = acc + jnp.dot(p.astype(jnp.bfloat16), v_pair, preferred_element_type=f32)
        outs.append(acc / den)
    return jnp.where(lane < NA_DIM, outs[0], outs[1])


def _natten_lat_kernel(q_ref, k_ref, v_ref, kc_ref, vc_ref, bias_ref, o_ref, *, rows):
    kh = NA_WIN_ROWS
    r = pl.program_id(1)
    start = pl.multiple_of(jnp.clip(r - kh // 2, 0, rows - kh) * GRID_W, GRID_W)
    nk = kh * GRID_W
    qcol = lax.broadcasted_iota(jnp.int32, (GRID_W, nk), 0)
    kcol = lax.broadcasted_iota(jnp.int32, (GRID_W, nk), 1) % GRID_W
    col_start = jnp.clip(qcol - NA_WIN_COLS // 2, 0, GRID_W - NA_WIN_COLS)
    valid = (kcol >= col_start) & (kcol < col_start + NA_WIN_COLS)
    lane = lax.broadcasted_iota(jnp.int32, (GRID_W, 2 * NA_DIM), 1)
    for hp in range(NA_HEADS // 2):
        cs = slice(hp * 2 * NA_DIM, (hp + 1) * 2 * NA_DIM)
        win = (k_ref[pl.ds(start, nk), cs], v_ref[pl.ds(start, nk), cs],
               (bias_ref[0, 2 * hp], bias_ref[0, 2 * hp + 1]), valid)
        ctx = (kc_ref[:, cs], vc_ref[:, cs], None, None)
        o_ref[:, cs] = _na_pair_attention(q_ref[:, cs], [win, ctx], lane).astype(o_ref.dtype)


def _natten_ctx_kernel(q_ref, k_ref, v_ref, o_ref):
    lane = lax.broadcasted_iota(jnp.int32, (q_ref.shape[0], 2 * NA_DIM), 1)
    for hp in range(NA_HEADS // 2):
        cs = slice(hp * 2 * NA_DIM, (hp + 1) * 2 * NA_DIM)
        o_ref[:, cs] = _na_pair_attention(q_ref[:, cs], [(k_ref[:, cs], v_ref[:, cs], None, None)],
                                          lane).astype(o_ref.dtype)


def _natten_bias_table(rpb, rows):
    kh = NA_WIN_ROWS
    dc = jnp.clip(jnp.arange(GRID_W)[None, :] - jnp.arange(GRID_W)[:, None], 1 - NA_WIN_COLS, NA_WIN_COLS - 1)
    t = rpb.astype(jnp.float32)[:, :, dc + NA_WIN_COLS - 1]
    per_off = [t[:, d:d + kh].transpose(0, 2, 1, 3).reshape(NA_HEADS, GRID_W, kh * GRID_W) for d in range(kh)]
    return jnp.stack(per_off, axis=0)


def _natten_pallas(p_lat, p_ctx, rpb, need_ctx, batch):
    W = GROUP_WIDTH
    L = p_lat.shape[0] // batch
    n_ctx = p_ctx.shape[0] // batch
    rows = L // GRID_W
    kh = NA_WIN_ROWS
    assert rows >= kh and L % GRID_W == 0
    bias = _natten_bias_table(rpb, rows)

    def first_key_row_offset(b, r):
        return (jnp.clip(r - kh // 2, 0, rows - kh) - r + kh - 1, 0, 0, 0)

    y_lat = pl.pallas_call(
        functools.partial(_natten_lat_kernel, rows=rows),
        grid=(batch, rows),
        in_specs=[pl.BlockSpec((GRID_W, W), lambda b, r: (b * rows + r, 0)),
                  pl.BlockSpec((L, W), lambda b, r: (b, 1)),
                  pl.BlockSpec((L, W), lambda b, r: (b, 2)),
                  pl.BlockSpec((n_ctx, W), lambda b, r: (b, 1)),
                  pl.BlockSpec((n_ctx, W), lambda b, r: (b, 2)),
                  pl.BlockSpec((1, NA_HEADS, GRID_W, kh * GRID_W), first_key_row_offset)],
        out_specs=pl.BlockSpec((GRID_W, W), lambda b, r: (b * rows + r, 0)),
        out_shape=jax.ShapeDtypeStruct((batch * L, W), jnp.bfloat16),
        compiler_params=pltpu.CompilerParams(vmem_limit_bytes=40 * 1024 * 1024),
        name="natten_lat",
    )(p_lat, p_lat, p_lat, p_ctx, p_ctx, bias)
    y_ctx = None
    if need_ctx:
        y_ctx = pl.pallas_call(
            _natten_ctx_kernel,
            grid=(batch,),
            in_specs=[pl.BlockSpec((n_ctx, W), lambda b: (b, 0)),
                      pl.BlockSpec((n_ctx, W), lambda b: (b, 1)),
                      pl.BlockSpec((n_ctx, W), lambda b: (b, 2))],
            out_specs=pl.BlockSpec((n_ctx, W), lambda b: (b, 0)),
            out_shape=jax.ShapeDtypeStruct((batch * n_ctx, W), jnp.bfloat16),
            name="natten_ctx",
        )(p_ctx, p_ctx, p_ctx)
    return y_lat, y_ctx


def _mlstm_scan(q, k, v, ig, lf, state):
    causal = jnp.tril(jnp.ones((CHUNK, CHUNK), dtype=bool))

    def step(carry, inp):
        ckv, n, m = carry
        qc, kc, vc, igc, lfc = inp
        g = jnp.cumsum(lfc, axis=-1)
        logd = jnp.where(causal, g[..., :, None] - g[..., None, :] + igc[..., None, :], MASK_VALUE)
        log_inter = g + m[..., None]
        m_out = jnp.maximum(log_inter, jnp.max(logd, axis=-1))
        dmat = jnp.exp(logd - m_out[..., None])
        w_inter = jnp.exp(log_inter - m_out)
        sc = jnp.einsum('bhtd,bhsd->bhts', qc, kc) * dmat
        num = jnp.einsum('bhts,bhse->bhte', sc, vc) + w_inter[..., None] * jnp.einsum('bhtd,bhde->bhte', qc, ckv)
        den = jnp.sum(sc, axis=-1) + w_inter * jnp.einsum('bhtd,bhd->bht', qc, n)
        h = num / jnp.maximum(jnp.abs(den), jnp.exp(-m_out))[..., None]
        g_end = g[..., -1]
        a = g_end[..., None] - g + igc
        m_new = jnp.maximum(g_end + m, jnp.max(a, axis=-1))
        carry_w = jnp.exp(g_end + m - m_new)
        w = jnp.exp(a - m_new[..., None])
        ckv = carry_w[..., None, None] * ckv + jnp.einsum('bhs,bhsd,bhse->bhde', w, kc, vc)
        n = carry_w[..., None] * n + jnp.einsum('bhs,bhsd->bhd', w, kc)
        return (ckv, n, m_new), h

    state, h = lax.scan(step, state, (_chunks(q), _chunks(k), _chunks(v), _chunks(ig), _chunks(lf)))
    return _unchunks(h), state


def _mlstm_group(p_lat, p_ctx, conv_w, conv_b, gate_b, need_ctx):
    W = GROUP_WIDTH
    f32 = jnp.float32

    def prep(p, rope):
        B, L, _ = p.shape
        qk = jax.nn.silu(_centred_dwconv(p[..., :2 * W], conv_w, conv_b)).astype(f32)
        q = qk[..., :W].reshape(B, L, ML_HEADS, ML_DIM)
        k = qk[..., W:].reshape(B, L, ML_HEADS, ML_DIM)
        if rope:
            q, k = _axial_rope(q), _axial_rope(k)
        v = p[..., 2 * W:3 * W].astype(f32).reshape(B, L, ML_HEADS, ML_DIM)
        og = jax.nn.sigmoid(p[..., 3 * W:4 * W].astype(f32))
        gates = (p[..., 4 * W:].astype(f32) + gate_b.astype(f32)).reshape(B, L, 4, ML_HEADS)
        gates = gates.transpose(2, 0, 3, 1)

        def hf(t):
            return t.transpose(0, 2, 1, 3)

        return (hf(q), hf(k * ML_DIM ** -0.5), hf(v), gates[0], jax.nn.log_sigmoid(gates[1]),
                gates[2], jax.nn.log_sigmoid(gates[3]), og)

    def readout(h, og):
        return og * h.transpose(0, 2, 1, 3).reshape(og.shape)

    qc, kc, vc, icf, lcf, icb, lcb, oc = prep(p_ctx, False)
    B = qc.shape[0]
    s0 = (jnp.zeros((B, ML_HEADS, ML_DIM, ML_DIM), f32), jnp.zeros((B, ML_HEADS, ML_DIM), f32),
          jnp.zeros((B, ML_HEADS), f32))
    hc_f, st_f = _mlstm_scan(qc, kc, vc, icf, lcf, s0)
    hc_b, st_b = _mlstm_scan(_flip(qc), _flip(kc), _flip(vc), _flip(icb), _flip(lcb), s0)
    ql, kl, vl, ilf, llf, ilb, llb, ol = prep(p_lat, True)
    hl_f, _ = _mlstm_scan(ql, kl, vl, ilf, llf, st_f)
    hl_b, _ = _mlstm_scan(_flip(ql), _flip(kl), _flip(vl), _flip(ilb), _flip(llb), st_b)
    y_lat = readout(hl_f + _flip(hl_b), ol)
    y_ctx = readout(hc_f + _flip(hc_b), oc) if need_ctx else None
    return y_lat, y_ctx


PEER_SUB_DK = PEER_DK // 2
PEER_CAND_ROWS = 16 + 8 + 6 * 8 + 8


def _topk_rows(s, order, payload, k):
    sentinel = jnp.int32(2 ** 30)
    vals, picked = [], []
    for _ in range(k):
        m = s.max(axis=0, keepdims=True)
        first = jnp.min(jnp.where(s == m, order, sentinel), axis=0, keepdims=True)
        hit = order == first
        vals.append(m)
        picked.append(first if payload is None else jnp.max(jnp.where(hit, payload, -1), axis=0, keepdims=True))
        s = jnp.where(hit, -jnp.inf, s)
    return jnp.concatenate(vals, axis=0), jnp.concatenate(picked, axis=0)


def _peer_candidates(a0, a1, combine):
    pieces = [combine(a0[0:1], a1), combine(a0[1:2], a1[0:8])]
    pieces += [combine(a0[i:i + 1], a1[0:8]) for i in range(2, 8)]
    pieces.append(combine(a0[8:16], a1[0:1]))
    return jnp.concatenate(pieces, axis=0)


def _peer_topk_kernel(x_ref, sc_ref, sh_ref, wq_ref, sk_ref, h_ref, ids_ref, gate_ref):
    f32 = jnp.float32
    tb = x_ref.shape[0]
    h = x_ref[...] * sc_ref[0] + sh_ref[0]
    h_ref[...] = h.astype(h_ref.dtype)
    q = jnp.dot(h.astype(jnp.bfloat16), wq_ref[...], preferred_element_type=f32).astype(jnp.bfloat16)
    key_order = lax.broadcasted_iota(jnp.int32, (PEER_NKEYS, tb), 0)
    rho = lax.broadcasted_iota(jnp.int32, (PEER_CAND_ROWS, tb), 0)
    ci = jnp.where(rho < 16, 0, jnp.where(rho < 24, 1, jnp.where(rho < 72, 2 + ((rho - 24) >> 3), rho - 64)))
    cj = jnp.where(rho < 16, rho, jnp.where(rho < 24, rho - 16, jnp.where(rho < 72, (rho - 24) & 7, 0)))
    cand_order = ci * PEER_TOPK + cj
    cand_valid = (ci + 1) * (cj + 1) <= PEER_TOPK
    nt = (((1,), (1,)), ((), ()))
    for head in range(PEER_HEADS):
        sv, si = [], []
        for half in range(2):
            c0 = (head * 2 + half) * PEER_SUB_DK
            s_t = lax.dot_general(sk_ref[head * 2 + half], q[:, c0:c0 + PEER_SUB_DK], nt,
                                  preferred_element_type=f32)
            v, i = _topk_rows(s_t, key_order, None, PEER_TOPK)
            sv.append(v)
            si.append(i)
        cand_s = _peer_candidates(sv[0], sv[1], lambda a, b: a + b)
        cand_s = jnp.where(cand_valid, cand_s, -jnp.inf)
        cand_id = _peer_candidates(si[0], si[1], lambda a, b: a * PEER_NKEYS + b)
        best_s, ids = _topk_rows(cand_s, cand_order, cand_id, PEER_TOPK)
        e = jnp.exp(best_s - best_s[0:1])
        rs = slice(head * PEER_TOPK, (head + 1) * PEER_TOPK)
        ids_ref[rs, :] = ids
        gate_ref[rs, :] = e / e.sum(axis=0, keepdims=True)


def _peer_route(x2d, scale, shift, rows_per_mod, wq, subkeys, tb=256):
    T, D = x2d.shape
    nk = PEER_HEADS * PEER_TOPK
    sk = subkeys.reshape(PEER_HEADS * 2, PEER_NKEYS, PEER_SUB_DK).astype(jnp.bfloat16)
    mod = lambda i: (i * tb // rows_per_mod, 0, 0)
    h, ids_t, gate_t = pl.pallas_call(
        _peer_topk_kernel,
        grid=(T // tb,),
        in_specs=[pl.BlockSpec((tb, D), lambda i: (i, 0)),
                  pl.BlockSpec((1, 1, D), mod),
                  pl.BlockSpec((1, 1, D), mod),
                  pl.BlockSpec(wq.shape, lambda i: (0, 0)),
                  pl.BlockSpec(sk.shape, lambda i: (0, 0, 0))],
        out_specs=[pl.BlockSpec((tb, D), lambda i: (i, 0)),
                   pl.BlockSpec((nk, tb), lambda i: (0, i)),
                   pl.BlockSpec((nk, tb), lambda i: (0, i))],
        out_shape=[jax.ShapeDtypeStruct((T, D), jnp.bfloat16),
                   jax.ShapeDtypeStruct((nk, T), jnp.int32),
                   jax.ShapeDtypeStruct((nk, T), jnp.float32)],
        compiler_params=pltpu.CompilerParams(vmem_limit_bytes=48 * 1024 * 1024),
        name="peer_route",
    )(x2d, scale, shift, wq.astype(jnp.bfloat16), sk)
    return h, ids_t.T, gate_t.T


PEER_MASK_SLAB = PEER_NKEYS + 8
PEER_MASK_UNROLL = 8
PEER_KEYS_PER_STEP = 8


def _peer_expert_kernel(x_ref, ids_ref, gate_ref, u_ref, v_ref, o_ref, w_scr):
    f32, bf16 = jnp.float32, jnp.bfloat16
    j = pl.program_id(1)
    tm = x_ref.shape[0]
    nt = (((1,), (1,)), ((), ()))

    @pl.when(j == 0)
    def _():
        o_ref[...] = jnp.zeros_like(o_ref)
        key_iota = lax.broadcasted_iota(jnp.int32, (PEER_NKEYS, ids_ref.shape[1]), 0)

        def token_mask(t):
            ids = ids_ref[pl.ds(t, 1), :]
            g = gate_ref[pl.ds(t, 1), :]
            g_hi = g.astype(bf16).astype(f32)
            g_lo = g - g_hi
            first = jnp.where((ids >> 7) == key_iota, 1.0, 0.0).astype(bf16)
            second = (ids & (PEER_NKEYS - 1)) == key_iota
            w_t = (lax.dot_general(first, jnp.where(second, g_hi, 0.0).astype(bf16), nt, preferred_element_type=f32)
                   + lax.dot_general(first, jnp.where(second, g_lo, 0.0).astype(bf16), nt,
                                     preferred_element_type=f32))
            w_scr[pl.ds(pl.multiple_of(t * PEER_MASK_SLAB, 8), PEER_NKEYS), :] = w_t

        def token_group(i, carry):
            for s in range(PEER_MASK_UNROLL):
                token_mask(i * PEER_MASK_UNROLL + s)
            return carry

        lax.fori_loop(0, tm // PEER_MASK_UNROLL, token_group, 0)

    a = lax.dot_general(x_ref[...], u_ref[...], nt, preferred_element_type=f32)
    w = jnp.concatenate([w_scr[pl.ds(j * PEER_KEYS_PER_STEP + s, tm, stride=PEER_MASK_SLAB), :]
                         for s in range(PEER_KEYS_PER_STEP)], axis=1)
    gelu = 0.5 * a * (1.0 + lax.erf(a * (2.0 ** -0.5)))
    c = (w * gelu).astype(bf16)
    o_ref[...] += jnp.dot(c, v_ref[...], preferred_element_type=f32)


def _peer_expert(tok, ids, gate, u, v, tm=256):
    T, D = tok.shape
    te = PEER_KEYS_PER_STEP * PEER_NKEYS
    nk = ids.shape[1]
    assert PEER_NKEYS == 128 and T % tm == 0 and PEER_EXPERTS % te == 0
    return pl.pallas_call(
        _peer_expert_kernel,
        grid=(T // tm, PEER_EXPERTS // te),
        in_specs=[pl.BlockSpec((tm, D), lambda i, j: (i, 0)),
                  pl.BlockSpec((tm, nk), lambda i, j: (i, 0)),
                  pl.BlockSpec((tm, nk), lambda i, j: (i, 0)),
                  pl.BlockSpec((te, D), lambda i, j: (j, 0)),
                  pl.BlockSpec((te, D), lambda i, j: (j, 0))],
        out_specs=pl.BlockSpec((tm, D), lambda i, j: (i, 0)),
        out_shape=jax.ShapeDtypeStruct((T, D), jnp.float32),
        scratch_shapes=[pltpu.VMEM((tm * PEER_MASK_SLAB, PEER_NKEYS), jnp.float32)],
        compiler_params=pltpu.CompilerParams(dimension_semantics=("arbitrary", "arbitrary"),
                                             vmem_limit_bytes=48 * 1024 * 1024),
        name="peer_expert",
    )(tok, ids, gate, u, v)


def _peer(x2d, scale, shift, rows_per_mod, wq, subkeys, u, v):
    tok, ids, gate = _peer_route(x2d, scale, shift, rows_per_mod, wq, subkeys)
    return _peer_expert(tok, ids, gate, u, v)


def _in_proj_kernel(x_ref, sc_ref, sh_ref, w_ref, o_ref, a_scr):
    @pl.when(pl.program_id(1) == 0)
    def _():
        a_scr[...] = (x_ref[...] * sc_ref[0] + sh_ref[0]).astype(a_scr.dtype)

    o_ref[...] = jnp.dot(a_scr[...], w_ref[...], preferred_element_type=jnp.float32).astype(o_ref.dtype)


def _in_proj(x2d, scale, shift, rows_per_mod, w, col0, ncols, out_dtype, tm=1024, tn=512):
    M, K = x2d.shape
    tm = min(tm, M)
    assert col0 % tn == 0 and M % tm == 0
    mod = lambda i, j: (i * tm // rows_per_mod, 0, 0)
    return pl.pallas_call(
        _in_proj_kernel,
        grid=(M // tm, pl.cdiv(ncols, tn)),
        in_specs=[pl.BlockSpec((tm, K), lambda i, j: (i, 0)),
                  pl.BlockSpec((1, 1, K), mod),
                  pl.BlockSpec((1, 1, K), mod),
                  pl.BlockSpec((K, tn), lambda i, j: (0, col0 // tn + j))],
        out_specs=pl.BlockSpec((tm, tn), lambda i, j: (i, j)),
        out_shape=jax.ShapeDtypeStruct((M, ncols), out_dtype),
        scratch_shapes=[pltpu.VMEM((tm, K), jnp.bfloat16)],
        compiler_params=pltpu.CompilerParams(dimension_semantics=("arbitrary", "arbitrary"),
                                             vmem_limit_bytes=48 * 1024 * 1024),
        name="in_proj",
    )(x2d, scale, shift, w)


def _out_proj_ln_kernel(a_ref, b_ref, c_ref, d_ref, w_ref, x_ref, gt_ref, g_ref, bt_ref, o_ref):
    f32 = jnp.float32
    W = GROUP_WIDTH
    y = jnp.zeros(x_ref.shape, f32)
    for g, r in enumerate((a_ref, b_ref, c_ref, d_ref)):
        y = y + jnp.dot(r[...].astype(jnp.bfloat16), w_ref[g * W:(g + 1) * W, :], preferred_element_type=f32)
    z = DEEPNORM_ALPHA * x_ref[...] + gt_ref[0] * y
    mu = jnp.mean(z, axis=-1, keepdims=True)
    var = jnp.mean(jnp.square(z - mu), axis=-1, keepdims=True)
    o_ref[...] = (z - mu) * lax.rsqrt(var + 1e-5) * g_ref[...] + bt_ref[...]


def _out_proj_ln(parts, w, x2d, gate, rows_per_mod, ln_g, ln_b, tm=256):
    M, D = x2d.shape
    W = GROUP_WIDTH
    tm = min(tm, M)
    row = lambda i: (i, 0)
    return pl.pallas_call(
        _out_proj_ln_kernel,
        grid=(M // tm,),
        in_specs=[pl.BlockSpec((tm, W), row)] * 4
                 + [pl.BlockSpec(w.shape, lambda i: (0, 0)),
                    pl.BlockSpec((tm, D), row),
                    pl.BlockSpec((1, 1, D), lambda i: (i * tm // rows_per_mod, 0, 0)),
                    pl.BlockSpec((1, D), lambda i: (0, 0)),
                    pl.BlockSpec((1, D), lambda i: (0, 0))],
        out_specs=pl.BlockSpec((tm, D), row),
        out_shape=jax.ShapeDtypeStruct((M, D), jnp.float32),
        compiler_params=pltpu.CompilerParams(vmem_limit_bytes=48 * 1024 * 1024),
        name="out_proj_ln",
    )(*parts, w, x2d, gate, ln_g.reshape(1, D), ln_b.reshape(1, D))


def kernel(x, c, ctx, c_ctx, w_ada, b_ada, w_in, w_out, ln_g, ln_b, hg_lower_bounds, hg_norm_w,
           hy_conv_w, hy_conv_b, hy_w1, hy_b1, hy_w2, hy_b2, hy_w3, hy_freq, hy_dbias, na_rpb,
           ml_conv_w, ml_conv_b, ml_gate_b, peer_wq, peer_subkeys, peer_u, peer_v):
    f32, bf16 = jnp.float32, jnp.bfloat16
    B, L, D = x.shape
    n_ctx = ctx.shape[1]
    lb_soft = jax.nn.softmax(hg_lower_bounds.astype(f32), axis=0)
    lower_bounds = jnp.cumsum(lb_soft, axis=0) - lb_soft[0]
    cond_lat = jax.nn.silu(c)
    cond_ctx = jax.nn.silu(c_ctx)
    c0_na = HG_COLS + HY_COLS
    c0_ml = c0_na + NA_COLS
    x2 = x.reshape(B * L, D)
    dft_lat = _hyena_dft_matrices(L)
    xc2 = ctx.reshape(B * n_ctx, D)
    for l in range(DEPTH):
        need_ctx = l < DEPTH - 1
        m_lat = (cond_lat @ w_ada[l] + b_ada[l]).reshape(B, 1, 6, D)
        m_ctx = (cond_ctx @ w_ada[l] + b_ada[l]).reshape(1, 1, 6, D)
        ml = [m_lat[:, :, k] for k in range(6)]
        mc = [m_ctx[:, :, k] for k in range(6)]
        w_in_l = w_in[l].astype(bf16)
        w_out_l = w_out[l].astype(bf16)

        def in_proj(t, m, rows_per_mod):
            sc, sh = 1 + m[1], m[0]
            return (_in_proj(t, sc, sh, rows_per_mod, w_in_l, 0, c0_na, f32),
                    _in_proj(t, sc, sh, rows_per_mod, w_in_l, c0_na, NA_COLS, bf16),
                    _in_proj(t, sc, sh, rows_per_mod, w_in_l, c0_ml, ML_COLS, f32))

        ph_l, pn_l, pm_l = in_proj(x2, ml, L)
        ph_c, pn_c, pm_c = in_proj(xc2, mc, B * n_ctx)
        pb_l, pd_l = ph_l[:, HG_COLS:].reshape(B, L, -1), pm_l.reshape(B, L, -1)
        pb_c, pd_c = ph_c[:, HG_COLS:].reshape(B, n_ctx, -1), pm_c.reshape(B, n_ctx, -1)
        hy = (hy_conv_w[l], hy_conv_b[l], hy_w1[l], hy_b1[l], hy_w2[l], hy_b2[l], hy_w3[l], hy_freq[l], hy_dbias[l])
        a_l, a_c = _hgrn2_pallas(ph_l, ph_c, B, lower_bounds[l], hg_norm_w[l], need_ctx)
        b_l = _hyena_seq(pb_l, dft_lat, *hy)
        c_l, c_c = _natten_pallas(pn_l, pn_c, na_rpb[l], need_ctx, B)
        d_l, d_c = _mlstm_group(pd_l, pd_c, ml_conv_w[l], ml_conv_b[l], ml_gate_b[l], need_ctx)
        parts_l = (a_l, b_l.reshape(B * L, -1), c_l, d_l.reshape(B * L, -1))
        if need_ctx:
            b_c = _hyena_seq(pb_c, _hyena_dft_matrices(n_ctx), *hy)
            parts_c = (a_c, b_c.reshape(B * n_ctx, -1), c_c, d_c.reshape(B * n_ctx, -1))
            xc2 = _out_proj_ln(parts_c, w_out_l, xc2, mc[2], B * n_ctx, ln_g[l, 0], ln_b[l, 0])
        x2 = _out_proj_ln(parts_l, w_out_l, x2, ml[2], L, ln_g[l, 0], ln_b[l, 0])
        u_l, v_l = peer_u[l].astype(bf16), peer_v[l].astype(bf16)
        y_lat = _peer(x2, 1 + ml[4], ml[3], L, peer_wq[l], peer_subkeys[l], u_l, v_l)
        x2 = _layer_norm(DEEPNORM_ALPHA * x2.reshape(B, L, D) + ml[5] * y_lat.reshape(B, L, D),
                         ln_g[l, 1], ln_b[l, 1]).reshape(B * L, D)
        if need_ctx:
            y_ctx = _peer(xc2, 1 + mc[4], mc[3], B * n_ctx, peer_wq[l], peer_subkeys[l], u_l, v_l)
            xc2 = _layer_norm(DEEPNORM_ALPHA * xc2 + mc[5][0] * y_ctx, ln_g[l, 1], ln_b[l, 1])
    return x2.reshape(B, L, D)
```

```python
import functools
import math
import jax
import jax.numpy as jnp
from jax import lax
from jax.experimental import pallas as pl
from jax.experimental.pallas import tpu as pltpu

D_MODEL = 2048
BATCH = 4
SEQ = 4096
DEPTH = 2

CTX_LEN = 256
GRID_W = 64
GROUP_WIDTH = D_MODEL // 4
MIX_WIDTH = 4 * GROUP_WIDTH
CHUNK = 64
HG_DIM = 128
HG_HEADS = GROUP_WIDTH // HG_DIM
HG_F_MIN = 1e-30
HY_WIDTH = GROUP_WIDTH
HY_ORDER = 2
HY_SHORT = 3
HY_BANDS = 16
HY_EMB = 1 + 2 * HY_BANDS
HY_HIDDEN = 64
HY_FAST_DECAY = 0.3
HY_SLOW_DECAY = 1.5
HY_DECAY_TARGET = 1e-2
HY_MIN_DECAY = math.log(HY_DECAY_TARGET) / HY_SLOW_DECAY
HY_MAX_DECAY = math.log(HY_DECAY_TARGET) / HY_FAST_DECAY
NA_DIM = 64
NA_HEADS = GROUP_WIDTH // NA_DIM
NA_WIN_ROWS = 8
NA_WIN_COLS = 16
NA_QBLK = 16
NA_KBLK = 32
ML_DIM = 128
ML_HEADS = GROUP_WIDTH // ML_DIM
ML_SHORT = 3
ROPE_THETA = 10000.0
PEER_HEADS = 8
PEER_NKEYS = 128
PEER_EXPERTS = PEER_NKEYS * PEER_NKEYS
PEER_DK = 256
PEER_TOPK = 16
PEER_TOKEN_BLOCK = 64
HG_COLS = 5 * GROUP_WIDTH
HY_COLS = 3 * GROUP_WIDTH
NA_COLS = 3 * GROUP_WIDTH
ML_COLS = 4 * GROUP_WIDTH + 4 * ML_HEADS
IN_WIDTH = HG_COLS + HY_COLS + NA_COLS + ML_COLS
DEEPNORM_ALPHA = (2 * DEPTH) ** 0.25
DEEPNORM_BETA = (8 * DEPTH) ** -0.25
MASK_VALUE = -1e30


def _layer_norm(x, g, b, eps=1e-5):
    xf = x.astype(jnp.float32)
    mu = jnp.mean(xf, axis=-1, keepdims=True)
    var = jnp.mean(jnp.square(xf - mu), axis=-1, keepdims=True)
    y = (xf - mu) * lax.rsqrt(var + eps) * g.astype(jnp.float32) + b.astype(jnp.float32)
    return y.astype(x.dtype)


def _rms_norm(x, g, eps=1e-6):
    xf = x.astype(jnp.float32)
    return xf * lax.rsqrt(jnp.mean(jnp.square(xf), axis=-1, keepdims=True) + eps) * g.astype(jnp.float32)


def _centred_dwconv(x, w, b):
    k, L = w.shape[0], x.shape[1]
    xp = jnp.pad(x, ((0, 0), (k // 2, k // 2), (0, 0)))
    y = sum(xp[:, j:j + L] * w[j].astype(x.dtype) for j in range(k))
    return y + b.astype(x.dtype)


def _axial_rope(x):
    L, dh = x.shape[1], x.shape[-1]
    half, quarter = dh // 2, dh // 4
    t = jnp.arange(L)
    inv_freq = ROPE_THETA ** (-jnp.arange(quarter, dtype=jnp.float32) / quarter)

    def rot(xa, pos):
        ang = pos.astype(jnp.float32)[:, None] * inv_freq
        cos = jnp.cos(ang)[None, :, None, :]
        sin = jnp.sin(ang)[None, :, None, :]
        x1, x2 = xa[..., :quarter], xa[..., quarter:]
        return jnp.concatenate([x1 * cos - x2 * sin, x2 * cos + x1 * sin], axis=-1)

    return jnp.concatenate([rot(x[..., :half], t // GRID_W), rot(x[..., half:], t % GRID_W)], axis=-1)


def _chunks(x):
    b, h, L = x.shape[:3]
    return jnp.moveaxis(x.reshape(b, h, L // CHUNK, CHUNK, *x.shape[3:]), 2, 0)


def _unchunks(x):
    x = jnp.moveaxis(x, 0, 2)
    return x.reshape(x.shape[0], x.shape[1], -1, *x.shape[4:])


def _flip(t):
    return jnp.flip(t, axis=2)


HG_SUB = 16
HG_BLOCK = 256


def _hgrn2_kernel(*refs, reverse, readout):
    if readout:
        q_ref, z_ref, v_ref, lb_ref, s0_ref, g_ref, nw_ref, of_ref, o_ref, sfin_ref, st_scr = refs
    else:
        q_ref, z_ref, v_ref, lb_ref, s0_ref, o_ref, sfin_ref, st_scr = refs
    f32, bf16 = jnp.float32, jnp.bfloat16
    hi = lax.Precision.HIGHEST
    C, S = CHUNK, HG_SUB
    nsub = C // S
    n_chunks = q_ref.shape[0] // C
    step = pl.program_id(2)

    @pl.when(step == 0)
    def _():
        st_scr[...] = s0_ref[0, 0]

    nt = (((1,), (1,)), ((), ()))
    tn = (((0,), (0,)), ((), ()))
    r_i = lax.broadcasted_iota(jnp.int32, (C, C), 0)
    c_i = lax.broadcasted_iota(jnp.int32, (C, C), 1)
    tri = jnp.where((c_i >= r_i) if reverse else (c_i <= r_i), 1.0, 0.0).astype(f32)
    row_in_sub = lax.broadcasted_iota(jnp.int32, (S, HG_DIM), 0)
    lb = lb_ref[...]

    def chunk(ci, st):
        c = (n_chunks - 1 - ci) if reverse else ci
        rs = slice(c * C, (c + 1) * C)
        z, q, v = z_ref[rs, :], q_ref[rs, :], v_ref[rs, :]
        neg = jax.nn.sigmoid(-z)
        f = jax.nn.sigmoid(z) + lb * neg
        lf = jnp.log(jnp.maximum(f, HG_F_MIN))
        k = (1.0 - lb) * neg
        cum = jnp.dot(tri, lf, precision=hi, preferred_element_type=f32)
        total = cum[0:1] if reverse else cum[C - 1:C]
        o = lax.dot_general((q * jnp.exp(cum)).astype(bf16), st.astype(bf16), nt, preferred_element_type=f32)
        kd_end = k * jnp.exp(total - cum)
        st_new = jnp.exp(total) * st + lax.dot_general(v.astype(bf16), kd_end.astype(bf16), tn,
                                                       preferred_element_type=f32)
        parts = []
        for i in range(nsub):
            ts = slice(i * S, (i + 1) * S)
            q_i, cum_i = q[ts], cum[ts]
            acc = o[ts]
            ps = slice((i + 1) * S, C) if reverse else slice(0, i * S)
            if ps.stop > ps.start:
                ref = cum[(i + 1) * S:(i + 1) * S + 1] if reverse else cum[i * S - 1:i * S]
                a = lax.dot_general(q_i * jnp.exp(cum_i - ref), k[ps] * jnp.exp(ref - cum[ps]), nt,
                                    precision=hi, preferred_element_type=f32)
                acc = acc + jnp.dot(a.astype(bf16), v[ps].astype(bf16), preferred_element_type=f32)
            for s in range(S):
                r = i * S + s
                seen = (row_in_sub <= s) if reverse else (row_in_sub >= s)
                e = jnp.exp(jnp.where(seen, cum_i - cum[r:r + 1], MASK_VALUE))
                w = jnp.sum(q_i * k[r:r + 1] * e, axis=-1, keepdims=True)
                acc = acc + w * v[r:r + 1]
            parts.append(acc)
        o_c = jnp.concatenate(parts, axis=0)
        if readout:
            x = o_c + of_ref[rs, :]
            y = x * lax.rsqrt(jnp.mean(jnp.square(x), axis=-1, keepdims=True) + 1e-6) * nw_ref[...]
            o_ref[rs, :] = (y * jax.nn.silu(g_ref[rs, :])).astype(o_ref.dtype)
        else:
            o_ref[rs, :] = o_c.astype(o_ref.dtype)
        return st_new

    st = st_scr[...]
    for ci in range(n_chunks):
        st = chunk(ci, st)
    st_scr[...] = st

    @pl.when(step == pl.num_programs(2) - 1)
    def _():
        sfin_ref[0, 0] = st


def _hgrn2_scan_call(p, batch, lb, s0, reverse, z_group, readout_args=None):
    L = p.shape[0] // batch
    tb = min(HG_BLOCK, L)
    n_t = L // tb
    assert L % tb == 0 and tb % CHUNK == 0
    H = HG_HEADS

    def rows(b, h, t):
        return b * n_t + ((n_t - 1 - t) if reverse else t)

    def group(g):
        return pl.BlockSpec((tb, HG_DIM), lambda b, h, t: (rows(b, h, t), g * H + h))

    head_cols = pl.BlockSpec((1, HG_DIM), lambda b, h, t: (0, h))
    state = pl.BlockSpec((1, 1, HG_DIM, HG_DIM), lambda b, h, t: (b, h, 0, 0))
    out_rows = pl.BlockSpec((tb, HG_DIM), lambda b, h, t: (rows(b, h, t), h))
    in_specs = [group(0), group(z_group), group(3), head_cols, state]
    args = [p, p, p, lb, s0]
    readout = readout_args is not None
    if readout:
        norm_w, o_other = readout_args
        in_specs += [group(4), pl.BlockSpec((1, HG_DIM), lambda b, h, t: (0, 0)), out_rows]
        args += [p, norm_w.reshape(1, HG_DIM), o_other]
    return pl.pallas_call(
        functools.partial(_hgrn2_kernel, reverse=reverse, readout=readout),
        grid=(batch, H, n_t),
        in_specs=in_specs,
        out_specs=[out_rows, state],
        out_shape=[jax.ShapeDtypeStruct((batch * L, GROUP_WIDTH), jnp.bfloat16 if readout else jnp.float32),
                   jax.ShapeDtypeStruct((batch, H, HG_DIM, HG_DIM), jnp.float32)],
        scratch_shapes=[pltpu.VMEM((HG_DIM, HG_DIM), jnp.float32)],
        compiler_params=pltpu.CompilerParams(dimension_semantics=("arbitrary", "arbitrary", "arbitrary")),
        name="hgrn2_bwd" if reverse else "hgrn2_fwd",
    )(*args)


def _hgrn2_pallas(p_lat, p_ctx, batch, lb, norm_w, need_ctx):
    zero = jnp.zeros((batch, HG_HEADS, HG_DIM, HG_DIM), jnp.float32)
    lb_f, lb_b = lb[0:1], lb[1:2]
    oc_f, s_f = _hgrn2_scan_call(p_ctx, batch, lb_f, zero, False, 1)
    y_ctx, s_b = _hgrn2_scan_call(p_ctx, batch, lb_b, zero, True, 2, (norm_w, oc_f) if need_ctx else None)
    ol_f, _ = _hgrn2_scan_call(p_lat, batch, lb_f, s_f, False, 1)
    y_lat, _ = _hgrn2_scan_call(p_lat, batch, lb_b, s_b, True, 2, (norm_w, ol_f))
    return y_lat, (y_ctx if need_ctx else None)


def _hyena_filters(L, w1, b1, w2, b2, w3, freq):
    f32 = jnp.float32
    pos = jnp.arange(L, dtype=f32)
    t = pos / (L - 1)
    bands = jnp.linspace(1e-4, HY_BANDS - 1, HY_BANDS, dtype=f32)
    ang = (2.0 * math.pi / L) * pos[:, None] * bands[None, :]
    z = jnp.concatenate([t[:, None], jnp.cos(ang), jnp.sin(ang)], axis=-1)
    freq = freq.astype(f32)
    hid = jnp.sin(freq[0] * (z @ w1.astype(f32) + b1.astype(f32)))
    hid = jnp.sin(freq[1] * (hid @ w2.astype(f32) + b2.astype(f32)))
    h = (hid @ w3.astype(f32)).reshape(L, 2, HY_ORDER, HY_WIDTH)
    deltas = jnp.abs(jnp.linspace(HY_MIN_DECAY, HY_MAX_DECAY, HY_WIDTH, dtype=f32))
    h = h * jnp.exp(-t[:, None] * deltas)[:, None, None, :]
    fwd, bwd = h[:, 0], h[:, 1]
    bwd = bwd.at[0].set(0.0)
    norm = jnp.sum(jnp.abs(fwd), axis=0, keepdims=True) + jnp.sum(jnp.abs(bwd), axis=0, keepdims=True)
    return fwd / norm, bwd / norm


HY_FREQ_TILE = 384
HY_TIME_TILE = 256
HY_ANGLE_SPLIT = 64


def _dft_angle_tables(rows, cols, period):
    f32 = jnp.float32
    r = jnp.arange(rows, dtype=jnp.int32)[:, None]
    ch = jnp.arange(cols // HY_ANGLE_SPLIT, dtype=jnp.int32)[None, :] * HY_ANGLE_SPLIT
    cl = jnp.arange(HY_ANGLE_SPLIT, dtype=jnp.int32)[None, :]
    ah = ((r * ch) % period).astype(f32) * (2.0 * math.pi / period)
    al = ((r * cl) % period).astype(f32) * (2.0 * math.pi / period)
    ca, sa, cb, sb = jnp.cos(ah)[:, :, None], jnp.sin(ah)[:, :, None], jnp.cos(al)[:, None, :], jnp.sin(al)[:, None, :]
    return (ca * cb - sa * sb).reshape(rows, cols), (sa * cb + ca * sb).reshape(rows, cols)


def _hyena_dft_matrices(L):
    N = 2 * L
    mp = -(-(L + 1) // HY_FREQ_TILE) * HY_FREQ_TILE
    bf16 = jnp.bfloat16
    k = jnp.arange(mp)
    live = (k <= L).astype(jnp.float32)
    cf, sf = _dft_angle_tables(mp, L, N)
    ci, si = _dft_angle_tables(L, mp, N)
    w = jnp.where((k == 0) | (k == L), 1.0, 2.0) * live / N
    return ((cf * live[:, None]).astype(bf16), (sf * live[:, None]).astype(bf16),
            (ci * w[None, :]).astype(bf16), (-si * w[None, :]).astype(bf16))


def _dft_pair_kernel(c_ref, s_ref, a_ref, b_ref, oc_ref, os_ref):
    oc_ref[...] = jnp.dot(c_ref[...], a_ref[...], preferred_element_type=jnp.float32)
    os_ref[...] = jnp.dot(s_ref[...], b_ref[...], preferred_element_type=jnp.float32)


def _hyena_filter_spectrum(cf, sf, fwd, bwd):
    mp, L = cf.shape
    C = fwd.shape[1]
    bf16 = jnp.bfloat16
    col = lambda i: (0, 0)
    re, s = pl.pallas_call(
        _dft_pair_kernel,
        grid=(mp // HY_FREQ_TILE,),
        in_specs=[pl.BlockSpec((HY_FREQ_TILE, L), lambda i: (i, 0))] * 2 + [pl.BlockSpec((L, C), col)] * 2,
        out_specs=[pl.BlockSpec((HY_FREQ_TILE, C), lambda i: (i, 0))] * 2,
        out_shape=[jax.ShapeDtypeStruct((mp, C), jnp.float32)] * 2,
        compiler_params=pltpu.CompilerParams(vmem_limit_bytes=48 * 1024 * 1024),
        name="hyena_filter_spectrum",
    )(cf, sf, (fwd + bwd).astype(bf16), (fwd - bwd).astype(bf16))
    return re, -s


def _hyena_fwd_kernel(c_ref, s_ref, z_ref, kre_ref, kim_ref, pre_ref, pim_ref):
    f32 = jnp.float32
    xc = jnp.dot(c_ref[...], z_ref[0], preferred_element_type=f32)
    xs = jnp.dot(s_ref[...], z_ref[0], preferred_element_type=f32)
    pre_ref[0] = (xc * kre_ref[...] + xs * kim_ref[...]).astype(pre_ref.dtype)
    pim_ref[0] = (xc * kim_ref[...] - xs * kre_ref[...]).astype(pim_ref.dtype)


def _hyena_inv_kernel(ci_ref, si_ref, pre_ref, pim_ref, z_ref, d_ref, m_ref, o_ref, ob_ref):
    f32 = jnp.float32
    y = (jnp.dot(ci_ref[...], pre_ref[0], preferred_element_type=f32)
         + jnp.dot(si_ref[...], pim_ref[0], preferred_element_type=f32))
    out = m_ref[0] * (y + z_ref[0] * d_ref[...])
    o_ref[0] = out
    ob_ref[0] = out.astype(ob_ref.dtype)


def _hyena_long_conv(mats, z, z_b, kre, kim, d, mult):
    cf, sf, ci, si = mats
    B, L, C = z.shape
    mp = cf.shape[0]
    tt = min(HY_TIME_TILE, L)
    bf16 = jnp.bfloat16
    pre, pim = pl.pallas_call(
        _hyena_fwd_kernel,
        grid=(mp // HY_FREQ_TILE, B),
        in_specs=[pl.BlockSpec((HY_FREQ_TILE, L), lambda i, b: (i, 0))] * 2
                 + [pl.BlockSpec((1, L, C), lambda i, b: (b, 0, 0))]
                 + [pl.BlockSpec((HY_FREQ_TILE, C), lambda i, b: (i, 0))] * 2,
        out_specs=[pl.BlockSpec((1, HY_FREQ_TILE, C), lambda i, b: (b, i, 0))] * 2,
        out_shape=[jax.ShapeDtypeStruct((B, mp, C), bf16)] * 2,
        compiler_params=pltpu.CompilerParams(vmem_limit_bytes=48 * 1024 * 1024),
        name="hyena_fwd",
    )(cf, sf, z_b, kre, kim)
    row = lambda b, t: (b, t, 0)
    return pl.pallas_call(
        _hyena_inv_kernel,
        grid=(B, L // tt),
        in_specs=[pl.BlockSpec((tt, mp), lambda b, t: (t, 0))] * 2
                 + [pl.BlockSpec((1, mp, C), lambda b, t: (b, 0, 0))] * 2
                 + [pl.BlockSpec((1, tt, C), row), pl.BlockSpec((1, C), lambda b, t: (0, 0)),
                    pl.BlockSpec((1, tt, C), row)],
        out_specs=[pl.BlockSpec((1, tt, C), row)] * 2,
        out_shape=[jax.ShapeDtypeStruct((B, L, C), jnp.float32), jax.ShapeDtypeStruct((B, L, C), bf16)],
        compiler_params=pltpu.CompilerParams(vmem_limit_bytes=48 * 1024 * 1024),
        name="hyena_inv",
    )(ci, si, pre, pim, z, d.reshape(1, C).astype(jnp.float32), mult)


def _hyena_seq(p, mats, conv_w, conv_b, w1, b1, w2, b2, w3, freq, dbias):
    u = _centred_dwconv(p, conv_w, conv_b).astype(jnp.float32)
    v, x1, x2 = jnp.split(u, 3, axis=-1)
    L = p.shape[1]
    fwd, bwd = _hyena_filters(L, w1, b1, w2, b2, w3, freq)
    kre, kim = _hyena_filter_spectrum(mats[0], mats[1], fwd.reshape(L, -1), bwd.reshape(L, -1))
    W = HY_WIDTH
    z, z_b = _hyena_long_conv(mats, v, v.astype(jnp.bfloat16), kre[:, :W], kim[:, :W], dbias[0], x1)
    return _hyena_long_conv(mats, z, z_b, kre[:, W:], kim[:, W:], dbias[1], x2)[1]


def _natten_group(p_lat, p_ctx, rpb, need_ctx):
    f32 = jnp.float32
    B, L, _ = p_lat.shape

    def heads(p):
        return [t.reshape(t.shape[0], t.shape[1], NA_HEADS, NA_DIM) for t in jnp.split(p, 3, axis=-1)]

    q, k, v = heads(p_lat)
    qc, kc, vc = heads(p_ctx)
    scale = NA_DIM ** -0.5
    rows = L // GRID_W
    kh = min(NA_WIN_ROWS, rows)
    ncb = GRID_W // NA_QBLK
    nk = kh * NA_KBLK
    r = jnp.arange(rows)
    key_rows = jnp.clip(r - kh // 2, 0, rows - kh)[:, None] + jnp.arange(kh)
    blk = jnp.arange(ncb)
    qcol = blk[:, None] * NA_QBLK + jnp.arange(NA_QBLK)
    key_cols = (jnp.clip(blk * NA_QBLK - NA_WIN_COLS // 2, 0, GRID_W - NA_KBLK)[:, None]
                + jnp.arange(NA_KBLK))
    idx = (key_rows[:, None, :, None] * GRID_W + key_cols[None, :, None, :]).reshape(rows, ncb, nk)
    col_start = jnp.clip(qcol - NA_WIN_COLS // 2, 0, GRID_W - NA_WIN_COLS)[:, :, None, None]
    kcol = key_cols[:, None, None, :]
    valid = jnp.broadcast_to((kcol >= col_start) & (kcol < col_start + NA_WIN_COLS),
                             (ncb, NA_QBLK, kh, NA_KBLK)).reshape(ncb, NA_QBLK, nk)
    dc = jnp.clip(kcol - qcol[:, :, None, None], 1 - NA_WIN_COLS, NA_WIN_COLS - 1)
    dr = key_rows - r[:, None]
    bias = rpb.astype(f32)[:, dr[:, None, None, :, None] + NA_WIN_ROWS - 1, dc[None] + NA_WIN_COLS - 1]
    bias = jnp.moveaxis(bias.reshape(NA_HEADS, rows, ncb, NA_QBLK, nk), 1, 0)
    q_rows = jnp.moveaxis((q * scale).reshape(B, rows, ncb, NA_QBLK, NA_HEADS, NA_DIM), 1, 0)

    def row_block(inp):
        q_r, idx_r, bias_r = inp
        k_r, v_r = k[:, idx_r], v[:, idx_r]
        s_win = jnp.einsum('bjqhd,bjkhd->bhjqk', q_r, k_r).astype(f32)
        s_win = jnp.where(valid, s_win + bias_r, MASK_VALUE)
        s_ctx = jnp.einsum('bjqhd,bchd->bhjqc', q_r, kc).astype(f32)
        prob = jax.nn.softmax(jnp.concatenate([s_win, s_ctx], axis=-1), axis=-1).astype(v.dtype)
        return (jnp.einsum('bhjqk,bjkhd->bjqhd', prob[..., :nk], v_r)
                + jnp.einsum('bhjqc,bchd->bjqhd', prob[..., nk:], vc))

    out = lax.map(row_block, (q_rows, idx, bias))
    y_lat = jnp.moveaxis(out, 0, 1).reshape(B, L, GROUP_WIDTH)
    y_ctx = None
    if need_ctx:
        s = jnp.einsum('bqhd,bkhd->bhqk', qc * scale, kc).astype(f32)
        prob = jax.nn.softmax(s, axis=-1).astype(vc.dtype)
        y_ctx = jnp.einsum('bhqk,bkhd->bqhd', prob, vc).reshape(B, -1, GROUP_WIDTH)
    return y_lat, y_ctx


def _na_pair_attention(q_pair, key_sets, lane):
    f32 = jnp.float32
    nt = (((1,), (1,)), ((), ()))
    outs = []
    for half in range(2):
        own = (lane >= NA_DIM) if half else (lane < NA_DIM)
        q = jnp.where(own, q_pair, jnp.zeros_like(q_pair))
        scores = []
        for k_pair, _, add, valid in key_sets:
            s = lax.dot_general(q, k_pair, nt, preferred_element_type=f32) * (NA_DIM ** -0.5)
            if add is not None:
                s = jnp.where(valid, s + add[half], MASK_VALUE)
            scores.append(s)
        m = scores[0].max(axis=-1, keepdims=True)
        for s in scores[1:]:
            m = jnp.maximum(m, s.max(axis=-1, keepdims=True))
        den = jnp.zeros_like(m)
        acc = jnp.zeros(q_pair.shape, f32)
        for s, (_, v_pair, _, _) in zip(scores, key_sets):
            p = jnp.exp(s - m)
            den = den + p.sum(axis=-1, keepdims=True)
            acc = acc + jnp.dot(p.astype(jnp.bfloat16), v_pair, preferred_element_type=f32)
        outs.append(acc / den)
    return jnp.where(lane < NA_DIM, outs[0], outs[1])


def _natten_lat_kernel(q_ref, k_ref, v_ref, kc_ref, vc_ref, bias_ref, o_ref, *, rows):
    kh = NA_WIN_ROWS
    r = pl.program_id(1)
    start = pl.multiple_of(jnp.clip(r - kh // 2, 0, rows - kh) * GRID_W, GRID_W)
    nk = kh * GRID_W
    qcol = lax.broadcasted_iota(jnp.int32, (GRID_W, nk), 0)
    kcol = lax.broadcasted_iota(jnp.int32, (GRID_W, nk), 1) % GRID_W
    col_start = jnp.clip(qcol - NA_WIN_COLS // 2, 0, GRID_W - NA_WIN_COLS)
    valid = (kcol >= col_start) & (kcol < col_start + NA_WIN_COLS)
    lane = lax.broadcasted_iota(jnp.int32, (GRID_W, 2 * NA_DIM), 1)
    for hp in range(NA_HEADS // 2):
        cs = slice(hp * 2 * NA_DIM, (hp + 1) * 2 * NA_DIM)
        win = (k_ref[pl.ds(start, nk), cs], v_ref[pl.ds(start, nk), cs],
               (bias_ref[0, 2 * hp], bias_ref[0, 2 * hp + 1]), valid)
        ctx = (kc_ref[:, cs], vc_ref[:, cs], None, None)
        o_ref[:, cs] = _na_pair_attention(q_ref[:, cs], [win, ctx], lane).astype(o_ref.dtype)


def _natten_ctx_kernel(q_ref, k_ref, v_ref, o_ref):
    lane = lax.broadcasted_iota(jnp.int32, (q_ref.shape[0], 2 * NA_DIM), 1)
    for hp in range(NA_HEADS // 2):
        cs = slice(hp * 2 * NA_DIM, (hp + 1) * 2 * NA_DIM)
        o_ref[:, cs] = _na_pair_attention(q_ref[:, cs], [(k_ref[:, cs], v_ref[:, cs], None, None)],
                                          lane).astype(o_ref.dtype)


def _natten_bias_table(rpb, rows):
    kh = NA_WIN_ROWS
    dc = jnp.clip(jnp.arange(GRID_W)[None, :] - jnp.arange(GRID_W)[:, None], 1 - NA_WIN_COLS, NA_WIN_COLS - 1)
    t = rpb.astype(jnp.float32)[:, :, dc + NA_WIN_COLS - 1]
    per_off = [t[:, d:d + kh].transpose(0, 2, 1, 3).reshape(NA_HEADS, GRID_W, kh * GRID_W) for d in range(kh)]
    return jnp.stack(per_off, axis=0)


def _natten_pallas(p_lat, p_ctx, rpb, need_ctx, batch):
    W = GROUP_WIDTH
    L = p_lat.shape[0] // batch
    n_ctx = p_ctx.shape[0] // batch
    rows = L // GRID_W
    kh = NA_WIN_ROWS
    assert rows >= kh and L % GRID_W == 0
    bias = _natten_bias_table(rpb, rows)

    def first_key_row_offset(b, r):
        return (jnp.clip(r - kh // 2, 0, rows - kh) - r + kh - 1, 0, 0, 0)

    y_lat = pl.pallas_call(
        functools.partial(_natten_lat_kernel, rows=rows),
        grid=(batch, rows),
        in_specs=[pl.BlockSpec((GRID_W, W), lambda b, r: (b * rows + r, 0)),
                  pl.BlockSpec((L, W), lambda b, r: (b, 1)),
                  pl.BlockSpec((L, W), lambda b, r: (b, 2)),
                  pl.BlockSpec((n_ctx, W), lambda b, r: (b, 1)),
                  pl.BlockSpec((n_ctx, W), lambda b, r: (b, 2)),
                  pl.BlockSpec((1, NA_HEADS, GRID_W, kh * GRID_W), first_key_row_offset)],
        out_specs=pl.BlockSpec((GRID_W, W), lambda b, r: (b * rows + r, 0)),
        out_shape=jax.ShapeDtypeStruct((batch * L, W), jnp.bfloat16),
        compiler_params=pltpu.CompilerParams(vmem_limit_bytes=40 * 1024 * 1024),
        name="natten_lat",
    )(p_lat, p_lat, p_lat, p_ctx, p_ctx, bias)
    y_ctx = None
    if need_ctx:
        y_ctx = pl.pallas_call(
            _natten_ctx_kernel,
            grid=(batch,),
            in_specs=[pl.BlockSpec((n_ctx, W), lambda b: (b, 0)),
                      pl.BlockSpec((n_ctx, W), lambda b: (b, 1)),
                      pl.BlockSpec((n_ctx, W), lambda b: (b, 2))],
            out_specs=pl.BlockSpec((n_ctx, W), lambda b: (b, 0)),
            out_shape=jax.ShapeDtypeStruct((batch * n_ctx, W), jnp.bfloat16),
            name="natten_ctx",
        )(p_ctx, p_ctx, p_ctx)
    return y_lat, y_ctx


ML_BLOCK = 256
ML_GATES = 4 * ML_HEADS


def _mlstm_kernel(*refs, reverse, readout):
    if readout:
        (q_ref, k_ref, v_ref, gc_ref, gr_ref, c0_ref, n0_ref, m0_ref, og_ref, hf_ref,
         o_ref, cfin_ref, nfin_ref, mfin_ref, c_scr, n_scr, m_scr) = refs
    else:
        (q_ref, k_ref, v_ref, gc_ref, gr_ref, c0_ref, n0_ref, m0_ref,
         o_ref, cfin_ref, nfin_ref, mfin_ref, c_scr, n_scr, m_scr) = refs
    f32, bf16 = jnp.float32, jnp.bfloat16
    hi = lax.Precision.HIGHEST
    C = CHUNK
    n_chunks = q_ref.shape[0] // C
    head = pl.program_id(1)
    step = pl.program_id(2)

    @pl.when(step == 0)
    def _():
        c_scr[...] = c0_ref[0, 0]
        n_scr[...] = n0_ref[0, 0]
        m_scr[...] = m0_ref[0, 0]

    nt = (((1,), (1,)), ((), ()))
    tn = (((0,), (0,)), ((), ()))
    r_i = lax.broadcasted_iota(jnp.int32, (C, C), 0)
    c_i = lax.broadcasted_iota(jnp.int32, (C, C), 1)
    seen = (c_i >= r_i) if reverse else (c_i <= r_i)
    tri = jnp.where(seen, 1.0, 0.0).astype(f32)
    tri_t = jnp.where((r_i >= c_i) if reverse else (r_i <= c_i), 1.0, 0.0).astype(f32)
    ig_idx = (2 if reverse else 0) * ML_HEADS + head
    fg_idx = ig_idx + ML_HEADS
    lane = lax.broadcasted_iota(jnp.int32, (C, ML_GATES), 1)
    sub = lax.broadcasted_iota(jnp.int32, (ML_GATES, C), 0)

    def chunk(ci, state):
        ckv, n, m = state
        c = (n_chunks - 1 - ci) if reverse else ci
        rs = slice(c * C, (c + 1) * C)
        q, k, v = q_ref[rs, :], k_ref[rs, :], v_ref[rs, :]
        gates = gc_ref[rs, :]
        lf_cols = jax.nn.log_sigmoid(gates)
        g_cols = jnp.dot(tri, lf_cols, precision=hi, preferred_element_type=f32)
        g_c = jnp.sum(jnp.where(lane == fg_idx, g_cols, 0.0), axis=1, keepdims=True)
        ig_c = jnp.sum(jnp.where(lane == ig_idx, gates, 0.0), axis=1, keepdims=True)
        gates_r = gr_ref[:, rs]
        ig_r = jnp.sum(jnp.where(sub == ig_idx, gates_r, 0.0), axis=0, keepdims=True)
        lf_r = jnp.sum(jnp.where(sub == fg_idx, jax.nn.log_sigmoid(gates_r), 0.0), axis=0, keepdims=True)
        g_r = jnp.dot(lf_r, tri_t, precision=hi, preferred_element_type=f32)
        m1 = m[:, 0:1]
        logd = jnp.where(seen, g_c - g_r + ig_r, MASK_VALUE)
        log_inter = g_c + m1
        m_out = jnp.maximum(log_inter, jnp.max(logd, axis=1, keepdims=True))
        dmat = jnp.exp(logd - m_out)
        w_inter = jnp.exp(log_inter - m_out)
        qb = q.astype(bf16)
        sc = lax.dot_general(qb, k.astype(bf16), nt, preferred_element_type=f32) * dmat
        num = (jnp.dot(sc.astype(bf16), v.astype(bf16), preferred_element_type=f32)
               + w_inter * jnp.dot(qb, ckv.astype(bf16), preferred_element_type=f32))
        den = jnp.sum(sc, axis=1, keepdims=True) + w_inter * jnp.sum(q * n, axis=1, keepdims=True)
        h = num / jnp.maximum(jnp.abs(den), jnp.exp(-m_out))
        g_end = g_c[0:1] if reverse else g_c[C - 1:C]
        a = g_end - g_c + ig_c
        m_new = jnp.maximum(g_end + m1, jnp.max(a, axis=0, keepdims=True))
        carry_w = jnp.exp(g_end + m1 - m_new)
        wk = jnp.exp(a - m_new) * k
        ckv_new = carry_w * ckv + lax.dot_general(wk.astype(bf16), v.astype(bf16), tn, preferred_element_type=f32)
        n_new = carry_w * n + jnp.sum(wk, axis=0, keepdims=True)
        if readout:
            o_ref[rs, :] = (jax.nn.sigmoid(og_ref[rs, :]) * (h + hf_ref[rs, :])).astype(o_ref.dtype)
        else:
            o_ref[rs, :] = h.astype(o_ref.dtype)
        return ckv_new, n_new, jnp.broadcast_to(m_new, m.shape)

    state = (c_scr[...], n_scr[...], m_scr[...])
    for ci in range(n_chunks):
        state = chunk(ci, state)
    c_scr[...], n_scr[...], m_scr[...] = state

    @pl.when(step == pl.num_programs(2) - 1)
    def _():
        cfin_ref[0, 0], nfin_ref[0, 0], mfin_ref[0, 0] = state


def _mlstm_scan_call(q, k, v_src, v_col0, gates, gates_t, batch, state0, reverse, readout_args=None):
    L = q.shape[0] // batch
    tb = min(ML_BLOCK, L)
    n_t = L // tb
    assert L % tb == 0 and tb % CHUNK == 0 and tb % 128 == 0
    H = ML_HEADS

    def rows(b, h, t):
        return b * n_t + ((n_t - 1 - t) if reverse else t)

    head_block = pl.BlockSpec((tb, ML_DIM), lambda b, h, t: (rows(b, h, t), h))
    mat_state = pl.BlockSpec((1, 1, ML_DIM, ML_DIM), lambda b, h, t: (b, h, 0, 0))
    vec_state = pl.BlockSpec((1, 1, 1, ML_DIM), lambda b, h, t: (b, h, 0, 0))
    in_specs = [head_block, head_block,
                pl.BlockSpec((tb, ML_DIM), lambda b, h, t: (rows(b, h, t), v_col0 + h)),
                pl.BlockSpec((tb, ML_GATES), lambda b, h, t: (rows(b, h, t), 0)),
                pl.BlockSpec((ML_GATES, tb), lambda b, h, t: (0, rows(b, h, t))),
                mat_state, vec_state, vec_state]
    args = [q, k, v_src, gates, gates_t, *state0]
    readout = readout_args is not None
    if readout:
        og_src, og_col0, h_other = readout_args
        in_specs += [pl.BlockSpec((tb, ML_DIM), lambda b, h, t: (rows(b, h, t), og_col0 + h)), head_block]
        args += [og_src, h_other]
    outs = pl.pallas_call(
        functools.partial(_mlstm_kernel, reverse=reverse, readout=readout),
        grid=(batch, H, n_t),
        in_specs=in_specs,
        out_specs=[head_block, mat_state, vec_state, vec_state],
        out_shape=[jax.ShapeDtypeStruct((batch * L, GROUP_WIDTH), jnp.bfloat16 if readout else jnp.float32),
                   jax.ShapeDtypeStruct((batch, H, ML_DIM, ML_DIM), jnp.float32),
                   jax.ShapeDtypeStruct((batch, H, 1, ML_DIM), jnp.float32),
                   jax.ShapeDtypeStruct((batch, H, 1, ML_DIM), jnp.float32)],
        scratch_shapes=[pltpu.VMEM((ML_DIM, ML_DIM), jnp.float32), pltpu.VMEM((1, ML_DIM), jnp.float32),
                        pltpu.VMEM((1, ML_DIM), jnp.float32)],
        compiler_params=pltpu.CompilerParams(dimension_semantics=("arbitrary", "arbitrary", "arbitrary")),
        name="mlstm_bwd" if reverse else "mlstm_fwd",
    )(*args)
    return outs[0], tuple(outs[1:])


def _mlstm_pallas(pm_lat, pm_ctx, batch, conv_w, conv_b, gate_b, need_ctx):
    W = GROUP_WIDTH
    f32 = jnp.float32

    def prep(pm, rope):
        L = pm.shape[0] // batch
        qk = jax.nn.silu(_centred_dwconv(pm[:, :2 * W].reshape(batch, L, 2 * W), conv_w, conv_b)).astype(f32)
        q = qk[..., :W].reshape(batch, L, ML_HEADS, ML_DIM)
        k = qk[..., W:].reshape(batch, L, ML_HEADS, ML_DIM)
        if rope:
            q, k = _axial_rope(q), _axial_rope(k)
        gates = pm[:, 4 * W:] + gate_b.astype(f32)
        return q.reshape(batch * L, W), (k * ML_DIM ** -0.5).reshape(batch * L, W), gates, gates.T

    zero = (jnp.zeros((batch, ML_HEADS, ML_DIM, ML_DIM), f32), jnp.zeros((batch, ML_HEADS, 1, ML_DIM), f32),
            jnp.zeros((batch, ML_HEADS, 1, ML_DIM), f32))
    v0, o0 = 2 * W // ML_DIM, 3 * W // ML_DIM
    qc, kc, gc, gct = prep(pm_ctx, False)
    hc_f, st_f = _mlstm_scan_call(qc, kc, pm_ctx, v0, gc, gct, batch, zero, False)
    y_ctx, st_b = _mlstm_scan_call(qc, kc, pm_ctx, v0, gc, gct, batch, zero, True,
                                   (pm_ctx, o0, hc_f) if need_ctx else None)
    ql, kl, gl, glt = prep(pm_lat, True)
    hl_f, _ = _mlstm_scan_call(ql, kl, pm_lat, v0, gl, glt, batch, st_f, False)
    y_lat, _ = _mlstm_scan_call(ql, kl, pm_lat, v0, gl, glt, batch, st_b, True, (pm_lat, o0, hl_f))
    return y_lat, (y_ctx if need_ctx else None)


PEER_SUB_DK = PEER_DK // 2
PEER_CAND_ROWS = 16 + 8 + 6 * 8 + 8


def _topk_rows(s, order, payload, k):
    sentinel = jnp.int32(2 ** 30)
    vals, picked = [], []
    for _ in range(k):
        m = s.max(axis=0, keepdims=True)
        first = jnp.min(jnp.where(s == m, order, sentinel), axis=0, keepdims=True)
        hit = order == first
        vals.append(m)
        picked.append(first if payload is None else jnp.max(jnp.where(hit, payload, -1), axis=0, keepdims=True))
        s = jnp.where(hit, -jnp.inf, s)
    return jnp.concatenate(vals, axis=0), jnp.concatenate(picked, axis=0)


def _peer_candidates(a0, a1, combine):
    pieces = [combine(a0[0:1], a1), combine(a0[1:2], a1[0:8])]
    pieces += [combine(a0[i:i + 1], a1[0:8]) for i in range(2, 8)]
    pieces.append(combine(a0[8:16], a1[0:1]))
    return jnp.concatenate(pieces, axis=0)


def _peer_topk_kernel(x_ref, sc_ref, sh_ref, wq_ref, sk_ref, h_ref, ids_ref, gate_ref):
    f32 = jnp.float32
    tb = x_ref.shape[0]
    h = x_ref[...] * sc_ref[0] + sh_ref[0]
    h_ref[...] = h.astype(h_ref.dtype)
    q = jnp.dot(h.astype(jnp.bfloat16), wq_ref[...], preferred_element_type=f32).astype(jnp.bfloat16)
    key_order = lax.broadcasted_iota(jnp.int32, (PEER_NKEYS, tb), 0)
    rho = lax.broadcasted_iota(jnp.int32, (PEER_CAND_ROWS, tb), 0)
    ci = jnp.where(rho < 16, 0, jnp.where(rho < 24, 1, jnp.where(rho < 72, 2 + ((rho - 24) >> 3), rho - 64)))
    cj = jnp.where(rho < 16, rho, jnp.where(rho < 24, rho - 16, jnp.where(rho < 72, (rho - 24) & 7, 0)))
    cand_order = ci * PEER_TOPK + cj
    cand_valid = (ci + 1) * (cj + 1) <= PEER_TOPK
    nt = (((1,), (1,)), ((), ()))
    for head in range(PEER_HEADS):
        sv, si = [], []
        for half in range(2):
            c0 = (head * 2 + half) * PEER_SUB_DK
            s_t = lax.dot_general(sk_ref[head * 2 + half], q[:, c0:c0 + PEER_SUB_DK], nt,
                                  preferred_element_type=f32)
            v, i = _topk_rows(s_t, key_order, None, PEER_TOPK)
            sv.append(v)
            si.append(i)
        cand_s = _peer_candidates(sv[0], sv[1], lambda a, b: a + b)
        cand_s = jnp.where(cand_valid, cand_s, -jnp.inf)
        cand_id = _peer_candidates(si[0], si[1], lambda a, b: a * PEER_NKEYS + b)
        best_s, ids = _topk_rows(cand_s, cand_order, cand_id, PEER_TOPK)
        e = jnp.exp(best_s - best_s[0:1])
        rs = slice(head * PEER_TOPK, (head + 1) * PEER_TOPK)
        ids_ref[rs, :] = ids
        gate_ref[rs, :] = e / e.sum(axis=0, keepdims=True)


def _peer_route(x2d, scale, shift, rows_per_mod, wq, subkeys, tb=256):
    T, D = x2d.shape
    nk = PEER_HEADS * PEER_TOPK
    sk = subkeys.reshape(PEER_HEADS * 2, PEER_NKEYS, PEER_SUB_DK).astype(jnp.bfloat16)
    mod = lambda i: (i * tb // rows_per_mod, 0, 0)
    h, ids_t, gate_t = pl.pallas_call(
        _peer_topk_kernel,
        grid=(T // tb,),
        in_specs=[pl.BlockSpec((tb, D), lambda i: (i, 0)),
                  pl.BlockSpec((1, 1, D), mod),
                  pl.BlockSpec((1, 1, D), mod),
                  pl.BlockSpec(wq.shape, lambda i: (0, 0)),
                  pl.BlockSpec(sk.shape, lambda i: (0, 0, 0))],
        out_specs=[pl.BlockSpec((tb, D), lambda i: (i, 0)),
                   pl.BlockSpec((nk, tb), lambda i: (0, i)),
                   pl.BlockSpec((nk, tb), lambda i: (0, i))],
        out_shape=[jax.ShapeDtypeStruct((T, D), jnp.bfloat16),
                   jax.ShapeDtypeStruct((nk, T), jnp.int32),
                   jax.ShapeDtypeStruct((nk, T), jnp.float32)],
        compiler_params=pltpu.CompilerParams(vmem_limit_bytes=48 * 1024 * 1024),
        name="peer_route",
    )(x2d, scale, shift, wq.astype(jnp.bfloat16), sk)
    return h, ids_t.T, gate_t.T


PEER_MASK_SLAB = PEER_NKEYS + 8
PEER_MASK_UNROLL = 8
PEER_KEYS_PER_STEP = 8


def _peer_expert_kernel(x_ref, ids_ref, gate_ref, u_ref, v_ref, o_ref, w_scr):
    f32, bf16 = jnp.float32, jnp.bfloat16
    j = pl.program_id(1)
    tm = x_ref.shape[0]
    nt = (((1,), (1,)), ((), ()))

    @pl.when(j == 0)
    def _():
        o_ref[...] = jnp.zeros_like(o_ref)
        key_iota = lax.broadcasted_iota(jnp.int32, (PEER_NKEYS, ids_ref.shape[1]), 0)

        def token_mask(t):
            ids = ids_ref[pl.ds(t, 1), :]
            g = gate_ref[pl.ds(t, 1), :]
            g_hi = g.astype(bf16).astype(f32)
            g_lo = g - g_hi
            first = jnp.where((ids >> 7) == key_iota, 1.0, 0.0).astype(bf16)
            second = (ids & (PEER_NKEYS - 1)) == key_iota
            w_t = (lax.dot_general(first, jnp.where(second, g_hi, 0.0).astype(bf16), nt, preferred_element_type=f32)
                   + lax.dot_general(first, jnp.where(second, g_lo, 0.0).astype(bf16), nt,
                                     preferred_element_type=f32))
            w_scr[pl.ds(pl.multiple_of(t * PEER_MASK_SLAB, 8), PEER_NKEYS), :] = w_t

        def token_group(i, carry):
            for s in range(PEER_MASK_UNROLL):
                token_mask(i * PEER_MASK_UNROLL + s)
            return carry

        lax.fori_loop(0, tm // PEER_MASK_UNROLL, token_group, 0)

    a = lax.dot_general(x_ref[...], u_ref[...], nt, preferred_element_type=f32)
    w = jnp.concatenate([w_scr[pl.ds(j * PEER_KEYS_PER_STEP + s, tm, stride=PEER_MASK_SLAB), :]
                         for s in range(PEER_KEYS_PER_STEP)], axis=1)
    gelu = 0.5 * a * (1.0 + lax.erf(a * (2.0 ** -0.5)))
    c = (w * gelu).astype(bf16)
    o_ref[...] += jnp.dot(c, v_ref[...], preferred_element_type=f32)


def _peer_expert(tok, ids, gate, u, v, tm=256):
    T, D = tok.shape
    te = PEER_KEYS_PER_STEP * PEER_NKEYS
    nk = ids.shape[1]
    assert PEER_NKEYS == 128 and T % tm == 0 and PEER_EXPERTS % te == 0
    return pl.pallas_call(
        _peer_expert_kernel,
        grid=(T // tm, PEER_EXPERTS // te),
        in_specs=[pl.BlockSpec((tm, D), lambda i, j: (i, 0)),
                  pl.BlockSpec((tm, nk), lambda i, j: (i, 0)),
                  pl.BlockSpec((tm, nk), lambda i, j: (i, 0)),
                  pl.BlockSpec((te, D), lambda i, j: (j, 0)),
                  pl.BlockSpec((te, D), lambda i, j: (j, 0))],
        out_specs=pl.BlockSpec((tm, D), lambda i, j: (i, 0)),
        out_shape=jax.ShapeDtypeStruct((T, D), jnp.float32),
        scratch_shapes=[pltpu.VMEM((tm * PEER_MASK_SLAB, PEER_NKEYS), jnp.float32)],
        compiler_params=pltpu.CompilerParams(dimension_semantics=("arbitrary", "arbitrary"),
                                             vmem_limit_bytes=48 * 1024 * 1024),
        name="peer_expert",
    )(tok, ids, gate, u, v)


def _peer(x2d, scale, shift, rows_per_mod, wq, subkeys, u, v):
    tok, ids, gate = _peer_route(x2d, scale, shift, rows_per_mod, wq, subkeys)
    return _peer_expert(tok, ids, gate, u, v)


def _in_proj_kernel(x_ref, sc_ref, sh_ref, w_ref, o_ref, a_scr):
    @pl.when(pl.program_id(1) == 0)
    def _():
        a_scr[...] = (x_ref[...] * sc_ref[0] + sh_ref[0]).astype(a_scr.dtype)

    o_ref[...] = jnp.dot(a_scr[...], w_ref[...], preferred_element_type=jnp.float32).astype(o_ref.dtype)


def _in_proj(x2d, scale, shift, rows_per_mod, w, col0, ncols, out_dtype, tm=1024, tn=512):
    M, K = x2d.shape
    tm = min(tm, M)
    assert col0 % tn == 0 and M % tm == 0
    mod = lambda i, j: (i * tm // rows_per_mod, 0, 0)
    return pl.pallas_call(
        _in_proj_kernel,
        grid=(M // tm, pl.cdiv(ncols, tn)),
        in_specs=[pl.BlockSpec((tm, K), lambda i, j: (i, 0)),
                  pl.BlockSpec((1, 1, K), mod),
                  pl.BlockSpec((1, 1, K), mod),
                  pl.BlockSpec((K, tn), lambda i, j: (0, col0 // tn + j))],
        out_specs=pl.BlockSpec((tm, tn), lambda i, j: (i, j)),
        out_shape=jax.ShapeDtypeStruct((M, ncols), out_dtype),
        scratch_shapes=[pltpu.VMEM((tm, K), jnp.bfloat16)],
        compiler_params=pltpu.CompilerParams(dimension_semantics=("arbitrary", "arbitrary"),
                                             vmem_limit_bytes=48 * 1024 * 1024),
        name="in_proj",
    )(x2d, scale, shift, w)


def _out_proj_ln_kernel(a_ref, b_ref, c_ref, d_ref, w_ref, x_ref, gt_ref, g_ref, bt_ref, o_ref):
    f32 = jnp.float32
    W = GROUP_WIDTH
    y = jnp.zeros(x_ref.shape, f32)
    for g, r in enumerate((a_ref, b_ref, c_ref, d_ref)):
        y = y + jnp.dot(r[...].astype(jnp.bfloat16), w_ref[g * W:(g + 1) * W, :], preferred_element_type=f32)
    z = DEEPNORM_ALPHA * x_ref[...] + gt_ref[0] * y
    mu = jnp.mean(z, axis=-1, keepdims=True)
    var = jnp.mean(jnp.square(z - mu), axis=-1, keepdims=True)
    o_ref[...] = (z - mu) * lax.rsqrt(var + 1e-5) * g_ref[...] + bt_ref[...]


def _out_proj_ln(parts, w, x2d, gate, rows_per_mod, ln_g, ln_b, tm=256):
    M, D = x2d.shape
    W = GROUP_WIDTH
    tm = min(tm, M)
    row = lambda i: (i, 0)
    return pl.pallas_call(
        _out_proj_ln_kernel,
        grid=(M // tm,),
        in_specs=[pl.BlockSpec((tm, W), row)] * 4
                 + [pl.BlockSpec(w.shape, lambda i: (0, 0)),
                    pl.BlockSpec((tm, D), row),
                    pl.BlockSpec((1, 1, D), lambda i: (i * tm // rows_per_mod, 0, 0)),
                    pl.BlockSpec((1, D), lambda i: (0, 0)),
                    pl.BlockSpec((1, D), lambda i: (0, 0))],
        out_specs=pl.BlockSpec((tm, D), row),
        out_shape=jax.ShapeDtypeStruct((M, D), jnp.float32),
        compiler_params=pltpu.CompilerParams(vmem_limit_bytes=48 * 1024 * 1024),
        name="out_proj_ln",
    )(*parts, w, x2d, gate, ln_g.reshape(1, D), ln_b.reshape(1, D))


def kernel(x, c, ctx, c_ctx, w_ada, b_ada, w_in, w_out, ln_g, ln_b, hg_lower_bounds, hg_norm_w,
           hy_conv_w, hy_conv_b, hy_w1, hy_b1, hy_w2, hy_b2, hy_w3, hy_freq, hy_dbias, na_rpb,
           ml_conv_w, ml_conv_b, ml_gate_b, peer_wq, peer_subkeys, peer_u, peer_v):
    f32, bf16 = jnp.float32, jnp.bfloat16
    B, L, D = x.shape
    n_ctx = ctx.shape[1]
    lb_soft = jax.nn.softmax(hg_lower_bounds.astype(f32), axis=0)
    lower_bounds = jnp.cumsum(lb_soft, axis=0) - lb_soft[0]
    cond_lat = jax.nn.silu(c)
    cond_ctx = jax.nn.silu(c_ctx)
    c0_na = HG_COLS + HY_COLS
    c0_ml = c0_na + NA_COLS
    x2 = x.reshape(B * L, D)
    dft_lat = _hyena_dft_matrices(L)
    xc2 = ctx.reshape(B * n_ctx, D)
    for l in range(DEPTH):
        need_ctx = l < DEPTH - 1
        m_lat = (cond_lat @ w_ada[l] + b_ada[l]).reshape(B, 1, 6, D)
        m_ctx = (cond_ctx @ w_ada[l] + b_ada[l]).reshape(1, 1, 6, D)
        ml = [m_lat[:, :, k] for k in range(6)]
        mc = [m_ctx[:, :, k] for k in range(6)]
        w_in_l = w_in[l].astype(bf16)
        w_out_l = w_out[l].astype(bf16)

        def in_proj(t, m, rows_per_mod):
            sc, sh = 1 + m[1], m[0]
            return (_in_proj(t, sc, sh, rows_per_mod, w_in_l, 0, c0_na, f32),
                    _in_proj(t, sc, sh, rows_per_mod, w_in_l, c0_na, NA_COLS, bf16),
                    _in_proj(t, sc, sh, rows_per_mod, w_in_l, c0_ml, ML_COLS, f32))

        ph_l, pn_l, pm_l = in_proj(x2, ml, L)
        ph_c, pn_c, pm_c = in_proj(xc2, mc, B * n_ctx)
        pb_l = ph_l[:, HG_COLS:].reshape(B, L, -1)
        pb_c = ph_c[:, HG_COLS:].reshape(B, n_ctx, -1)
        hy = (hy_conv_w[l], hy_conv_b[l], hy_w1[l], hy_b1[l], hy_w2[l], hy_b2[l], hy_w3[l], hy_freq[l], hy_dbias[l])
        a_l, a_c = _hgrn2_pallas(ph_l, ph_c, B, lower_bounds[l], hg_norm_w[l], need_ctx)
        b_l = _hyena_seq(pb_l, dft_lat, *hy)
        c_l, c_c = _natten_pallas(pn_l, pn_c, na_rpb[l], need_ctx, B)
        d_l, d_c = _mlstm_pallas(pm_l, pm_c, B, ml_conv_w[l], ml_conv_b[l], ml_gate_b[l], need_ctx)
        parts_l = (a_l, b_l.reshape(B * L, -1), c_l, d_l)
        if need_ctx:
            b_c = _hyena_seq(pb_c, _hyena_dft_matrices(n_ctx), *hy)
            parts_c = (a_c, b_c.reshape(B * n_ctx, -1), c_c, d_c)
            xc2 = _out_proj_ln(parts_c, w_out_l, xc2, mc[2], B * n_ctx, ln_g[l, 0], ln_b[l, 0])
        x2 = _out_proj_ln(parts_l, w_out_l, x2, ml[2], L, ln_g[l, 0], ln_b[l, 0])
        u_l, v_l = peer_u[l].astype(bf16), peer_v[l].astype(bf16)
        y_lat = _peer(x2, 1 + ml[4], ml[3], L, peer_wq[l], peer_subkeys[l], u_l, v_l)
        x2 = _layer_norm(DEEPNORM_ALPHA * x2.reshape(B, L, D) + ml[5] * y_lat.reshape(B, L, D),
                         ln_g[l, 1], ln_b[l, 1]).reshape(B * L, D)
        if need_ctx:
            y_ctx = _peer(xc2, 1 + mc[4], mc[3], B * n_ctx, peer_wq[l], peer_subkeys[l], u_l, v_l)
            xc2 = _layer_norm(DEEPNORM_ALPHA * xc2 + mc[5][0] * y_ctx, ln_g[l, 1], ln_b[l, 1])
    return x2.reshape(B, L, D)
```

```python
import functools
import math
import jax
import jax.numpy as jnp
from jax import lax
from jax.experimental import pallas as pl
from jax.experimental.pallas import tpu as pltpu

D_MODEL = 2048
BATCH = 4
SEQ = 4096
DEPTH = 2

CTX_LEN = 256
GRID_W = 64
GROUP_WIDTH = D_MODEL // 4
MIX_WIDTH = 4 * GROUP_WIDTH
CHUNK = 64
HG_DIM = 128
HG_HEADS = GROUP_WIDTH // HG_DIM
HG_F_MIN = 1e-30
HY_WIDTH = GROUP_WIDTH
HY_ORDER = 2
HY_SHORT = 3
HY_BANDS = 16
HY_EMB = 1 + 2 * HY_BANDS
HY_HIDDEN = 64
HY_FAST_DECAY = 0.3
HY_SLOW_DECAY = 1.5
HY_DECAY_TARGET = 1e-2
HY_MIN_DECAY = math.log(HY_DECAY_TARGET) / HY_SLOW_DECAY
HY_MAX_DECAY = math.log(HY_DECAY_TARGET) / HY_FAST_DECAY
NA_DIM = 64
NA_HEADS = GROUP_WIDTH // NA_DIM
NA_WIN_ROWS = 8
NA_WIN_COLS = 16
NA_QBLK = 16
NA_KBLK = 32
ML_DIM = 128
ML_HEADS = GROUP_WIDTH // ML_DIM
ML_SHORT = 3
ROPE_THETA = 10000.0
PEER_HEADS = 8
PEER_NKEYS = 128
PEER_EXPERTS = PEER_NKEYS * PEER_NKEYS
PEER_DK = 256
PEER_TOPK = 16
PEER_TOKEN_BLOCK = 64
HG_COLS = 5 * GROUP_WIDTH
HY_COLS = 3 * GROUP_WIDTH
NA_COLS = 3 * GROUP_WIDTH
ML_COLS = 4 * GROUP_WIDTH + 4 * ML_HEADS
IN_WIDTH = HG_COLS + HY_COLS + NA_COLS + ML_COLS
DEEPNORM_ALPHA = (2 * DEPTH) ** 0.25
DEEPNORM_BETA = (8 * DEPTH) ** -0.25
MASK_VALUE = -1e30


def _layer_norm(x, g, b, eps=1e-5):
    xf = x.astype(jnp.float32)
    mu = jnp.mean(xf, axis=-1, keepdims=True)
    var = jnp.mean(jnp.square(xf - mu), axis=-1, keepdims=True)
    y = (xf - mu) * lax.rsqrt(var + eps) * g.astype(jnp.float32) + b.astype(jnp.float32)
    return y.astype(x.dtype)


def _rms_norm(x, g, eps=1e-6):
    xf = x.astype(jnp.float32)
    return xf * lax.rsqrt(jnp.mean(jnp.square(xf), axis=-1, keepdims=True) + eps) * g.astype(jnp.float32)


def _centred_dwconv(x, w, b):
    k, L = w.shape[0], x.shape[1]
    xp = jnp.pad(x, ((0, 0), (k // 2, k // 2), (0, 0)))
    y = sum(xp[:, j:j + L] * w[j].astype(x.dtype) for j in range(k))
    return y + b.astype(x.dtype)


def _axial_rope(x):
    L, dh = x.shape[1], x.shape[-1]
    half, quarter = dh // 2, dh // 4
    t = jnp.arange(L)
    inv_freq = ROPE_THETA ** (-jnp.arange(quarter, dtype=jnp.float32) / quarter)

    def rot(xa, pos):
        ang = pos.astype(jnp.float32)[:, None] * inv_freq
        cos = jnp.cos(ang)[None, :, None, :]
        sin = jnp.sin(ang)[None, :, None, :]
        x1, x2 = xa[..., :quarter], xa[..., quarter:]
        return jnp.concatenate([x1 * cos - x2 * sin, x2 * cos + x1 * sin], axis=-1)

    return jnp.concatenate([rot(x[..., :half], t // GRID_W), rot(x[..., half:], t % GRID_W)], axis=-1)


def _chunks(x):
    b, h, L = x.shape[:3]
    return jnp.moveaxis(x.reshape(b, h, L // CHUNK, CHUNK, *x.shape[3:]), 2, 0)


def _unchunks(x):
    x = jnp.moveaxis(x, 0, 2)
    return x.reshape(x.shape[0], x.shape[1], -1, *x.shape[4:])


def _flip(t):
    return jnp.flip(t, axis=2)


HG_SUB = 16
HG_BLOCK = 256


def _hgrn2_kernel(*refs, reverse, readout):
    if readout:
        q_ref, z_ref, v_ref, lb_ref, s0_ref, g_ref, nw_ref, of_ref, o_ref, sfin_ref, st_scr = refs
    else:
        q_ref, z_ref, v_ref, lb_ref, s0_ref, o_ref, sfin_ref, st_scr = refs
    f32, bf16 = jnp.float32, jnp.bfloat16
    hi = lax.Precision.HIGHEST
    C, S = CHUNK, HG_SUB
    nsub = C // S
    n_chunks = q_ref.shape[0] // C
    step = pl.program_id(2)

    @pl.when(step == 0)
    def _():
        st_scr[...] = s0_ref[0, 0]

    nt = (((1,), (1,)), ((), ()))
    tn = (((0,), (0,)), ((), ()))
    r_i = lax.broadcasted_iota(jnp.int32, (C, C), 0)
    c_i = lax.broadcasted_iota(jnp.int32, (C, C), 1)
    tri = jnp.where((c_i >= r_i) if reverse else (c_i <= r_i), 1.0, 0.0).astype(f32)
    row_in_sub = lax.broadcasted_iota(jnp.int32, (S, HG_DIM), 0)
    lb = lb_ref[...]

    def chunk(ci, st):
        c = (n_chunks - 1 - ci) if reverse else ci
        rs = slice(c * C, (c + 1) * C)
        z, q, v = z_ref[rs, :], q_ref[rs, :], v_ref[rs, :]
        neg = jax.nn.sigmoid(-z)
        f = jax.nn.sigmoid(z) + lb * neg
        lf = jnp.log(jnp.maximum(f, HG_F_MIN))
        k = (1.0 - lb) * neg
        cum = jnp.dot(tri, lf, precision=hi, preferred_element_type=f32)
        total = cum[0:1] if reverse else cum[C - 1:C]
        o = lax.dot_general((q * jnp.exp(cum)).astype(bf16), st.astype(bf16), nt, preferred_element_type=f32)
        kd_end = k * jnp.exp(total - cum)
        st_new = jnp.exp(total) * st + lax.dot_general(v.astype(bf16), kd_end.astype(bf16), tn,
                                                       preferred_element_type=f32)
        parts = []
        for i in range(nsub):
            ts = slice(i * S, (i + 1) * S)
            q_i, cum_i = q[ts], cum[ts]
            acc = o[ts]
            ps = slice((i + 1) * S, C) if reverse else slice(0, i * S)
            if ps.stop > ps.start:
                ref = cum[(i + 1) * S:(i + 1) * S + 1] if reverse else cum[i * S - 1:i * S]
                a = lax.dot_general(q_i * jnp.exp(cum_i - ref), k[ps] * jnp.exp(ref - cum[ps]), nt,
                                    precision=hi, preferred_element_type=f32)
                acc = acc + jnp.dot(a.astype(bf16), v[ps].astype(bf16), preferred_element_type=f32)
            for s in range(S):
                r = i * S + s
                seen = (row_in_sub <= s) if reverse else (row_in_sub >= s)
                e = jnp.exp(jnp.where(seen, cum_i - cum[r:r + 1], MASK_VALUE))
                w = jnp.sum(q_i * k[r:r + 1] * e, axis=-1, keepdims=True)
                acc = acc + w * v[r:r + 1]
            parts.append(acc)
        o_c = jnp.concatenate(parts, axis=0)
        if readout:
            x = o_c + of_ref[rs, :]
            y = x * lax.rsqrt(jnp.mean(jnp.square(x), axis=-1, keepdims=True) + 1e-6) * nw_ref[...]
            o_ref[rs, :] = (y * jax.nn.silu(g_ref[rs, :])).astype(o_ref.dtype)
        else:
            o_ref[rs, :] = o_c.astype(o_ref.dtype)
        return st_new

    st = st_scr[...]
    for ci in range(n_chunks):
        st = chunk(ci, st)
    st_scr[...] = st

    @pl.when(step == pl.num_programs(2) - 1)
    def _():
        sfin_ref[0, 0] = st


def _hgrn2_scan_call(p, batch, lb, s0, reverse, z_group, readout_args=None):
    L = p.shape[0] // batch
    tb = min(HG_BLOCK, L)
    n_t = L // tb
    assert L % tb == 0 and tb % CHUNK == 0
    H = HG_HEADS

    def rows(b, h, t):
        return b * n_t + ((n_t - 1 - t) if reverse else t)

    def group(g):
        return pl.BlockSpec((tb, HG_DIM), lambda b, h, t: (rows(b, h, t), g * H + h))

    head_cols = pl.BlockSpec((1, HG_DIM), lambda b, h, t: (0, h))
    state = pl.BlockSpec((1, 1, HG_DIM, HG_DIM), lambda b, h, t: (b, h, 0, 0))
    out_rows = pl.BlockSpec((tb, HG_DIM), lambda b, h, t: (rows(b, h, t), h))
    in_specs = [group(0), group(z_group), group(3), head_cols, state]
    args = [p, p, p, lb, s0]
    readout = readout_args is not None
    if readout:
        norm_w, o_other = readout_args
        in_specs += [group(4), pl.BlockSpec((1, HG_DIM), lambda b, h, t: (0, 0)), out_rows]
        args += [p, norm_w.reshape(1, HG_DIM), o_other]
    return pl.pallas_call(
        functools.partial(_hgrn2_kernel, reverse=reverse, readout=readout),
        grid=(batch, H, n_t),
        in_specs=in_specs,
        out_specs=[out_rows, state],
        out_shape=[jax.ShapeDtypeStruct((batch * L, GROUP_WIDTH), jnp.bfloat16 if readout else jnp.float32),
                   jax.ShapeDtypeStruct((batch, H, HG_DIM, HG_DIM), jnp.float32)],
        scratch_shapes=[pltpu.VMEM((HG_DIM, HG_DIM), jnp.float32)],
        compiler_params=pltpu.CompilerParams(dimension_semantics=("arbitrary", "arbitrary", "arbitrary")),
        name="hgrn2_bwd" if reverse else "hgrn2_fwd",
    )(*args)


def _hgrn2_pallas(p_lat, p_ctx, batch, lb, norm_w, need_ctx):
    zero = jnp.zeros((batch, HG_HEADS, HG_DIM, HG_DIM), jnp.float32)
    lb_f, lb_b = lb[0:1], lb[1:2]
    oc_f, s_f = _hgrn2_scan_call(p_ctx, batch, lb_f, zero, False, 1)
    y_ctx, s_b = _hgrn2_scan_call(p_ctx, batch, lb_b, zero, True, 2, (norm_w, oc_f) if need_ctx else None)
    ol_f, _ = _hgrn2_scan_call(p_lat, batch, lb_f, s_f, False, 1)
    y_lat, _ = _hgrn2_scan_call(p_lat, batch, lb_b, s_b, True, 2, (norm_w, ol_f))
    return y_lat, (y_ctx if need_ctx else None)


def _hyena_filters(L, w1, b1, w2, b2, w3, freq):
    f32 = jnp.float32
    pos = jnp.arange(L, dtype=f32)
    t = pos / (L - 1)
    bands = jnp.linspace(1e-4, HY_BANDS - 1, HY_BANDS, dtype=f32)
    ang = (2.0 * math.pi / L) * pos[:, None] * bands[None, :]
    z = jnp.concatenate([t[:, None], jnp.cos(ang), jnp.sin(ang)], axis=-1)
    freq = freq.astype(f32)
    hid = jnp.sin(freq[0] * (z @ w1.astype(f32) + b1.astype(f32)))
    hid = jnp.sin(freq[1] * (hid @ w2.astype(f32) + b2.astype(f32)))
    h = (hid @ w3.astype(f32)).reshape(L, 2, HY_ORDER, HY_WIDTH)
    deltas = jnp.abs(jnp.linspace(HY_MIN_DECAY, HY_MAX_DECAY, HY_WIDTH, dtype=f32))
    h = h * jnp.exp(-t[:, None] * deltas)[:, None, None, :]
    fwd, bwd = h[:, 0], h[:, 1]
    bwd = bwd.at[0].set(0.0)
    norm = jnp.sum(jnp.abs(fwd), axis=0, keepdims=True) + jnp.sum(jnp.abs(bwd), axis=0, keepdims=True)
    return fwd / norm, bwd / norm


HY_FREQ_TILE = 384
HY_TIME_TILE = 256
HY_ANGLE_SPLIT = 64


def _dft_angle_tables(rows, cols, period):
    f32 = jnp.float32
    r = jnp.arange(rows, dtype=jnp.int32)[:, None]
    ch = jnp.arange(cols // HY_ANGLE_SPLIT, dtype=jnp.int32)[None, :] * HY_ANGLE_SPLIT
    cl = jnp.arange(HY_ANGLE_SPLIT, dtype=jnp.int32)[None, :]
    ah = ((r * ch) % period).astype(f32) * (2.0 * math.pi / period)
    al = ((r * cl) % period).astype(f32) * (2.0 * math.pi / period)
    ca, sa, cb, sb = jnp.cos(ah)[:, :, None], jnp.sin(ah)[:, :, None], jnp.cos(al)[:, None, :], jnp.sin(al)[:, None, :]
    return (ca * cb - sa * sb).reshape(rows, cols), (sa * cb + ca * sb).reshape(rows, cols)


def _hyena_dft_matrices(L):
    N = 2 * L
    mp = -(-(L + 1) // HY_FREQ_TILE) * HY_FREQ_TILE
    bf16 = jnp.bfloat16
    k = jnp.arange(mp)
    live = (k <= L).astype(jnp.float32)
    cf, sf = _dft_angle_tables(mp, L, N)
    ci, si = _dft_angle_tables(L, mp, N)
    w = jnp.where((k == 0) | (k == L), 1.0, 2.0) * live / N
    return ((cf * live[:, None]).astype(bf16), (sf * live[:, None]).astype(bf16),
            (ci * w[None, :]).astype(bf16), (-si * w[None, :]).astype(bf16))


def _dft_pair_kernel(c_ref, s_ref, a_ref, b_ref, oc_ref, os_ref):
    oc_ref[...] = jnp.dot(c_ref[...], a_ref[...], preferred_element_type=jnp.float32)
    os_ref[...] = jnp.dot(s_ref[...], b_ref[...], preferred_element_type=jnp.float32)


def _hyena_filter_spectrum(cf, sf, fwd, bwd):
    mp, L = cf.shape
    C = fwd.shape[1]
    bf16 = jnp.bfloat16
    col = lambda i: (0, 0)
    re, s = pl.pallas_call(
        _dft_pair_kernel,
        grid=(mp // HY_FREQ_TILE,),
        in_specs=[pl.BlockSpec((HY_FREQ_TILE, L), lambda i: (i, 0))] * 2 + [pl.BlockSpec((L, C), col)] * 2,
        out_specs=[pl.BlockSpec((HY_FREQ_TILE, C), lambda i: (i, 0))] * 2,
        out_shape=[jax.ShapeDtypeStruct((mp, C), jnp.float32)] * 2,
        compiler_params=pltpu.CompilerParams(vmem_limit_bytes=48 * 1024 * 1024),
        name="hyena_filter_spectrum",
    )(cf, sf, (fwd + bwd).astype(bf16), (fwd - bwd).astype(bf16))
    return re, -s


def _hyena_fwd_kernel(c_ref, s_ref, z_ref, kre_ref, kim_ref, pre_ref, pim_ref):
    f32 = jnp.float32
    xc = jnp.dot(c_ref[...], z_ref[0], preferred_element_type=f32)
    xs = jnp.dot(s_ref[...], z_ref[0], preferred_element_type=f32)
    pre_ref[0] = (xc * kre_ref[...] + xs * kim_ref[...]).astype(pre_ref.dtype)
    pim_ref[0] = (xc * kim_ref[...] - xs * kre_ref[...]).astype(pim_ref.dtype)


def _hyena_inv_kernel(ci_ref, si_ref, pre_ref, pim_ref, z_ref, d_ref, m_ref, o_ref, ob_ref):
    f32 = jnp.float32
    y = (jnp.dot(ci_ref[...], pre_ref[0], preferred_element_type=f32)
         + jnp.dot(si_ref[...], pim_ref[0], preferred_element_type=f32))
    out = m_ref[0] * (y + z_ref[0] * d_ref[...])
    o_ref[0] = out
    ob_ref[0] = out.astype(ob_ref.dtype)


def _hyena_long_conv(mats, z, z_b, kre, kim, d, mult):
    cf, sf, ci, si = mats
    B, L, C = z.shape
    mp = cf.shape[0]
    tt = min(HY_TIME_TILE, L)
    bf16 = jnp.bfloat16
    pre, pim = pl.pallas_call(
        _hyena_fwd_kernel,
        grid=(mp // HY_FREQ_TILE, B),
        in_specs=[pl.BlockSpec((HY_FREQ_TILE, L), lambda i, b: (i, 0))] * 2
                 + [pl.BlockSpec((1, L, C), lambda i, b: (b, 0, 0))]
                 + [pl.BlockSpec((HY_FREQ_TILE, C), lambda i, b: (i, 0))] * 2,
        out_specs=[pl.BlockSpec((1, HY_FREQ_TILE, C), lambda i, b: (b, i, 0))] * 2,
        out_shape=[jax.ShapeDtypeStruct((B, mp, C), bf16)] * 2,
        compiler_params=pltpu.CompilerParams(vmem_limit_bytes=48 * 1024 * 1024),
        name="hyena_fwd",
    )(cf, sf, z_b, kre, kim)
    row = lambda b, t: (b, t, 0)
    return pl.pallas_call(
        _hyena_inv_kernel,
        grid=(B, L // tt),
        in_specs=[pl.BlockSpec((tt, mp), lambda b, t: (t, 0))] * 2
                 + [pl.BlockSpec((1, mp, C), lambda b, t: (b, 0, 0))] * 2
                 + [pl.BlockSpec((1, tt, C), row), pl.BlockSpec((1, C), lambda b, t: (0, 0)),
                    pl.BlockSpec((1, tt, C), row)],
        out_specs=[pl.BlockSpec((1, tt, C), row)] * 2,
        out_shape=[jax.ShapeDtypeStruct((B, L, C), jnp.float32), jax.ShapeDtypeStruct((B, L, C), bf16)],
        compiler_params=pltpu.CompilerParams(vmem_limit_bytes=48 * 1024 * 1024),
        name="hyena_inv",
    )(ci, si, pre, pim, z, d.reshape(1, C).astype(jnp.float32), mult)


def _hyena_seq(p, mats, conv_w, conv_b, w1, b1, w2, b2, w3, freq, dbias):
    u = _centred_dwconv(p, conv_w, conv_b).astype(jnp.float32)
    v, x1, x2 = jnp.split(u, 3, axis=-1)
    L = p.shape[1]
    fwd, bwd = _hyena_filters(L, w1, b1, w2, b2, w3, freq)
    kre, kim = _hyena_filter_spectrum(mats[0], mats[1], fwd.reshape(L, -1), bwd.reshape(L, -1))
    W = HY_WIDTH
    z, z_b = _hyena_long_conv(mats, v, v.astype(jnp.bfloat16), kre[:, :W], kim[:, :W], dbias[0], x1)
    return _hyena_long_conv(mats, z, z_b, kre[:, W:], kim[:, W:], dbias[1], x2)[1]


def _natten_group(p_lat, p_ctx, rpb, need_ctx):
    f32 = jnp.float32
    B, L, _ = p_lat.shape

    def heads(p):
        return [t.reshape(t.shape[0], t.shape[1], NA_HEADS, NA_DIM) for t in jnp.split(p, 3, axis=-1)]

    q, k, v = heads(p_lat)
    qc, kc, vc = heads(p_ctx)
    scale = NA_DIM ** -0.5
    rows = L // GRID_W
    kh = min(NA_WIN_ROWS, rows)
    ncb = GRID_W // NA_QBLK
    nk = kh * NA_KBLK
    r = jnp.arange(rows)
    key_rows = jnp.clip(r - kh // 2, 0, rows - kh)[:, None] + jnp.arange(kh)
    blk = jnp.arange(ncb)
    qcol = blk[:, None] * NA_QBLK + jnp.arange(NA_QBLK)
    key_cols = (jnp.clip(blk * NA_QBLK - NA_WIN_COLS // 2, 0, GRID_W - NA_KBLK)[:, None]
                + jnp.arange(NA_KBLK))
    idx = (key_rows[:, None, :, None] * GRID_W + key_cols[None, :, None, :]).reshape(rows, ncb, nk)
    col_start = jnp.clip(qcol - NA_WIN_COLS // 2, 0, GRID_W - NA_WIN_COLS)[:, :, None, None]
    kcol = key_cols[:, None, None, :]
    valid = jnp.broadcast_to((kcol >= col_start) & (kcol < col_start + NA_WIN_COLS),
                             (ncb, NA_QBLK, kh, NA_KBLK)).reshape(ncb, NA_QBLK, nk)
    dc = jnp.clip(kcol - qcol[:, :, None, None], 1 - NA_WIN_COLS, NA_WIN_COLS - 1)
    dr = key_rows - r[:, None]
    bias = rpb.astype(f32)[:, dr[:, None, None, :, None] + NA_WIN_ROWS - 1, dc[None] + NA_WIN_COLS - 1]
    bias = jnp.moveaxis(bias.reshape(NA_HEADS, rows, ncb, NA_QBLK, nk), 1, 0)
    q_rows = jnp.moveaxis((q * scale).reshape(B, rows, ncb, NA_QBLK, NA_HEADS, NA_DIM), 1, 0)

    def row_block(inp):
        q_r, idx_r, bias_r = inp
        k_r, v_r = k[:, idx_r], v[:, idx_r]
        s_win = jnp.einsum('bjqhd,bjkhd->bhjqk', q_r, k_r).astype(f32)
        s_win = jnp.where(valid, s_win + bias_r, MASK_VALUE)
        s_ctx = jnp.einsum('bjqhd,bchd->bhjqc', q_r, kc).astype(f32)
        prob = jax.nn.softmax(jnp.concatenate([s_win, s_ctx], axis=-1), axis=-1).astype(v.dtype)
        return (jnp.einsum('bhjqk,bjkhd->bjqhd', prob[..., :nk], v_r)
                + jnp.einsum('bhjqc,bchd->bjqhd', prob[..., nk:], vc))

    out = lax.map(row_block, (q_rows, idx, bias))
    y_lat = jnp.moveaxis(out, 0, 1).reshape(B, L, GROUP_WIDTH)
    y_ctx = None
    if need_ctx:
        s = jnp.einsum('bqhd,bkhd->bhqk', qc * scale, kc).astype(f32)
        prob = jax.nn.softmax(s, axis=-1).astype(vc.dtype)
        y_ctx = jnp.einsum('bhqk,bkhd->bqhd', prob, vc).reshape(B, -1, GROUP_WIDTH)
    return y_lat, y_ctx


def _na_pair_attention(q_pair, key_sets, lane):
    f32 = jnp.float32
    nt = (((1,), (1,)), ((), ()))
    outs = []
    for half in range(2):
        own = (lane >= NA_DIM) if half else (lane < NA_DIM)
        q = jnp.where(own, q_pair, jnp.zeros_like(q_pair))
        scores = []
        for k_pair, _, add, valid in key_sets:
            s = lax.dot_general(q, k_pair, nt, preferred_element_type=f32) * (NA_DIM ** -0.5)
            if add is not None:
                s = jnp.where(valid, s + add[half], MASK_VALUE)
            scores.append(s)
        m = scores[0].max(axis=-1, keepdims=True)
        for s in scores[1:]:
            m = jnp.maximum(m, s.max(axis=-1, keepdims=True))
        den = jnp.zeros_like(m)
        acc = jnp.zeros(q_pair.shape, f32)
        for s, (_, v_pair, _, _) in zip(scores, key_sets):
            p = jnp.exp(s - m)
            den = den + p.sum(axis=-1, keepdims=True)
            acc = acc + jnp.dot(p.astype(jnp.bfloat16), v_pair, preferred_element_type=f32)
        outs.append(acc / den)
    return jnp.where(lane < NA_DIM, outs[0], outs[1])


def _natten_lat_kernel(q_ref, k_ref, v_ref, kc_ref, vc_ref, bias_ref, o_ref, *, rows):
    kh = NA_WIN_ROWS
    r = pl.program_id(1)
    start = pl.multiple_of(jnp.clip(r - kh // 2, 0, rows - kh) * GRID_W, GRID_W)
    nk = kh * GRID_W
    qcol = lax.broadcasted_iota(jnp.int32, (GRID_W, nk), 0)
    kcol = lax.broadcasted_iota(jnp.int32, (GRID_W, nk), 1) % GRID_W
    col_start = jnp.clip(qcol - NA_WIN_COLS // 2, 0, GRID_W - NA_WIN_COLS)
    valid = (kcol >= col_start) & (kcol < col_start + NA_WIN_COLS)
    lane = lax.broadcasted_iota(jnp.int32, (GRID_W, 2 * NA_DIM), 1)
    for hp in range(NA_HEADS // 2):
        cs = slice(hp * 2 * NA_DIM, (hp + 1) * 2 * NA_DIM)
        win = (k_ref[pl.ds(start, nk), cs], v_ref[pl.ds(start, nk), cs],
               (bias_ref[0, 2 * hp], bias_ref[0, 2 * hp + 1]), valid)
        ctx = (kc_ref[:, cs], vc_ref[:, cs], None, None)
        o_ref[:, cs] = _na_pair_attention(q_ref[:, cs], [win, ctx], lane).astype(o_ref.dtype)


def _natten_ctx_kernel(q_ref, k_ref, v_ref, o_ref):
    lane = lax.broadcasted_iota(jnp.int32, (q_ref.shape[0], 2 * NA_DIM), 1)
    for hp in range(NA_HEADS // 2):
        cs = slice(hp * 2 * NA_DIM, (hp + 1) * 2 * NA_DIM)
        o_ref[:, cs] = _na_pair_attention(q_ref[:, cs], [(k_ref[:, cs], v_ref[:, cs], None, None)],
                                          lane).astype(o_ref.dtype)


def _natten_bias_table(rpb, rows):
    kh = NA_WIN_ROWS
    dc = jnp.clip(jnp.arange(GRID_W)[None, :] - jnp.arange(GRID_W)[:, None], 1 - NA_WIN_COLS, NA_WIN_COLS - 1)
    t = rpb.astype(jnp.float32)[:, :, dc + NA_WIN_COLS - 1]
    per_off = [t[:, d:d + kh].transpose(0, 2, 1, 3).reshape(NA_HEADS, GRID_W, kh * GRID_W) for d in range(kh)]
    return jnp.stack(per_off, axis=0)


def _natten_pallas(p_lat, p_ctx, rpb, need_ctx, batch):
    W = GROUP_WIDTH
    L = p_lat.shape[0] // batch
    n_ctx = p_ctx.shape[0] // batch
    rows = L // GRID_W
    kh = NA_WIN_ROWS
    assert rows >= kh and L % GRID_W == 0
    bias = _natten_bias_table(rpb, rows)

    def first_key_row_offset(b, r):
        return (jnp.clip(r - kh // 2, 0, rows - kh) - r + kh - 1, 0, 0, 0)

    y_lat = pl.pallas_call(
        functools.partial(_natten_lat_kernel, rows=rows),
        grid=(batch, rows),
        in_specs=[pl.BlockSpec((GRID_W, W), lambda b, r: (b * rows + r, 0)),
                  pl.BlockSpec((L, W), lambda b, r: (b, 1)),
                  pl.BlockSpec((L, W), lambda b, r: (b, 2)),
                  pl.BlockSpec((n_ctx, W), lambda b, r: (b, 1)),
                  pl.BlockSpec((n_ctx, W), lambda b, r: (b, 2)),
                  pl.BlockSpec((1, NA_HEADS, GRID_W, kh * GRID_W), first_key_row_offset)],
        out_specs=pl.BlockSpec((GRID_W, W), lambda b, r: (b * rows + r, 0)),
        out_shape=jax.ShapeDtypeStruct((batch * L, W), jnp.bfloat16),
        compiler_params=pltpu.CompilerParams(vmem_limit_bytes=40 * 1024 * 1024),
        name="natten_lat",
    )(p_lat, p_lat, p_lat, p_ctx, p_ctx, bias)
    y_ctx = None
    if need_ctx:
        y_ctx = pl.pallas_call(
            _natten_ctx_kernel,
            grid=(batch,),
            in_specs=[pl.BlockSpec((n_ctx, W), lambda b: (b, 0)),
                      pl.BlockSpec((n_ctx, W), lambda b: (b, 1)),
                      pl.BlockSpec((n_ctx, W), lambda b: (b, 2))],
            out_specs=pl.BlockSpec((n_ctx, W), lambda b: (b, 0)),
            out_shape=jax.ShapeDtypeStruct((batch * n_ctx, W), jnp.bfloat16),
            name="natten_ctx",
        )(p_ctx, p_ctx, p_ctx)
    return y_lat, y_ctx


ML_BLOCK = 256
ML_GATES = 4 * ML_HEADS


def _mlstm_kernel(*refs, reverse, readout):
    if readout:
        (q_ref, k_ref, v_ref, gc_ref, gr_ref, c0_ref, n0_ref, m0_ref, og_ref, hf_ref,
         o_ref, cfin_ref, nfin_ref, mfin_ref, c_scr, n_scr, m_scr) = refs
    else:
        (q_ref, k_ref, v_ref, gc_ref, gr_ref, c0_ref, n0_ref, m0_ref,
         o_ref, cfin_ref, nfin_ref, mfin_ref, c_scr, n_scr, m_scr) = refs
    f32, bf16 = jnp.float32, jnp.bfloat16
    hi = lax.Precision.HIGHEST
    C = CHUNK
    n_chunks = q_ref.shape[0] // C
    head = pl.program_id(1)
    step = pl.program_id(2)

    @pl.when(step == 0)
    def _():
        c_scr[...] = c0_ref[0, 0]
        n_scr[...] = n0_ref[0, 0]
        m_scr[...] = m0_ref[0, 0]

    nt = (((1,), (1,)), ((), ()))
    tn = (((0,), (0,)), ((), ()))
    r_i = lax.broadcasted_iota(jnp.int32, (C, C), 0)
    c_i = lax.broadcasted_iota(jnp.int32, (C, C), 1)
    seen = (c_i >= r_i) if reverse else (c_i <= r_i)
    tri = jnp.where(seen, 1.0, 0.0).astype(f32)
    tri_t = jnp.where((r_i >= c_i) if reverse else (r_i <= c_i), 1.0, 0.0).astype(f32)
    ig_idx = (2 if reverse else 0) * ML_HEADS + head
    fg_idx = ig_idx + ML_HEADS
    lane = lax.broadcasted_iota(jnp.int32, (C, ML_GATES), 1)
    sub = lax.broadcasted_iota(jnp.int32, (ML_GATES, C), 0)

    def chunk(ci, state):
        ckv, n, m = state
        c = (n_chunks - 1 - ci) if reverse else ci
        rs = slice(c * C, (c + 1) * C)
        q, k, v = q_ref[rs, :], k_ref[rs, :], v_ref[rs, :]
        gates = gc_ref[rs, :]
        lf_cols = jax.nn.log_sigmoid(gates)
        g_cols = jnp.dot(tri, lf_cols, precision=hi, preferred_element_type=f32)
        g_c = jnp.sum(jnp.where(lane == fg_idx, g_cols, 0.0), axis=1, keepdims=True)
        ig_c = jnp.sum(jnp.where(lane == ig_idx, gates, 0.0), axis=1, keepdims=True)
        gates_r = gr_ref[:, rs]
        ig_r = jnp.sum(jnp.where(sub == ig_idx, gates_r, 0.0), axis=0, keepdims=True)
        lf_r = jnp.sum(jnp.where(sub == fg_idx, jax.nn.log_sigmoid(gates_r), 0.0), axis=0, keepdims=True)
        g_r = jnp.dot(lf_r, tri_t, precision=hi, preferred_element_type=f32)
        m1 = m[:, 0:1]
        logd = jnp.where(seen, g_c - g_r + ig_r, MASK_VALUE)
        log_inter = g_c + m1
        m_out = jnp.maximum(log_inter, jnp.max(logd, axis=1, keepdims=True))
        dmat = jnp.exp(logd - m_out)
        w_inter = jnp.exp(log_inter - m_out)
        qb = q.astype(bf16)
        sc = lax.dot_general(qb, k.astype(bf16), nt, preferred_element_type=f32) * dmat
        num = (jnp.dot(sc.astype(bf16), v.astype(bf16), preferred_element_type=f32)
               + w_inter * jnp.dot(qb, ckv.astype(bf16), preferred_element_type=f32))
        den = jnp.sum(sc, axis=1, keepdims=True) + w_inter * jnp.sum(q * n, axis=1, keepdims=True)
        h = num / jnp.maximum(jnp.abs(den), jnp.exp(-m_out))
        g_end = g_c[0:1] if reverse else g_c[C - 1:C]
        a = g_end - g_c + ig_c
        m_new = jnp.maximum(g_end + m1, jnp.max(a, axis=0, keepdims=True))
        carry_w = jnp.exp(g_end + m1 - m_new)
        wk = jnp.exp(a - m_new) * k
        ckv_new = carry_w * ckv + lax.dot_general(wk.astype(bf16), v.astype(bf16), tn, preferred_element_type=f32)
        n_new = carry_w * n + jnp.sum(wk, axis=0, keepdims=True)
        if readout:
            o_ref[rs, :] = (jax.nn.sigmoid(og_ref[rs, :]) * (h + hf_ref[rs, :])).astype(o_ref.dtype)
        else:
            o_ref[rs, :] = h.astype(o_ref.dtype)
        return ckv_new, n_new, jnp.broadcast_to(m_new, m.shape)

    state = (c_scr[...], n_scr[...], m_scr[...])
    for ci in range(n_chunks):
        state = chunk(ci, state)
    c_scr[...], n_scr[...], m_scr[...] = state

    @pl.when(step == pl.num_programs(2) - 1)
    def _():
        cfin_ref[0, 0], nfin_ref[0, 0], mfin_ref[0, 0] = state


def _mlstm_scan_call(q, k, v_src, v_col0, gates, gates_t, batch, state0, reverse, readout_args=None):
    L = q.shape[0] // batch
    tb = min(ML_BLOCK, L)
    n_t = L // tb
    assert L % tb == 0 and tb % CHUNK == 0 and tb % 128 == 0
    H = ML_HEADS

    def rows(b, h, t):
        return b * n_t + ((n_t - 1 - t) if reverse else t)

    head_block = pl.BlockSpec((tb, ML_DIM), lambda b, h, t: (rows(b, h, t), h))
    mat_state = pl.BlockSpec((1, 1, ML_DIM, ML_DIM), lambda b, h, t: (b, h, 0, 0))
    vec_state = pl.BlockSpec((1, 1, 1, ML_DIM), lambda b, h, t: (b, h, 0, 0))
    in_specs = [head_block, head_block,
                pl.BlockSpec((tb, ML_DIM), lambda b, h, t: (rows(b, h, t), v_col0 + h)),
                pl.BlockSpec((tb, ML_GATES), lambda b, h, t: (rows(b, h, t), 0)),
                pl.BlockSpec((ML_GATES, tb), lambda b, h, t: (0, rows(b, h, t))),
                mat_state, vec_state, vec_state]
    args = [q, k, v_src, gates, gates_t, *state0]
    readout = readout_args is not None
    if readout:
        og_src, og_col0, h_other = readout_args
        in_specs += [pl.BlockSpec((tb, ML_DIM), lambda b, h, t: (rows(b, h, t), og_col0 + h)), head_block]
        args += [og_src, h_other]
    outs = pl.pallas_call(
        functools.partial(_mlstm_kernel, reverse=reverse, readout=readout),
        grid=(batch, H, n_t),
        in_specs=in_specs,
        out_specs=[head_block, mat_state, vec_state, vec_state],
        out_shape=[jax.ShapeDtypeStruct((batch * L, GROUP_WIDTH), jnp.bfloat16 if readout else jnp.float32),
                   jax.ShapeDtypeStruct((batch, H, ML_DIM, ML_DIM), jnp.float32),
                   jax.ShapeDtypeStruct((batch, H, 1, ML_DIM), jnp.float32),
                   jax.ShapeDtypeStruct((batch, H, 1, ML_DIM), jnp.float32)],
        scratch_shapes=[pltpu.VMEM((ML_DIM, ML_DIM), jnp.float32), pltpu.VMEM((1, ML_DIM), jnp.float32),
                        pltpu.VMEM((1, ML_DIM), jnp.float32)],
        compiler_params=pltpu.CompilerParams(dimension_semantics=("arbitrary", "arbitrary", "arbitrary")),
        name="mlstm_bwd" if reverse else "mlstm_fwd",
    )(*args)
    return outs[0], tuple(outs[1:])


def _mlstm_pallas(pm_lat, pm_ctx, batch, conv_w, conv_b, gate_b, need_ctx):
    W = GROUP_WIDTH
    f32 = jnp.float32

    def prep(pm, rope):
        L = pm.shape[0] // batch
        qk = jax.nn.silu(_centred_dwconv(pm[:, :2 * W].reshape(batch, L, 2 * W), conv_w, conv_b)).astype(f32)
        q = qk[..., :W].reshape(batch, L, ML_HEADS, ML_DIM)
        k = qk[..., W:].reshape(batch, L, ML_HEADS, ML_DIM)
        if rope:
            q, k = _axial_rope(q), _axial_rope(k)
        gates = pm[:, 4 * W:] + gate_b.astype(f32)
        return q.reshape(batch * L, W), (k * ML_DIM ** -0.5).reshape(batch * L, W), gates, gates.T

    zero = (jnp.zeros((batch, ML_HEADS, ML_DIM, ML_DIM), f32), jnp.zeros((batch, ML_HEADS, 1, ML_DIM), f32),
            jnp.zeros((batch, ML_HEADS, 1, ML_DIM), f32))
    v0, o0 = 2 * W // ML_DIM, 3 * W // ML_DIM
    qc, kc, gc, gct = prep(pm_ctx, False)
    hc_f, st_f = _mlstm_scan_call(qc, kc, pm_ctx, v0, gc, gct, batch, zero, False)
    y_ctx, st_b = _mlstm_scan_call(qc, kc, pm_ctx, v0, gc, gct, batch, zero, True,
                                   (pm_ctx, o0, hc_f) if need_ctx else None)
    ql, kl, gl, glt = prep(pm_lat, True)
    hl_f, _ = _mlstm_scan_call(ql, kl, pm_lat, v0, gl, glt, batch, st_f, False)
    y_lat, _ = _mlstm_scan_call(ql, kl, pm_lat, v0, gl, glt, batch, st_b, True, (pm_lat, o0, hl_f))
    return y_lat, (y_ctx if need_ctx else None)


PEER_SUB_DK = PEER_DK // 2
PEER_CAND_ROWS = 16 + 8 + 6 * 8 + 8


def _topk_rows(s, order, payload, k):
    sentinel = jnp.int32(2 ** 30)
    vals, picked = [], []
    for _ in range(k):
        m = s.max(axis=0, keepdims=True)
        first = jnp.min(jnp.where(s == m, order, sentinel), axis=0, keepdims=True)
        hit = order == first
        vals.append(m)
        picked.append(first if payload is None else jnp.max(jnp.where(hit, payload, -1), axis=0, keepdims=True))
        s = jnp.where(hit, -jnp.inf, s)
    return jnp.concatenate(vals, axis=0), jnp.concatenate(picked, axis=0)


def _peer_candidates(a0, a1, combine):
    pieces = [combine(a0[0:1], a1), combine(a0[1:2], a1[0:8])]
    pieces += [combine(a0[i:i + 1], a1[0:8]) for i in range(2, 8)]
    pieces.append(combine(a0[8:16], a1[0:1]))
    return jnp.concatenate(pieces, axis=0)


def _peer_topk_kernel(x_ref, sc_ref, sh_ref, wq_ref, sk_ref, h_ref, ids_ref, gate_ref):
    f32 = jnp.float32
    tb = x_ref.shape[0]
    h = x_ref[...] * sc_ref[0] + sh_ref[0]
    h_ref[...] = h.astype(h_ref.dtype)
    q = jnp.dot(h.astype(jnp.bfloat16), wq_ref[...], preferred_element_type=f32).astype(jnp.bfloat16)
    key_order = lax.broadcasted_iota(jnp.int32, (PEER_NKEYS, tb), 0)
    rho = lax.broadcasted_iota(jnp.int32, (PEER_CAND_ROWS, tb), 0)
    ci = jnp.where(rho < 16, 0, jnp.where(rho < 24, 1, jnp.where(rho < 72, 2 + ((rho - 24) >> 3), rho - 64)))
    cj = jnp.where(rho < 16, rho, jnp.where(rho < 24, rho - 16, jnp.where(rho < 72, (rho - 24) & 7, 0)))
    cand_order = ci * PEER_TOPK + cj
    cand_valid = (ci + 1) * (cj + 1) <= PEER_TOPK
    nt = (((1,), (1,)), ((), ()))
    for head in range(PEER_HEADS):
        sv, si = [], []
        for half in range(2):
            c0 = (head * 2 + half) * PEER_SUB_DK
            s_t = lax.dot_general(sk_ref[head * 2 + half], q[:, c0:c0 + PEER_SUB_DK], nt,
                                  preferred_element_type=f32)
            v, i = _topk_rows(s_t, key_order, None, PEER_TOPK)
            sv.append(v)
            si.append(i)
        cand_s = _peer_candidates(sv[0], sv[1], lambda a, b: a + b)
        cand_s = jnp.where(cand_valid, cand_s, -jnp.inf)
        cand_id = _peer_candidates(si[0], si[1], lambda a, b: a * PEER_NKEYS + b)
        best_s, ids = _topk_rows(cand_s, cand_order, cand_id, PEER_TOPK)
        e = jnp.exp(best_s - best_s[0:1])
        rs = slice(head * PEER_TOPK, (head + 1) * PEER_TOPK)
        ids_ref[rs, :] = ids
        gate_ref[rs, :] = e / e.sum(axis=0, keepdims=True)


def _peer_route(x2d, scale, shift, rows_per_mod, wq, subkeys, tb=256):
    T, D = x2d.shape
    nk = PEER_HEADS * PEER_TOPK
    sk = subkeys.reshape(PEER_HEADS * 2, PEER_NKEYS, PEER_SUB_DK).astype(jnp.bfloat16)
    mod = lambda i: (i * tb // rows_per_mod, 0, 0)
    h, ids_t, gate_t = pl.pallas_call(
        _peer_topk_kernel,
        grid=(T // tb,),
        in_specs=[pl.BlockSpec((tb, D), lambda i: (i, 0)),
                  pl.BlockSpec((1, 1, D), mod),
                  pl.BlockSpec((1, 1, D), mod),
                  pl.BlockSpec(wq.shape, lambda i: (0, 0)),
                  pl.BlockSpec(sk.shape, lambda i: (0, 0, 0))],
        out_specs=[pl.BlockSpec((tb, D), lambda i: (i, 0)),
                   pl.BlockSpec((nk, tb), lambda i: (0, i)),
                   pl.BlockSpec((nk, tb), lambda i: (0, i))],
        out_shape=[jax.ShapeDtypeStruct((T, D), jnp.bfloat16),
                   jax.ShapeDtypeStruct((nk, T), jnp.int32),
                   jax.ShapeDtypeStruct((nk, T), jnp.float32)],
        compiler_params=pltpu.CompilerParams(vmem_limit_bytes=48 * 1024 * 1024),
        name="peer_route",
    )(x2d, scale, shift, wq.astype(jnp.bfloat16), sk)
    return h, ids_t.T, gate_t.T


PEER_MASK_SLAB = PEER_NKEYS + 8
PEER_MASK_UNROLL = 8
PEER_KEYS_PER_STEP = 8
PEER_TOKEN_TILE = 512


def _bf16_bits(x):
    u = lax.bitcast_convert_type(x, jnp.uint32)
    return (u + jnp.uint32(0x7FFF) + ((u >> 16) & jnp.uint32(1))) >> 16


def _peer_expert_kernel(x_ref, ids_ref, gate_ref, u_ref, v_ref, o_ref, w_scr):
    f32, bf16 = jnp.float32, jnp.bfloat16
    j = pl.program_id(1)
    tm = x_ref.shape[0]
    half = tm // 2
    nt = (((1,), (1,)), ((), ()))

    @pl.when(j == 0)
    def _():
        o_ref[...] = jnp.zeros_like(o_ref)
        key_iota = lax.broadcasted_iota(jnp.int32, (PEER_NKEYS, ids_ref.shape[1]), 0)

        def token_mask(t):
            ids = ids_ref[pl.ds(t, 1), :]
            g = gate_ref[pl.ds(t, 1), :]
            first = jnp.where((ids >> 7) == key_iota, 1.0, 0.0).astype(bf16)
            second = jnp.where((ids & (PEER_NKEYS - 1)) == key_iota, g, 0.0).astype(bf16)
            return lax.dot_general(first, second, nt, preferred_element_type=f32)

        def pair_group(i, carry):
            for s in range(PEER_MASK_UNROLL // 2):
                p = i * (PEER_MASK_UNROLL // 2) + s
                packed = _bf16_bits(token_mask(p)) | (_bf16_bits(token_mask(p + half)) << 16)
                w_scr[pl.ds(pl.multiple_of(p * PEER_MASK_SLAB, 8), PEER_NKEYS), :] = packed
            return carry

        lax.fori_loop(0, half // (PEER_MASK_UNROLL // 2), pair_group, 0)

    a = lax.dot_general(x_ref[...], u_ref[...], nt, preferred_element_type=f32)
    cols = []
    for s in range(PEER_KEYS_PER_STEP):
        packed = w_scr[pl.ds(j * PEER_KEYS_PER_STEP + s, half, stride=PEER_MASK_SLAB), :]
        lo = lax.bitcast_convert_type(packed << 16, f32)
        hi = lax.bitcast_convert_type(packed & jnp.uint32(0xFFFF0000), f32)
        cols.append(jnp.concatenate([lo, hi], axis=0))
    w = jnp.concatenate(cols, axis=1)
    gelu = 0.5 * a * (1.0 + lax.erf(a * (2.0 ** -0.5)))
    c = (w * gelu).astype(bf16)
    o_ref[...] += jnp.dot(c, v_ref[...], preferred_element_type=f32)


def _peer_expert(tok, ids, gate, u, v):
    T, D = tok.shape
    tm = min(PEER_TOKEN_TILE, T)
    te = PEER_KEYS_PER_STEP * PEER_NKEYS
    nk = ids.shape[1]
    assert PEER_NKEYS == 128 and T % tm == 0 and PEER_EXPERTS % te == 0 and (tm // 2) % PEER_MASK_UNROLL == 0
    return pl.pallas_call(
        _peer_expert_kernel,
        grid=(T // tm, PEER_EXPERTS // te),
        in_specs=[pl.BlockSpec((tm, D), lambda i, j: (i, 0)),
                  pl.BlockSpec((tm, nk), lambda i, j: (i, 0)),
                  pl.BlockSpec((tm, nk), lambda i, j: (i, 0)),
                  pl.BlockSpec((te, D), lambda i, j: (j, 0)),
                  pl.BlockSpec((te, D), lambda i, j: (j, 0))],
        out_specs=pl.BlockSpec((tm, D), lambda i, j: (i, 0)),
        out_shape=jax.ShapeDtypeStruct((T, D), jnp.float32),
        scratch_shapes=[pltpu.VMEM((tm // 2 * PEER_MASK_SLAB, PEER_NKEYS), jnp.uint32)],
        compiler_params=pltpu.CompilerParams(dimension_semantics=("arbitrary", "arbitrary"),
                                             vmem_limit_bytes=56 * 1024 * 1024),
        name="peer_expert",
    )(tok, ids, gate, u, v)


def _peer(x2d, scale, shift, rows_per_mod, wq, subkeys, u, v):
    tok, ids, gate = _peer_route(x2d, scale, shift, rows_per_mod, wq, subkeys)
    return _peer_expert(tok, ids, gate, u, v)


def _in_proj_kernel(x_ref, sc_ref, sh_ref, w_ref, o_ref, a_scr):
    @pl.when(pl.program_id(1) == 0)
    def _():
        a_scr[...] = (x_ref[...] * sc_ref[0] + sh_ref[0]).astype(a_scr.dtype)

    o_ref[...] = jnp.dot(a_scr[...], w_ref[...], preferred_element_type=jnp.float32).astype(o_ref.dtype)


def _in_proj(x2d, scale, shift, rows_per_mod, w, col0, ncols, out_dtype, tm=1024, tn=512):
    M, K = x2d.shape
    tm = min(tm, M)
    assert col0 % tn == 0 and M % tm == 0
    mod = lambda i, j: (i * tm // rows_per_mod, 0, 0)
    return pl.pallas_call(
        _in_proj_kernel,
        grid=(M // tm, pl.cdiv(ncols, tn)),
        in_specs=[pl.BlockSpec((tm, K), lambda i, j: (i, 0)),
                  pl.BlockSpec((1, 1, K), mod),
                  pl.BlockSpec((1, 1, K), mod),
                  pl.BlockSpec((K, tn), lambda i, j: (0, col0 // tn + j))],
        out_specs=pl.BlockSpec((tm, tn), lambda i, j: (i, j)),
        out_shape=jax.ShapeDtypeStruct((M, ncols), out_dtype),
        scratch_shapes=[pltpu.VMEM((tm, K), jnp.bfloat16)],
        compiler_params=pltpu.CompilerParams(dimension_semantics=("arbitrary", "arbitrary"),
                                             vmem_limit_bytes=48 * 1024 * 1024),
        name="in_proj",
    )(x2d, scale, shift, w)


def _out_proj_ln_kernel(a_ref, b_ref, c_ref, d_ref, w_ref, x_ref, gt_ref, g_ref, bt_ref, o_ref):
    f32 = jnp.float32
    W = GROUP_WIDTH
    y = jnp.zeros(x_ref.shape, f32)
    for g, r in enumerate((a_ref, b_ref, c_ref, d_ref)):
        y = y + jnp.dot(r[...].astype(jnp.bfloat16), w_ref[g * W:(g + 1) * W, :], preferred_element_type=f32)
    z = DEEPNORM_ALPHA * x_ref[...] + gt_ref[0] * y
    mu = jnp.mean(z, axis=-1, keepdims=True)
    var = jnp.mean(jnp.square(z - mu), axis=-1, keepdims=True)
    o_ref[...] = (z - mu) * lax.rsqrt(var + 1e-5) * g_ref[...] + bt_ref[...]


def _out_proj_ln(parts, w, x2d, gate, rows_per_mod, ln_g, ln_b, tm=256):
    M, D = x2d.shape
    W = GROUP_WIDTH
    tm = min(tm, M)
    row = lambda i: (i, 0)
    return pl.pallas_call(
        _out_proj_ln_kernel,
        grid=(M // tm,),
        in_specs=[pl.BlockSpec((tm, W), row)] * 4
                 + [pl.BlockSpec(w.shape, lambda i: (0, 0)),
                    pl.BlockSpec((tm, D), row),
                    pl.BlockSpec((1, 1, D), lambda i: (i * tm // rows_per_mod, 0, 0)),
                    pl.BlockSpec((1, D), lambda i: (0, 0)),
                    pl.BlockSpec((1, D), lambda i: (0, 0))],
        out_specs=pl.BlockSpec((tm, D), row),
        out_shape=jax.ShapeDtypeStruct((M, D), jnp.float32),
        compiler_params=pltpu.CompilerParams(vmem_limit_bytes=48 * 1024 * 1024),
        name="out_proj_ln",
    )(*parts, w, x2d, gate, ln_g.reshape(1, D), ln_b.reshape(1, D))


def kernel(x, c, ctx, c_ctx, w_ada, b_ada, w_in, w_out, ln_g, ln_b, hg_lower_bounds, hg_norm_w,
           hy_conv_w, hy_conv_b, hy_w1, hy_b1, hy_w2, hy_b2, hy_w3, hy_freq, hy_dbias, na_rpb,
           ml_conv_w, ml_conv_b, ml_gate_b, peer_wq, peer_subkeys, peer_u, peer_v):
    f32, bf16 = jnp.float32, jnp.bfloat16
    B, L, D = x.shape
    n_ctx = ctx.shape[1]
    lb_soft = jax.nn.softmax(hg_lower_bounds.astype(f32), axis=0)
    lower_bounds = jnp.cumsum(lb_soft, axis=0) - lb_soft[0]
    cond_lat = jax.nn.silu(c)
    cond_ctx = jax.nn.silu(c_ctx)
    c0_na = HG_COLS + HY_COLS
    c0_ml = c0_na + NA_COLS
    x2 = x.reshape(B * L, D)
    dft_lat = _hyena_dft_matrices(L)
    xc2 = ctx.reshape(B * n_ctx, D)
    for l in range(DEPTH):
        need_ctx = l < DEPTH - 1
        m_lat = (cond_lat @ w_ada[l] + b_ada[l]).reshape(B, 1, 6, D)
        m_ctx = (cond_ctx @ w_ada[l] + b_ada[l]).reshape(1, 1, 6, D)
        ml = [m_lat[:, :, k] for k in range(6)]
        mc = [m_ctx[:, :, k] for k in range(6)]
        w_in_l = w_in[l].astype(bf16)
        w_out_l = w_out[l].astype(bf16)

        def in_proj(t, m, rows_per_mod):
            sc, sh = 1 + m[1], m[0]
            return (_in_proj(t, sc, sh, rows_per_mod, w_in_l, 0, c0_na, f32),
                    _in_proj(t, sc, sh, rows_per_mod, w_in_l, c0_na, NA_COLS, bf16),
                    _in_proj(t, sc, sh, rows_per_mod, w_in_l, c0_ml, ML_COLS, f32))

        ph_l, pn_l, pm_l = in_proj(x2, ml, L)
        ph_c, pn_c, pm_c = in_proj(xc2, mc, B * n_ctx)
        pb_l = ph_l[:, HG_COLS:].reshape(B, L, -1)
        pb_c = ph_c[:, HG_COLS:].reshape(B, n_ctx, -1)
        hy = (hy_conv_w[l], hy_conv_b[l], hy_w1[l], hy_b1[l], hy_w2[l], hy_b2[l], hy_w3[l], hy_freq[l], hy_dbias[l])
        a_l, a_c = _hgrn2_pallas(ph_l, ph_c, B, lower_bounds[l], hg_norm_w[l], need_ctx)
        b_l = _hyena_seq(pb_l, dft_lat, *hy)
        c_l, c_c = _natten_pallas(pn_l, pn_c, na_rpb[l], need_ctx, B)
        d_l, d_c = _mlstm_pallas(pm_l, pm_c, B, ml_conv_w[l], ml_conv_b[l], ml_gate_b[l], need_ctx)
        parts_l = (a_l, b_l.reshape(B * L, -1), c_l, d_l)
        if need_ctx:
            b_c = _hyena_seq(pb_c, _hyena_dft_matrices(n_ctx), *hy)
            parts_c = (a_c, b_c.reshape(B * n_ctx, -1), c_c, d_c)
            xc2 = _out_proj_ln(parts_c, w_out_l, xc2, mc[2], B * n_ctx, ln_g[l, 0], ln_b[l, 0])
        x2 = _out_proj_ln(parts_l, w_out_l, x2, ml[2], L, ln_g[l, 0], ln_b[l, 0])
        u_l, v_l = peer_u[l].astype(bf16), peer_v[l].astype(bf16)
        y_lat = _peer(x2, 1 + ml[4], ml[3], L, peer_wq[l], peer_subkeys[l], u_l, v_l)
        x2 = _layer_norm(DEEPNORM_ALPHA * x2.reshape(B, L, D) + ml[5] * y_lat.reshape(B, L, D),
                         ln_g[l, 1], ln_b[l, 1]).reshape(B * L, D)
        if need_ctx:
            y_ctx = _peer(xc2, 1 + mc[4], mc[3], B * n_ctx, peer_wq[l], peer_subkeys[l], u_l, v_l)
            xc2 = _layer_norm(DEEPNORM_ALPHA * xc2 + mc[5][0] * y_ctx, ln_g[l, 1], ln_b[l, 1])
    return x2.reshape(B, L, D)
```

```python
import functools
import math
import jax
import jax.numpy as jnp
from jax import lax
from jax.experimental import pallas as pl
from jax.experimental.pallas import tpu as pltpu

D_MODEL = 2048
BATCH = 4
SEQ = 4096
DEPTH = 2

CTX_LEN = 256
GRID_W = 64
GROUP_WIDTH = D_MODEL // 4
MIX_WIDTH = 4 * GROUP_WIDTH
CHUNK = 64
HG_DIM = 128
HG_HEADS = GROUP_WIDTH // HG_DIM
HG_F_MIN = 1e-30
HY_WIDTH = GROUP_WIDTH
HY_ORDER = 2
HY_SHORT = 3
HY_BANDS = 16
HY_EMB = 1 + 2 * HY_BANDS
HY_HIDDEN = 64
HY_FAST_DECAY = 0.3
HY_SLOW_DECAY = 1.5
HY_DECAY_TARGET = 1e-2
HY_MIN_DECAY = math.log(HY_DECAY_TARGET) / HY_SLOW_DECAY
HY_MAX_DECAY = math.log(HY_DECAY_TARGET) / HY_FAST_DECAY
NA_DIM = 64
NA_HEADS = GROUP_WIDTH // NA_DIM
NA_WIN_ROWS = 8
NA_WIN_COLS = 16
NA_QBLK = 16
NA_KBLK = 32
ML_DIM = 128
ML_HEADS = GROUP_WIDTH // ML_DIM
ML_SHORT = 3
ROPE_THETA = 10000.0
PEER_HEADS = 8
PEER_NKEYS = 128
PEER_EXPERTS = PEER_NKEYS * PEER_NKEYS
PEER_DK = 256
PEER_TOPK = 16
HG_COLS = 5 * GROUP_WIDTH
HY_COLS = 3 * GROUP_WIDTH
NA_COLS = 3 * GROUP_WIDTH
ML_COLS = 4 * GROUP_WIDTH + 4 * ML_HEADS
IN_WIDTH = HG_COLS + HY_COLS + NA_COLS + ML_COLS
DEEPNORM_ALPHA = (2 * DEPTH) ** 0.25
DEEPNORM_BETA = (8 * DEPTH) ** -0.25
MASK_VALUE = -1e30


def _layer_norm(x, g, b, eps=1e-5):
    xf = x.astype(jnp.float32)
    mu = jnp.mean(xf, axis=-1, keepdims=True)
    var = jnp.mean(jnp.square(xf - mu), axis=-1, keepdims=True)
    y = (xf - mu) * lax.rsqrt(var + eps) * g.astype(jnp.float32) + b.astype(jnp.float32)
    return y.astype(x.dtype)


def _centred_dwconv(x, w, b):
    k, L = w.shape[0], x.shape[1]
    xp = jnp.pad(x, ((0, 0), (k // 2, k // 2), (0, 0)))
    y = sum(xp[:, j:j + L] * w[j].astype(x.dtype) for j in range(k))
    return y + b.astype(x.dtype)


def _axial_rope(x):
    L, dh = x.shape[1], x.shape[-1]
    half, quarter = dh // 2, dh // 4
    t = jnp.arange(L)
    inv_freq = ROPE_THETA ** (-jnp.arange(quarter, dtype=jnp.float32) / quarter)

    def rot(xa, pos):
        ang = pos.astype(jnp.float32)[:, None] * inv_freq
        cos = jnp.cos(ang)[None, :, None, :]
        sin = jnp.sin(ang)[None, :, None, :]
        x1, x2 = xa[..., :quarter], xa[..., quarter:]
        return jnp.concatenate([x1 * cos - x2 * sin, x2 * cos + x1 * sin], axis=-1)

    return jnp.concatenate([rot(x[..., :half], t // GRID_W), rot(x[..., half:], t % GRID_W)], axis=-1)


HG_SUB = 16
HG_BLOCK = 256


def _bf16_terms(x):
    f32, bf16 = jnp.float32, jnp.bfloat16
    x_hi = x.astype(bf16)
    r1 = x - x_hi.astype(f32)
    x_mid = r1.astype(bf16)
    return x_hi, x_mid, (r1 - x_mid.astype(f32)).astype(bf16)


def _masked_sums(mask, x):
    return sum(jnp.dot(mask, t, preferred_element_type=jnp.float32) for t in _bf16_terms(x))


def _hgrn2_kernel(*refs, reverse, readout):
    if readout:
        q_ref, z_ref, v_ref, lb_ref, s0_ref, g_ref, nw_ref, of_ref, o_ref, sfin_ref, st_scr = refs
    else:
        q_ref, z_ref, v_ref, lb_ref, s0_ref, o_ref, sfin_ref, st_scr = refs
    f32, bf16 = jnp.float32, jnp.bfloat16
    C, S = CHUNK, HG_SUB
    nsub = C // S
    n_chunks = q_ref.shape[0] // C
    step = pl.program_id(2)

    @pl.when(step == 0)
    def _():
        st_scr[...] = s0_ref[0, 0]

    nt = (((1,), (1,)), ((), ()))
    tn = (((0,), (0,)), ((), ()))
    r_i = lax.broadcasted_iota(jnp.int32, (C, C), 0)
    c_i = lax.broadcasted_iota(jnp.int32, (C, C), 1)
    tri = jnp.where((c_i >= r_i) if reverse else (c_i <= r_i), 1.0, 0.0).astype(bf16)
    row_in_sub = lax.broadcasted_iota(jnp.int32, (S, HG_DIM), 0)
    lb = lb_ref[...]

    def chunk(ci, st):
        c = (n_chunks - 1 - ci) if reverse else ci
        rs = slice(c * C, (c + 1) * C)
        z, q, v = z_ref[rs, :], q_ref[rs, :], v_ref[rs, :]
        neg = jax.nn.sigmoid(-z)
        f = jax.nn.sigmoid(z) + lb * neg
        lf = jnp.log(jnp.maximum(f, HG_F_MIN))
        k = (1.0 - lb) * neg
        cum = _masked_sums(tri, lf)
        total = cum[0:1] if reverse else cum[C - 1:C]
        o = lax.dot_general((q * jnp.exp(cum)).astype(bf16), st.astype(bf16), nt, preferred_element_type=f32)
        kd_end = k * jnp.exp(total - cum)
        st_new = jnp.exp(total) * st + lax.dot_general(v.astype(bf16), kd_end.astype(bf16), tn,
                                                       preferred_element_type=f32)
        parts = []
        for i in range(nsub):
            ts = slice(i * S, (i + 1) * S)
            q_i, cum_i = q[ts], cum[ts]
            acc = o[ts]
            ps = slice((i + 1) * S, C) if reverse else slice(0, i * S)
            if ps.stop > ps.start:
                ref = cum[(i + 1) * S:(i + 1) * S + 1] if reverse else cum[i * S - 1:i * S]
                a = lax.dot_general((q_i * jnp.exp(cum_i - ref)).astype(bf16),
                                    (k[ps] * jnp.exp(ref - cum[ps])).astype(bf16), nt, preferred_element_type=f32)
                acc = acc + jnp.dot(a.astype(bf16), v[ps].astype(bf16), preferred_element_type=f32)
            for s in range(S):
                r = i * S + s
                seen = (row_in_sub <= s) if reverse else (row_in_sub >= s)
                e = jnp.exp(jnp.where(seen, cum_i - cum[r:r + 1], MASK_VALUE))
                w = jnp.sum(q_i * k[r:r + 1] * e, axis=-1, keepdims=True)
                acc = acc + w * v[r:r + 1]
            parts.append(acc)
        o_c = jnp.concatenate(parts, axis=0)
        if readout:
            x = o_c + of_ref[rs, :]
            y = x * lax.rsqrt(jnp.mean(jnp.square(x), axis=-1, keepdims=True) + 1e-6) * nw_ref[...]
            o_ref[rs, :] = (y * jax.nn.silu(g_ref[rs, :])).astype(o_ref.dtype)
        else:
            o_ref[rs, :] = o_c.astype(o_ref.dtype)
        return st_new

    st = st_scr[...]
    for ci in range(n_chunks):
        st = chunk(ci, st)
    st_scr[...] = st

    @pl.when(step == pl.num_programs(2) - 1)
    def _():
        sfin_ref[0, 0] = st


def _hgrn2_scan_call(p, batch, lb, s0, reverse, z_group, readout_args=None):
    L = p.shape[0] // batch
    tb = min(HG_BLOCK, L)
    n_t = L // tb
    assert L % tb == 0 and tb % CHUNK == 0
    H = HG_HEADS

    def rows(b, h, t):
        return b * n_t + ((n_t - 1 - t) if reverse else t)

    def group(g):
        return pl.BlockSpec((tb, HG_DIM), lambda b, h, t: (rows(b, h, t), g * H + h))

    head_cols = pl.BlockSpec((1, HG_DIM), lambda b, h, t: (0, h))
    state = pl.BlockSpec((1, 1, HG_DIM, HG_DIM), lambda b, h, t: (b, h, 0, 0))
    out_rows = pl.BlockSpec((tb, HG_DIM), lambda b, h, t: (rows(b, h, t), h))
    in_specs = [group(0), group(z_group), group(3), head_cols, state]
    args = [p, p, p, lb, s0]
    readout = readout_args is not None
    if readout:
        norm_w, o_other = readout_args
        in_specs += [group(4), pl.BlockSpec((1, HG_DIM), lambda b, h, t: (0, 0)), out_rows]
        args += [p, norm_w.reshape(1, HG_DIM), o_other]
    return pl.pallas_call(
        functools.partial(_hgrn2_kernel, reverse=reverse, readout=readout),
        grid=(batch, H, n_t),
        in_specs=in_specs,
        out_specs=[out_rows, state],
        out_shape=[jax.ShapeDtypeStruct((batch * L, GROUP_WIDTH), jnp.bfloat16 if readout else jnp.float32),
                   jax.ShapeDtypeStruct((batch, H, HG_DIM, HG_DIM), jnp.float32)],
        scratch_shapes=[pltpu.VMEM((HG_DIM, HG_DIM), jnp.float32)],
        compiler_params=pltpu.CompilerParams(dimension_semantics=("arbitrary", "arbitrary", "arbitrary")),
        name="hgrn2_bwd" if reverse else "hgrn2_fwd",
    )(*args)


def _hgrn2_pallas(p_lat, p_ctx, batch, lb, norm_w, need_ctx):
    zero = jnp.zeros((batch, HG_HEADS, HG_DIM, HG_DIM), jnp.float32)
    lb_f, lb_b = lb[0:1], lb[1:2]
    oc_f, s_f = _hgrn2_scan_call(p_ctx, batch, lb_f, zero, False, 1)
    y_ctx, s_b = _hgrn2_scan_call(p_ctx, batch, lb_b, zero, True, 2, (norm_w, oc_f) if need_ctx else None)
    ol_f, _ = _hgrn2_scan_call(p_lat, batch, lb_f, s_f, False, 1)
    y_lat, _ = _hgrn2_scan_call(p_lat, batch, lb_b, s_b, True, 2, (norm_w, ol_f))
    return y_lat, (y_ctx if need_ctx else None)


def _hyena_filters(L, w1, b1, w2, b2, w3, freq):
    f32 = jnp.float32
    pos = jnp.arange(L, dtype=f32)
    t = pos / (L - 1)
    bands = jnp.linspace(1e-4, HY_BANDS - 1, HY_BANDS, dtype=f32)
    ang = (2.0 * math.pi / L) * pos[:, None] * bands[None, :]
    z = jnp.concatenate([t[:, None], jnp.cos(ang), jnp.sin(ang)], axis=-1)
    freq = freq.astype(f32)
    hid = jnp.sin(freq[0] * (z @ w1.astype(f32) + b1.astype(f32)))
    hid = jnp.sin(freq[1] * (hid @ w2.astype(f32) + b2.astype(f32)))
    h = (hid @ w3.astype(f32)).reshape(L, 2, HY_ORDER, HY_WIDTH)
    deltas = jnp.abs(jnp.linspace(HY_MIN_DECAY, HY_MAX_DECAY, HY_WIDTH, dtype=f32))
    h = h * jnp.exp(-t[:, None] * deltas)[:, None, None, :]
    fwd, bwd = h[:, 0], h[:, 1]
    bwd = bwd.at[0].set(0.0)
    norm = jnp.sum(jnp.abs(fwd), axis=0, keepdims=True) + jnp.sum(jnp.abs(bwd), axis=0, keepdims=True)
    return fwd / norm, bwd / norm


HY_FREQ_TILE = 384
HY_TIME_TILE = 256
HY_ANGLE_SPLIT = 64


def _dft_angle_tables(rows, cols, period):
    f32 = jnp.float32
    r = jnp.arange(rows, dtype=jnp.int32)[:, None]
    ch = jnp.arange(cols // HY_ANGLE_SPLIT, dtype=jnp.int32)[None, :] * HY_ANGLE_SPLIT
    cl = jnp.arange(HY_ANGLE_SPLIT, dtype=jnp.int32)[None, :]
    ah = ((r * ch) % period).astype(f32) * (2.0 * math.pi / period)
    al = ((r * cl) % period).astype(f32) * (2.0 * math.pi / period)
    ca, sa, cb, sb = jnp.cos(ah)[:, :, None], jnp.sin(ah)[:, :, None], jnp.cos(al)[:, None, :], jnp.sin(al)[:, None, :]
    return (ca * cb - sa * sb).reshape(rows, cols), (sa * cb + ca * sb).reshape(rows, cols)


def _hyena_dft_matrices(L):
    N = 2 * L
    mp = -(-(L + 1) // HY_FREQ_TILE) * HY_FREQ_TILE
    bf16 = jnp.bfloat16
    k = jnp.arange(mp)
    live = (k <= L).astype(jnp.float32)
    cf, sf = _dft_angle_tables(mp, L, N)
    ci, si = _dft_angle_tables(L, mp, N)
    w = jnp.where((k == 0) | (k == L), 1.0, 2.0) * live / N
    return ((cf * live[:, None]).astype(bf16), (sf * live[:, None]).astype(bf16),
            (ci * w[None, :]).astype(bf16), (-si * w[None, :]).astype(bf16))


def _dft_pair_kernel(c_ref, s_ref, a_ref, b_ref, oc_ref, os_ref):
    oc_ref[...] = jnp.dot(c_ref[...], a_ref[...], preferred_element_type=jnp.float32)
    os_ref[...] = jnp.dot(s_ref[...], b_ref[...], preferred_element_type=jnp.float32)


def _hyena_filter_spectrum(cf, sf, fwd, bwd):
    mp, L = cf.shape
    C = fwd.shape[1]
    bf16 = jnp.bfloat16
    col = lambda i: (0, 0)
    re, s = pl.pallas_call(
        _dft_pair_kernel,
        grid=(mp // HY_FREQ_TILE,),
        in_specs=[pl.BlockSpec((HY_FREQ_TILE, L), lambda i: (i, 0))] * 2 + [pl.BlockSpec((L, C), col)] * 2,
        out_specs=[pl.BlockSpec((HY_FREQ_TILE, C), lambda i: (i, 0))] * 2,
        out_shape=[jax.ShapeDtypeStruct((mp, C), jnp.float32)] * 2,
        compiler_params=pltpu.CompilerParams(vmem_limit_bytes=48 * 1024 * 1024),
        name="hyena_filter_spectrum",
    )(cf, sf, (fwd + bwd).astype(bf16), (fwd - bwd).astype(bf16))
    return re, -s


def _hyena_fwd_kernel(c_ref, s_ref, z_ref, kre_ref, kim_ref, pre_ref, pim_ref):
    f32 = jnp.float32
    xc = jnp.dot(c_ref[...], z_ref[0], preferred_element_type=f32)
    xs = jnp.dot(s_ref[...], z_ref[0], preferred_element_type=f32)
    pre_ref[0] = (xc * kre_ref[...] + xs * kim_ref[...]).astype(pre_ref.dtype)
    pim_ref[0] = (xc * kim_ref[...] - xs * kre_ref[...]).astype(pim_ref.dtype)


def _hyena_inv_kernel(ci_ref, si_ref, pre_ref, pim_ref, z_ref, d_ref, m_ref, o_ref, ob_ref):
    f32 = jnp.float32
    y = (jnp.dot(ci_ref[...], pre_ref[0], preferred_element_type=f32)
         + jnp.dot(si_ref[...], pim_ref[0], preferred_element_type=f32))
    out = m_ref[0] * (y + z_ref[0] * d_ref[...])
    o_ref[0] = out
    ob_ref[0] = out.astype(ob_ref.dtype)


def _hyena_long_conv(mats, z, z_b, kre, kim, d, mult):
    cf, sf, ci, si = mats
    B, L, C = z.shape
    mp = cf.shape[0]
    tt = min(HY_TIME_TILE, L)
    bf16 = jnp.bfloat16
    pre, pim = pl.pallas_call(
        _hyena_fwd_kernel,
        grid=(mp // HY_FREQ_TILE, B),
        in_specs=[pl.BlockSpec((HY_FREQ_TILE, L), lambda i, b: (i, 0))] * 2
                 + [pl.BlockSpec((1, L, C), lambda i, b: (b, 0, 0))]
                 + [pl.BlockSpec((HY_FREQ_TILE, C), lambda i, b: (i, 0))] * 2,
        out_specs=[pl.BlockSpec((1, HY_FREQ_TILE, C), lambda i, b: (b, i, 0))] * 2,
        out_shape=[jax.ShapeDtypeStruct((B, mp, C), bf16)] * 2,
        compiler_params=pltpu.CompilerParams(vmem_limit_bytes=48 * 1024 * 1024),
        name="hyena_fwd",
    )(cf, sf, z_b, kre, kim)
    row = lambda b, t: (b, t, 0)
    return pl.pallas_call(
        _hyena_inv_kernel,
        grid=(B, L // tt),
        in_specs=[pl.BlockSpec((tt, mp), lambda b, t: (t, 0))] * 2
                 + [pl.BlockSpec((1, mp, C), lambda b, t: (b, 0, 0))] * 2
                 + [pl.BlockSpec((1, tt, C), row), pl.BlockSpec((1, C), lambda b, t: (0, 0)),
                    pl.BlockSpec((1, tt, C), row)],
        out_specs=[pl.BlockSpec((1, tt, C), row)] * 2,
        out_shape=[jax.ShapeDtypeStruct((B, L, C), jnp.float32), jax.ShapeDtypeStruct((B, L, C), bf16)],
        compiler_params=pltpu.CompilerParams(vmem_limit_bytes=48 * 1024 * 1024),
        name="hyena_inv",
    )(ci, si, pre, pim, z, d.reshape(1, C).astype(jnp.float32), mult)


def _hyena_seq(p, mats, conv_w, conv_b, w1, b1, w2, b2, w3, freq, dbias):
    u = _centred_dwconv(p, conv_w, conv_b).astype(jnp.float32)
    v, x1, x2 = jnp.split(u, 3, axis=-1)
    L = p.shape[1]
    fwd, bwd = _hyena_filters(L, w1, b1, w2, b2, w3, freq)
    kre, kim = _hyena_filter_spectrum(mats[0], mats[1], fwd.reshape(L, -1), bwd.reshape(L, -1))
    W = HY_WIDTH
    z, z_b = _hyena_long_conv(mats, v, v.astype(jnp.bfloat16), kre[:, :W], kim[:, :W], dbias[0], x1)
    return _hyena_long_conv(mats, z, z_b, kre[:, W:], kim[:, W:], dbias[1], x2)[1]


def _na_pair_attention(q_pair, key_sets, lane):
    f32 = jnp.float32
    nt = (((1,), (1,)), ((), ()))
    outs = []
    for half in range(2):
        own = (lane >= NA_DIM) if half else (lane < NA_DIM)
        q = jnp.where(own, q_pair, jnp.zeros_like(q_pair))
        scores = []
        for k_pair, _, add, valid in key_sets:
            s = lax.dot_general(q, k_pair, nt, preferred_element_type=f32) * (NA_DIM ** -0.5)
            if add is not None:
                s = jnp.where(valid, s + add[half], MASK_VALUE)
            scores.append(s)
        m = scores[0].max(axis=-1, keepdims=True)
        for s in scores[1:]:
            m = jnp.maximum(m, s.max(axis=-1, keepdims=True))
        den = jnp.zeros_like(m)
        acc = jnp.zeros(q_pair.shape, f32)
        for s, (_, v_pair, _, _) in zip(scores, key_sets):
            p = jnp.exp(s - m)
            den = den + p.sum(axis=-1, keepdims=True)
            acc = acc + jnp.dot(p.astype(jnp.bfloat16), v_pair, preferred_element_type=f32)
        outs.append(acc / den)
    return jnp.where(lane < NA_DIM, outs[0], outs[1])


def _natten_lat_kernel(q_ref, k_ref, v_ref, kc_ref, vc_ref, bias_ref, o_ref, *, rows):
    kh = NA_WIN_ROWS
    r = pl.program_id(1)
    start = pl.multiple_of(jnp.clip(r - kh // 2, 0, rows - kh) * GRID_W, GRID_W)
    nk = kh * GRID_W
    qcol = lax.broadcasted_iota(jnp.int32, (GRID_W, nk), 0)
    kcol = lax.broadcasted_iota(jnp.int32, (GRID_W, nk), 1) % GRID_W
    col_start = jnp.clip(qcol - NA_WIN_COLS // 2, 0, GRID_W - NA_WIN_COLS)
    valid = (kcol >= col_start) & (kcol < col_start + NA_WIN_COLS)
    lane = lax.broadcasted_iota(jnp.int32, (GRID_W, 2 * NA_DIM), 1)
    for hp in range(NA_HEADS // 2):
        cs = slice(hp * 2 * NA_DIM, (hp + 1) * 2 * NA_DIM)
        win = (k_ref[pl.ds(start, nk), cs], v_ref[pl.ds(start, nk), cs],
               (bias_ref[0, 2 * hp], bias_ref[0, 2 * hp + 1]), valid)
        ctx = (kc_ref[:, cs], vc_ref[:, cs], None, None)
        o_ref[:, cs] = _na_pair_attention(q_ref[:, cs], [win, ctx], lane).astype(o_ref.dtype)


def _natten_ctx_kernel(q_ref, k_ref, v_ref, o_ref):
    lane = lax.broadcasted_iota(jnp.int32, (q_ref.shape[0], 2 * NA_DIM), 1)
    for hp in range(NA_HEADS // 2):
        cs = slice(hp * 2 * NA_DIM, (hp + 1) * 2 * NA_DIM)
        o_ref[:, cs] = _na_pair_attention(q_ref[:, cs], [(k_ref[:, cs], v_ref[:, cs], None, None)],
                                          lane).astype(o_ref.dtype)


def _natten_bias_table(rpb, rows):
    kh = NA_WIN_ROWS
    dc = jnp.clip(jnp.arange(GRID_W)[None, :] - jnp.arange(GRID_W)[:, None], 1 - NA_WIN_COLS, NA_WIN_COLS - 1)
    t = rpb.astype(jnp.float32)[:, :, dc + NA_WIN_COLS - 1]
    per_off = [t[:, d:d + kh].transpose(0, 2, 1, 3).reshape(NA_HEADS, GRID_W, kh * GRID_W) for d in range(kh)]
    return jnp.stack(per_off, axis=0)


def _natten_pallas(p_lat, p_ctx, rpb, need_ctx, batch):
    W = GROUP_WIDTH
    L = p_lat.shape[0] // batch
    n_ctx = p_ctx.shape[0] // batch
    rows = L // GRID_W
    kh = NA_WIN_ROWS
    assert rows >= kh and L % GRID_W == 0
    bias = _natten_bias_table(rpb, rows)

    def first_key_row_offset(b, r):
        return (jnp.clip(r - kh // 2, 0, rows - kh) - r + kh - 1, 0, 0, 0)

    y_lat = pl.pallas_call(
        functools.partial(_natten_lat_kernel, rows=rows),
        grid=(batch, rows),
        in_specs=[pl.BlockSpec((GRID_W, W), lambda b, r: (b * rows + r, 0)),
                  pl.BlockSpec((L, W), lambda b, r: (b, 1)),
                  pl.BlockSpec((L, W), lambda b, r: (b, 2)),
                  pl.BlockSpec((n_ctx, W), lambda b, r: (b, 1)),
                  pl.BlockSpec((n_ctx, W), lambda b, r: (b, 2)),
                  pl.BlockSpec((1, NA_HEADS, GRID_W, kh * GRID_W), first_key_row_offset)],
        out_specs=pl.BlockSpec((GRID_W, W), lambda b, r: (b * rows + r, 0)),
        out_shape=jax.ShapeDtypeStruct((batch * L, W), jnp.bfloat16),
        compiler_params=pltpu.CompilerParams(vmem_limit_bytes=40 * 1024 * 1024),
        name="natten_lat",
    )(p_lat, p_lat, p_lat, p_ctx, p_ctx, bias)
    y_ctx = None
    if need_ctx:
        y_ctx = pl.pallas_call(
            _natten_ctx_kernel,
            grid=(batch,),
            in_specs=[pl.BlockSpec((n_ctx, W), lambda b: (b, 0)),
                      pl.BlockSpec((n_ctx, W), lambda b: (b, 1)),
                      pl.BlockSpec((n_ctx, W), lambda b: (b, 2))],
            out_specs=pl.BlockSpec((n_ctx, W), lambda b: (b, 0)),
            out_shape=jax.ShapeDtypeStruct((batch * n_ctx, W), jnp.bfloat16),
            name="natten_ctx",
        )(p_ctx, p_ctx, p_ctx)
    return y_lat, y_ctx


ML_BLOCK = 256
ML_GATES = 4 * ML_HEADS


def _mlstm_kernel(*refs, reverse, readout):
    if readout:
        (q_ref, k_ref, v_ref, gc_ref, gr_ref, c0_ref, n0_ref, m0_ref, og_ref, hf_ref,
         o_ref, cfin_ref, nfin_ref, mfin_ref, c_scr, n_scr, m_scr) = refs
    else:
        (q_ref, k_ref, v_ref, gc_ref, gr_ref, c0_ref, n0_ref, m0_ref,
         o_ref, cfin_ref, nfin_ref, mfin_ref, c_scr, n_scr, m_scr) = refs
    f32, bf16 = jnp.float32, jnp.bfloat16
    C = CHUNK
    n_chunks = q_ref.shape[0] // C
    head = pl.program_id(1)
    step = pl.program_id(2)

    @pl.when(step == 0)
    def _():
        c_scr[...] = c0_ref[0, 0]
        n_scr[...] = n0_ref[0, 0]
        m_scr[...] = m0_ref[0, 0]

    nt = (((1,), (1,)), ((), ()))
    tn = (((0,), (0,)), ((), ()))
    r_i = lax.broadcasted_iota(jnp.int32, (C, C), 0)
    c_i = lax.broadcasted_iota(jnp.int32, (C, C), 1)
    seen = (c_i >= r_i) if reverse else (c_i <= r_i)
    tri = jnp.where(seen, 1.0, 0.0).astype(bf16)
    tri_t = jnp.where((r_i >= c_i) if reverse else (r_i <= c_i), 1.0, 0.0).astype(bf16)
    ig_idx = (2 if reverse else 0) * ML_HEADS + head
    fg_idx = ig_idx + ML_HEADS
    lane = lax.broadcasted_iota(jnp.int32, (C, ML_GATES), 1)
    sub = lax.broadcasted_iota(jnp.int32, (ML_GATES, C), 0)

    def chunk(ci, state):
        ckv, n, m = state
        c = (n_chunks - 1 - ci) if reverse else ci
        rs = slice(c * C, (c + 1) * C)
        q, k, v = q_ref[rs, :], k_ref[rs, :], v_ref[rs, :]
        gates = gc_ref[rs, :]
        lf_cols = jax.nn.log_sigmoid(gates)
        g_cols = _masked_sums(tri, lf_cols)
        g_c = jnp.sum(jnp.where(lane == fg_idx, g_cols, 0.0), axis=1, keepdims=True)
        ig_c = jnp.sum(jnp.where(lane == ig_idx, gates, 0.0), axis=1, keepdims=True)
        gates_r = gr_ref[:, rs]
        ig_r = jnp.sum(jnp.where(sub == ig_idx, gates_r, 0.0), axis=0, keepdims=True)
        lf_r = jnp.sum(jnp.where(sub == fg_idx, jax.nn.log_sigmoid(gates_r), 0.0), axis=0, keepdims=True)
        g_r = sum(jnp.dot(t, tri_t, preferred_element_type=f32) for t in _bf16_terms(lf_r))
        m1 = m[:, 0:1]
        logd = jnp.where(seen, g_c - g_r + ig_r, MASK_VALUE)
        log_inter = g_c + m1
        m_out = jnp.maximum(log_inter, jnp.max(logd, axis=1, keepdims=True))
        dmat = jnp.exp(logd - m_out)
        w_inter = jnp.exp(log_inter - m_out)
        qb = q.astype(bf16)
        sc = lax.dot_general(qb, k.astype(bf16), nt, preferred_element_type=f32) * dmat
        num = (jnp.dot(sc.astype(bf16), v.astype(bf16), preferred_element_type=f32)
               + w_inter * jnp.dot(qb, ckv.astype(bf16), preferred_element_type=f32))
        den = jnp.sum(sc, axis=1, keepdims=True) + w_inter * jnp.sum(q * n, axis=1, keepdims=True)
        h = num / jnp.maximum(jnp.abs(den), jnp.exp(-m_out))
        g_end = g_c[0:1] if reverse else g_c[C - 1:C]
        a = g_end - g_c + ig_c
        m_new = jnp.maximum(g_end + m1, jnp.max(a, axis=0, keepdims=True))
        carry_w = jnp.exp(g_end + m1 - m_new)
        wk = jnp.exp(a - m_new) * k
        ckv_new = carry_w * ckv + lax.dot_general(wk.astype(bf16), v.astype(bf16), tn, preferred_element_type=f32)
        n_new = carry_w * n + jnp.sum(wk, axis=0, keepdims=True)
        if readout:
            o_ref[rs, :] = (jax.nn.sigmoid(og_ref[rs, :]) * (h + hf_ref[rs, :])).astype(o_ref.dtype)
        else:
            o_ref[rs, :] = h.astype(o_ref.dtype)
        return ckv_new, n_new, jnp.broadcast_to(m_new, m.shape)

    state = (c_scr[...], n_scr[...], m_scr[...])
    for ci in range(n_chunks):
        state = chunk(ci, state)
    c_scr[...], n_scr[...], m_scr[...] = state

    @pl.when(step == pl.num_programs(2) - 1)
    def _():
        cfin_ref[0, 0], nfin_ref[0, 0], mfin_ref[0, 0] = state


def _mlstm_scan_call(q, k, v_src, v_col0, gates, gates_t, batch, state0, reverse, readout_args=None):
    L = q.shape[0] // batch
    tb = min(ML_BLOCK, L)
    n_t = L // tb
    assert L % tb == 0 and tb % CHUNK == 0 and tb % 128 == 0
    H = ML_HEADS

    def rows(b, h, t):
        return b * n_t + ((n_t - 1 - t) if reverse else t)

    head_block = pl.BlockSpec((tb, ML_DIM), lambda b, h, t: (rows(b, h, t), h))
    mat_state = pl.BlockSpec((1, 1, ML_DIM, ML_DIM), lambda b, h, t: (b, h, 0, 0))
    vec_state = pl.BlockSpec((1, 1, 1, ML_DIM), lambda b, h, t: (b, h, 0, 0))
    in_specs = [head_block, head_block,
                pl.BlockSpec((tb, ML_DIM), lambda b, h, t: (rows(b, h, t), v_col0 + h)),
                pl.BlockSpec((tb, ML_GATES), lambda b, h, t: (rows(b, h, t), 0)),
                pl.BlockSpec((ML_GATES, tb), lambda b, h, t: (0, rows(b, h, t))),
                mat_state, vec_state, vec_state]
    args = [q, k, v_src, gates, gates_t, *state0]
    readout = readout_args is not None
    if readout:
        og_src, og_col0, h_other = readout_args
        in_specs += [pl.BlockSpec((tb, ML_DIM), lambda b, h, t: (rows(b, h, t), og_col0 + h)), head_block]
        args += [og_src, h_other]
    outs = pl.pallas_call(
        functools.partial(_mlstm_kernel, reverse=reverse, readout=readout),
        grid=(batch, H, n_t),
        in_specs=in_specs,
        out_specs=[head_block, mat_state, vec_state, vec_state],
        out_shape=[jax.ShapeDtypeStruct((batch * L, GROUP_WIDTH), jnp.bfloat16 if readout else jnp.float32),
                   jax.ShapeDtypeStruct((batch, H, ML_DIM, ML_DIM), jnp.float32),
                   jax.ShapeDtypeStruct((batch, H, 1, ML_DIM), jnp.float32),
                   jax.ShapeDtypeStruct((batch, H, 1, ML_DIM), jnp.float32)],
        scratch_shapes=[pltpu.VMEM((ML_DIM, ML_DIM), jnp.float32), pltpu.VMEM((1, ML_DIM), jnp.float32),
                        pltpu.VMEM((1, ML_DIM), jnp.float32)],
        compiler_params=pltpu.CompilerParams(dimension_semantics=("arbitrary", "arbitrary", "arbitrary")),
        name="mlstm_bwd" if reverse else "mlstm_fwd",
    )(*args)
    return outs[0], tuple(outs[1:])


def _mlstm_pallas(pm_lat, pm_ctx, batch, conv_w, conv_b, gate_b, need_ctx):
    W = GROUP_WIDTH
    f32 = jnp.float32

    def prep(pm, rope):
        L = pm.shape[0] // batch
        qk = jax.nn.silu(_centred_dwconv(pm[:, :2 * W].reshape(batch, L, 2 * W), conv_w, conv_b)).astype(f32)
        q = qk[..., :W].reshape(batch, L, ML_HEADS, ML_DIM)
        k = qk[..., W:].reshape(batch, L, ML_HEADS, ML_DIM)
        if rope:
            q, k = _axial_rope(q), _axial_rope(k)
        gates = pm[:, 4 * W:] + gate_b.astype(f32)
        return q.reshape(batch * L, W), (k * ML_DIM ** -0.5).reshape(batch * L, W), gates, gates.T

    zero = (jnp.zeros((batch, ML_HEADS, ML_DIM, ML_DIM), f32), jnp.zeros((batch, ML_HEADS, 1, ML_DIM), f32),
            jnp.zeros((batch, ML_HEADS, 1, ML_DIM), f32))
    v0, o0 = 2 * W // ML_DIM, 3 * W // ML_DIM
    qc, kc, gc, gct = prep(pm_ctx, False)
    hc_f, st_f = _mlstm_scan_call(qc, kc, pm_ctx, v0, gc, gct, batch, zero, False)
    y_ctx, st_b = _mlstm_scan_call(qc, kc, pm_ctx, v0, gc, gct, batch, zero, True,
                                   (pm_ctx, o0, hc_f) if need_ctx else None)
    ql, kl, gl, glt = prep(pm_lat, True)
    hl_f, _ = _mlstm_scan_call(ql, kl, pm_lat, v0, gl, glt, batch, st_f, False)
    y_lat, _ = _mlstm_scan_call(ql, kl, pm_lat, v0, gl, glt, batch, st_b, True, (pm_lat, o0, hl_f))
    return y_lat, (y_ctx if need_ctx else None)


PEER_SUB_DK = PEER_DK // 2
PEER_CAND_ROWS = 16 + 8 + 6 * 8 + 8


def _topk_rows(s, order, payload, k):
    sentinel = jnp.int32(2 ** 30)
    vals, picked = [], []
    for _ in range(k):
        m = s.max(axis=0, keepdims=True)
        first = jnp.min(jnp.where(s == m, order, sentinel), axis=0, keepdims=True)
        hit = order == first
        vals.append(m)
        picked.append(first if payload is None else jnp.max(jnp.where(hit, payload, -1), axis=0, keepdims=True))
        s = jnp.where(hit, -jnp.inf, s)
    return jnp.concatenate(vals, axis=0), jnp.concatenate(picked, axis=0)


def _peer_candidates(a0, a1, combine):
    pieces = [combine(a0[0:1], a1), combine(a0[1:2], a1[0:8])]
    pieces += [combine(a0[i:i + 1], a1[0:8]) for i in range(2, 8)]
    pieces.append(combine(a0[8:16], a1[0:1]))
    return jnp.concatenate(pieces, axis=0)


def _peer_topk_kernel(x_ref, sc_ref, sh_ref, wq_ref, sk_ref, h_ref, ids_ref, gate_ref):
    f32 = jnp.float32
    tb = x_ref.shape[0]
    h = x_ref[...] * sc_ref[0] + sh_ref[0]
    h_ref[...] = h.astype(h_ref.dtype)
    q = jnp.dot(h.astype(jnp.bfloat16), wq_ref[...], preferred_element_type=f32).astype(jnp.bfloat16)
    key_order = lax.broadcasted_iota(jnp.int32, (PEER_NKEYS, tb), 0)
    rho = lax.broadcasted_iota(jnp.int32, (PEER_CAND_ROWS, tb), 0)
    ci = jnp.where(rho < 16, 0, jnp.where(rho < 24, 1, jnp.where(rho < 72, 2 + ((rho - 24) >> 3), rho - 64)))
    cj = jnp.where(rho < 16, rho, jnp.where(rho < 24, rho - 16, jnp.where(rho < 72, (rho - 24) & 7, 0)))
    cand_order = ci * PEER_TOPK + cj
    cand_valid = (ci + 1) * (cj + 1) <= PEER_TOPK
    nt = (((1,), (1,)), ((), ()))
    for head in range(PEER_HEADS):
        sv, si = [], []
        for half in range(2):
            c0 = (head * 2 + half) * PEER_SUB_DK
            s_t = lax.dot_general(sk_ref[head * 2 + half], q[:, c0:c0 + PEER_SUB_DK], nt,
                                  preferred_element_type=f32)
            v, i = _topk_rows(s_t, key_order, None, PEER_TOPK)
            sv.append(v)
            si.append(i)
        cand_s = _peer_candidates(sv[0], sv[1], lambda a, b: a + b)
        cand_s = jnp.where(cand_valid, cand_s, -jnp.inf)
        cand_id = _peer_candidates(si[0], si[1], lambda a, b: a * PEER_NKEYS + b)
        best_s, ids = _topk_rows(cand_s, cand_order, cand_id, PEER_TOPK)
        e = jnp.exp(best_s - best_s[0:1])
        rs = slice(head * PEER_TOPK, (head + 1) * PEER_TOPK)
        ids_ref[rs, :] = ids
        gate_ref[rs, :] = e / e.sum(axis=0, keepdims=True)


def _peer_route(x2d, scale, shift, rows_per_mod, wq, subkeys, tb=256):
    T, D = x2d.shape
    nk = PEER_HEADS * PEER_TOPK
    sk = subkeys.reshape(PEER_HEADS * 2, PEER_NKEYS, PEER_SUB_DK).astype(jnp.bfloat16)
    mod = lambda i: (i * tb // rows_per_mod, 0, 0)
    h, ids_t, gate_t = pl.pallas_call(
        _peer_topk_kernel,
        grid=(T // tb,),
        in_specs=[pl.BlockSpec((tb, D), lambda i: (i, 0)),
                  pl.BlockSpec((1, 1, D), mod),
                  pl.BlockSpec((1, 1, D), mod),
                  pl.BlockSpec(wq.shape, lambda i: (0, 0)),
                  pl.BlockSpec(sk.shape, lambda i: (0, 0, 0))],
        out_specs=[pl.BlockSpec((tb, D), lambda i: (i, 0)),
                   pl.BlockSpec((nk, tb), lambda i: (0, i)),
                   pl.BlockSpec((nk, tb), lambda i: (0, i))],
        out_shape=[jax.ShapeDtypeStruct((T, D), jnp.bfloat16),
                   jax.ShapeDtypeStruct((nk, T), jnp.int32),
                   jax.ShapeDtypeStruct((nk, T), jnp.float32)],
        compiler_params=pltpu.CompilerParams(vmem_limit_bytes=48 * 1024 * 1024),
        name="peer_route",
    )(x2d, scale, shift, wq.astype(jnp.bfloat16), sk)
    return h, ids_t.T, gate_t.T


PEER_MASK_SLAB = PEER_NKEYS + 8
PEER_MASK_UNROLL = 8
PEER_KEYS_PER_STEP = 8
PEER_TOKEN_TILE = 512


def _bf16_bits(x):
    u = lax.bitcast_convert_type(x, jnp.uint32)
    return (u + jnp.uint32(0x7FFF) + ((u >> 16) & jnp.uint32(1))) >> 16


def _peer_expert_kernel(x_ref, ids_ref, gate_ref, u_ref, v_ref, o_ref, w_scr):
    f32, bf16 = jnp.float32, jnp.bfloat16
    j = pl.program_id(1)
    tm = x_ref.shape[0]
    half = tm // 2
    nt = (((1,), (1,)), ((), ()))

    @pl.when(j == 0)
    def _():
        o_ref[...] = jnp.zeros_like(o_ref)
        key_iota = lax.broadcasted_iota(jnp.int32, (PEER_NKEYS, ids_ref.shape[1]), 0)

        def token_mask(t):
            ids = ids_ref[pl.ds(t, 1), :]
            g = gate_ref[pl.ds(t, 1), :]
            first = jnp.where((ids >> 7) == key_iota, 1.0, 0.0).astype(bf16)
            second = jnp.where((ids & (PEER_NKEYS - 1)) == key_iota, g, 0.0).astype(bf16)
            return lax.dot_general(first, second, nt, preferred_element_type=f32)

        def pair_group(i, carry):
            for s in range(PEER_MASK_UNROLL // 2):
                p = i * (PEER_MASK_UNROLL // 2) + s
                packed = _bf16_bits(token_mask(p)) | (_bf16_bits(token_mask(p + half)) << 16)
                w_scr[pl.ds(pl.multiple_of(p * PEER_MASK_SLAB, 8), PEER_NKEYS), :] = packed
            return carry

        lax.fori_loop(0, half // (PEER_MASK_UNROLL // 2), pair_group, 0)

    a = lax.dot_general(x_ref[...], u_ref[...], nt, preferred_element_type=f32)
    cols = []
    for s in range(PEER_KEYS_PER_STEP):
        packed = w_scr[pl.ds(j * PEER_KEYS_PER_STEP + s, half, stride=PEER_MASK_SLAB), :]
        lo = lax.bitcast_convert_type(packed << 16, f32)
        hi = lax.bitcast_convert_type(packed & jnp.uint32(0xFFFF0000), f32)
        cols.append(jnp.concatenate([lo, hi], axis=0))
    w = jnp.concatenate(cols, axis=1)
    gelu = 0.5 * a * (1.0 + lax.erf(a * (2.0 ** -0.5)))
    c = (w * gelu).astype(bf16)
    o_ref[...] += jnp.dot(c, v_ref[...], preferred_element_type=f32)


def _peer_expert(tok, ids, gate, u, v):
    T, D = tok.shape
    tm = min(PEER_TOKEN_TILE, T)
    te = PEER_KEYS_PER_STEP * PEER_NKEYS
    nk = ids.shape[1]
    assert PEER_NKEYS == 128 and T % tm == 0 and PEER_EXPERTS % te == 0 and (tm // 2) % PEER_MASK_UNROLL == 0
    return pl.pallas_call(
        _peer_expert_kernel,
        grid=(T // tm, PEER_EXPERTS // te),
        in_specs=[pl.BlockSpec((tm, D), lambda i, j: (i, 0)),
                  pl.BlockSpec((tm, nk), lambda i, j: (i, 0)),
                  pl.BlockSpec((tm, nk), lambda i, j: (i, 0)),
                  pl.BlockSpec((te, D), lambda i, j: (j, 0)),
                  pl.BlockSpec((te, D), lambda i, j: (j, 0))],
        out_specs=pl.BlockSpec((tm, D), lambda i, j: (i, 0)),
        out_shape=jax.ShapeDtypeStruct((T, D), jnp.float32),
        scratch_shapes=[pltpu.VMEM((tm // 2 * PEER_MASK_SLAB, PEER_NKEYS), jnp.uint32)],
        compiler_params=pltpu.CompilerParams(dimension_semantics=("arbitrary", "arbitrary"),
                                             vmem_limit_bytes=56 * 1024 * 1024),
        name="peer_expert",
    )(tok, ids, gate, u, v)


def _peer(x2d, scale, shift, rows_per_mod, wq, subkeys, u, v):
    tok, ids, gate = _peer_route(x2d, scale, shift, rows_per_mod, wq, subkeys)
    return _peer_expert(tok, ids, gate, u, v)


def _in_proj_kernel(x_ref, sc_ref, sh_ref, w_ref, o_ref, a_scr):
    @pl.when(pl.program_id(1) == 0)
    def _():
        a_scr[...] = (x_ref[...] * sc_ref[0] + sh_ref[0]).astype(a_scr.dtype)

    o_ref[...] = jnp.dot(a_scr[...], w_ref[...], preferred_element_type=jnp.float32).astype(o_ref.dtype)


def _in_proj(x2d, scale, shift, rows_per_mod, w, col0, ncols, out_dtype, tm=1024, tn=512):
    M, K = x2d.shape
    tm = min(tm, M)
    assert col0 % tn == 0 and M % tm == 0
    mod = lambda i, j: (i * tm // rows_per_mod, 0, 0)
    return pl.pallas_call(
        _in_proj_kernel,
        grid=(M // tm, pl.cdiv(ncols, tn)),
        in_specs=[pl.BlockSpec((tm, K), lambda i, j: (i, 0)),
                  pl.BlockSpec((1, 1, K), mod),
                  pl.BlockSpec((1, 1, K), mod),
                  pl.BlockSpec((K, tn), lambda i, j: (0, col0 // tn + j))],
        out_specs=pl.BlockSpec((tm, tn), lambda i, j: (i, j)),
        out_shape=jax.ShapeDtypeStruct((M, ncols), out_dtype),
        scratch_shapes=[pltpu.VMEM((tm, K), jnp.bfloat16)],
        compiler_params=pltpu.CompilerParams(dimension_semantics=("arbitrary", "arbitrary"),
                                             vmem_limit_bytes=48 * 1024 * 1024),
        name="in_proj",
    )(x2d, scale, shift, w)


def _out_proj_ln_kernel(a_ref, b_ref, c_ref, d_ref, w_ref, x_ref, gt_ref, g_ref, bt_ref, o_ref):
    f32 = jnp.float32
    W = GROUP_WIDTH
    y = jnp.zeros(x_ref.shape, f32)
    for g, r in enumerate((a_ref, b_ref, c_ref, d_ref)):
        y = y + jnp.dot(r[...].astype(jnp.bfloat16), w_ref[g * W:(g + 1) * W, :], preferred_element_type=f32)
    z = DEEPNORM_ALPHA * x_ref[...] + gt_ref[0] * y
    mu = jnp.mean(z, axis=-1, keepdims=True)
    var = jnp.mean(jnp.square(z - mu), axis=-1, keepdims=True)
    o_ref[...] = (z - mu) * lax.rsqrt(var + 1e-5) * g_ref[...] + bt_ref[...]


def _out_proj_ln(parts, w, x2d, gate, rows_per_mod, ln_g, ln_b, tm=256):
    M, D = x2d.shape
    W = GROUP_WIDTH
    tm = min(tm, M)
    row = lambda i: (i, 0)
    return pl.pallas_call(
        _out_proj_ln_kernel,
        grid=(M // tm,),
        in_specs=[pl.BlockSpec((tm, W), row)] * 4
                 + [pl.BlockSpec(w.shape, lambda i: (0, 0)),
                    pl.BlockSpec((tm, D), row),
                    pl.BlockSpec((1, 1, D), lambda i: (i * tm // rows_per_mod, 0, 0)),
                    pl.BlockSpec((1, D), lambda i: (0, 0)),
                    pl.BlockSpec((1, D), lambda i: (0, 0))],
        out_specs=pl.BlockSpec((tm, D), row),
        out_shape=jax.ShapeDtypeStruct((M, D), jnp.float32),
        compiler_params=pltpu.CompilerParams(vmem_limit_bytes=48 * 1024 * 1024),
        name="out_proj_ln",
    )(*parts, w, x2d, gate, ln_g.reshape(1, D), ln_b.reshape(1, D))


def kernel(x, c, ctx, c_ctx, w_ada, b_ada, w_in, w_out, ln_g, ln_b, hg_lower_bounds, hg_norm_w,
           hy_conv_w, hy_conv_b, hy_w1, hy_b1, hy_w2, hy_b2, hy_w3, hy_freq, hy_dbias, na_rpb,
           ml_conv_w, ml_conv_b, ml_gate_b, peer_wq, peer_subkeys, peer_u, peer_v):
    f32, bf16 = jnp.float32, jnp.bfloat16
    B, L, D = x.shape
    n_ctx = ctx.shape[1]
    lb_soft = jax.nn.softmax(hg_lower_bounds.astype(f32), axis=0)
    lower_bounds = jnp.cumsum(lb_soft, axis=0) - lb_soft[0]
    cond_lat = jax.nn.silu(c)
    cond_ctx = jax.nn.silu(c_ctx)
    c0_na = HG_COLS + HY_COLS
    c0_ml = c0_na + NA_COLS
    x2 = x.reshape(B * L, D)
    dft_lat = _hyena_dft_matrices(L)
    xc2 = ctx.reshape(B * n_ctx, D)
    for l in range(DEPTH):
        need_ctx = l < DEPTH - 1
        m_lat = (cond_lat @ w_ada[l] + b_ada[l]).reshape(B, 1, 6, D)
        m_ctx = (cond_ctx @ w_ada[l] + b_ada[l]).reshape(1, 1, 6, D)
        ml = [m_lat[:, :, k] for k in range(6)]
        mc = [m_ctx[:, :, k] for k in range(6)]
        w_in_l = w_in[l].astype(bf16)
        w_out_l = w_out[l].astype(bf16)

        def in_proj(t, m, rows_per_mod):
            sc, sh = 1 + m[1], m[0]
            return (_in_proj(t, sc, sh, rows_per_mod, w_in_l, 0, c0_na, f32),
                    _in_proj(t, sc, sh, rows_per_mod, w_in_l, c0_na, NA_COLS, bf16),
                    _in_proj(t, sc, sh, rows_per_mod, w_in_l, c0_ml, ML_COLS, f32))

        ph_l, pn_l, pm_l = in_proj(x2, ml, L)
        ph_c, pn_c, pm_c = in_proj(xc2, mc, B * n_ctx)
        pb_l = ph_l[:, HG_COLS:].reshape(B, L, -1)
        pb_c = ph_c[:, HG_COLS:].reshape(B, n_ctx, -1)
        hy = (hy_conv_w[l], hy_conv_b[l], hy_w1[l], hy_b1[l], hy_w2[l], hy_b2[l], hy_w3[l], hy_freq[l], hy_dbias[l])
        a_l, a_c = _hgrn2_pallas(ph_l, ph_c, B, lower_bounds[l], hg_norm_w[l], need_ctx)
        b_l = _hyena_seq(pb_l, dft_lat, *hy)
        c_l, c_c = _natten_pallas(pn_l, pn_c, na_rpb[l], need_ctx, B)
        d_l, d_c = _mlstm_pallas(pm_l, pm_c, B, ml_conv_w[l], ml_conv_b[l], ml_gate_b[l], need_ctx)
        parts_l = (a_l, b_l.reshape(B * L, -1), c_l, d_l)
        if need_ctx:
            b_c = _hyena_seq(pb_c, _hyena_dft_matrices(n_ctx), *hy)
            parts_c = (a_c, b_c.reshape(B * n_ctx, -1), c_c, d_c)
            xc2 = _out_proj_ln(parts_c, w_out_l, xc2, mc[2], B * n_ctx, ln_g[l, 0], ln_b[l, 0])
        x2 = _out_proj_ln(parts_l, w_out_l, x2, ml[2], L, ln_g[l, 0], ln_b[l, 0])
        u_l, v_l = peer_u[l].astype(bf16), peer_v[l].astype(bf16)
        y_lat = _peer(x2, 1 + ml[4], ml[3], L, peer_wq[l], peer_subkeys[l], u_l, v_l)
        x2 = _layer_norm(DEEPNORM_ALPHA * x2.reshape(B, L, D) + ml[5] * y_lat.reshape(B, L, D),
                         ln_g[l, 1], ln_b[l, 1]).reshape(B * L, D)
        if need_ctx:
            y_ctx = _peer(xc2, 1 + mc[4], mc[3], B * n_ctx, peer_wq[l], peer_subkeys[l], u_l, v_l)
            xc2 = _layer_norm(DEEPNORM_ALPHA * xc2 + mc[5][0] * y_ctx, ln_g[l, 1], ln_b[l, 1])
    return x2.reshape(B, L, D)
```

```python
import functools
import math
import jax
import jax.numpy as jnp
from jax import lax
from jax.experimental import pallas as pl
from jax.experimental.pallas import tpu as pltpu

D_MODEL = 2048
BATCH = 4
SEQ = 4096
DEPTH = 2

CTX_LEN = 256
GRID_W = 64
GROUP_WIDTH = D_MODEL // 4
MIX_WIDTH = 4 * GROUP_WIDTH
CHUNK = 64
HG_DIM = 128
HG_HEADS = GROUP_WIDTH // HG_DIM
HG_F_MIN = 1e-30
HY_WIDTH = GROUP_WIDTH
HY_ORDER = 2
HY_SHORT = 3
HY_BANDS = 16
HY_EMB = 1 + 2 * HY_BANDS
HY_HIDDEN = 64
HY_FAST_DECAY = 0.3
HY_SLOW_DECAY = 1.5
HY_DECAY_TARGET = 1e-2
HY_MIN_DECAY = math.log(HY_DECAY_TARGET) / HY_SLOW_DECAY
HY_MAX_DECAY = math.log(HY_DECAY_TARGET) / HY_FAST_DECAY
NA_DIM = 64
NA_HEADS = GROUP_WIDTH // NA_DIM
NA_WIN_ROWS = 8
NA_WIN_COLS = 16
NA_QBLK = 16
NA_KBLK = 32
ML_DIM = 128
ML_HEADS = GROUP_WIDTH // ML_DIM
ML_SHORT = 3
ROPE_THETA = 10000.0
PEER_HEADS = 8
PEER_NKEYS = 128
PEER_EXPERTS = PEER_NKEYS * PEER_NKEYS
PEER_DK = 256
PEER_TOPK = 16
HG_COLS = 5 * GROUP_WIDTH
HY_COLS = 3 * GROUP_WIDTH
NA_COLS = 3 * GROUP_WIDTH
ML_COLS = 4 * GROUP_WIDTH + 4 * ML_HEADS
IN_WIDTH = HG_COLS + HY_COLS + NA_COLS + ML_COLS
DEEPNORM_ALPHA = (2 * DEPTH) ** 0.25
DEEPNORM_BETA = (8 * DEPTH) ** -0.25
MASK_VALUE = -1e30


def _centred_dwconv(x, w, b):
    k, L = w.shape[0], x.shape[1]
    xp = jnp.pad(x, ((0, 0), (k // 2, k // 2), (0, 0)))
    y = sum(xp[:, j:j + L] * w[j].astype(x.dtype) for j in range(k))
    return y + b.astype(x.dtype)


def _axial_rope(x):
    L, dh = x.shape[1], x.shape[-1]
    half, quarter = dh // 2, dh // 4
    t = jnp.arange(L)
    inv_freq = ROPE_THETA ** (-jnp.arange(quarter, dtype=jnp.float32) / quarter)

    def rot(xa, pos):
        ang = pos.astype(jnp.float32)[:, None] * inv_freq
        cos = jnp.cos(ang)[None, :, None, :]
        sin = jnp.sin(ang)[None, :, None, :]
        x1, x2 = xa[..., :quarter], xa[..., quarter:]
        return jnp.concatenate([x1 * cos - x2 * sin, x2 * cos + x1 * sin], axis=-1)

    return jnp.concatenate([rot(x[..., :half], t // GRID_W), rot(x[..., half:], t % GRID_W)], axis=-1)


HG_SUB = 16
HG_BLOCK = 256


def _bf16_terms(x):
    f32, bf16 = jnp.float32, jnp.bfloat16
    x_hi = x.astype(bf16)
    r1 = x - x_hi.astype(f32)
    x_mid = r1.astype(bf16)
    return x_hi, x_mid, (r1 - x_mid.astype(f32)).astype(bf16)


def _masked_sums(mask, x):
    return sum(jnp.dot(mask, t, preferred_element_type=jnp.float32) for t in _bf16_terms(x))


def _hgrn2_kernel(*refs, reverse, readout):
    if readout:
        q_ref, z_ref, v_ref, lb_ref, s0_ref, g_ref, nw_ref, of_ref, o_ref, sfin_ref, st_scr = refs
    else:
        q_ref, z_ref, v_ref, lb_ref, s0_ref, o_ref, sfin_ref, st_scr = refs
    f32, bf16 = jnp.float32, jnp.bfloat16
    C, S = CHUNK, HG_SUB
    nsub = C // S
    n_chunks = q_ref.shape[0] // C
    step = pl.program_id(2)

    @pl.when(step == 0)
    def _():
        st_scr[...] = s0_ref[0, 0]

    nt = (((1,), (1,)), ((), ()))
    tn = (((0,), (0,)), ((), ()))
    r_i = lax.broadcasted_iota(jnp.int32, (C, C), 0)
    c_i = lax.broadcasted_iota(jnp.int32, (C, C), 1)
    tri = jnp.where((c_i >= r_i) if reverse else (c_i <= r_i), 1.0, 0.0).astype(bf16)
    row_in_sub = lax.broadcasted_iota(jnp.int32, (S, HG_DIM), 0)
    lb = lb_ref[...]

    def chunk(ci, st):
        c = (n_chunks - 1 - ci) if reverse else ci
        rs = slice(c * C, (c + 1) * C)
        z, q, v = z_ref[rs, :], q_ref[rs, :], v_ref[rs, :]
        neg = jax.nn.sigmoid(-z)
        f = jax.nn.sigmoid(z) + lb * neg
        lf = jnp.log(jnp.maximum(f, HG_F_MIN))
        k = (1.0 - lb) * neg
        cum = _masked_sums(tri, lf)
        total = cum[0:1] if reverse else cum[C - 1:C]
        o = lax.dot_general((q * jnp.exp(cum)).astype(bf16), st.astype(bf16), nt, preferred_element_type=f32)
        kd_end = k * jnp.exp(total - cum)
        st_new = jnp.exp(total) * st + lax.dot_general(v.astype(bf16), kd_end.astype(bf16), tn,
                                                       preferred_element_type=f32)
        parts = []
        for i in range(nsub):
            ts = slice(i * S, (i + 1) * S)
            q_i, cum_i = q[ts], cum[ts]
            acc = o[ts]
            ps = slice((i + 1) * S, C) if reverse else slice(0, i * S)
            if ps.stop > ps.start:
                ref = cum[(i + 1) * S:(i + 1) * S + 1] if reverse else cum[i * S - 1:i * S]
                a = lax.dot_general((q_i * jnp.exp(cum_i - ref)).astype(bf16),
                                    (k[ps] * jnp.exp(ref - cum[ps])).astype(bf16), nt, preferred_element_type=f32)
                acc = acc + jnp.dot(a.astype(bf16), v[ps].astype(bf16), preferred_element_type=f32)
            for s in range(S):
                r = i * S + s
                seen = (row_in_sub <= s) if reverse else (row_in_sub >= s)
                e = jnp.exp(jnp.where(seen, cum_i - cum[r:r + 1], MASK_VALUE))
                w = jnp.sum(q_i * k[r:r + 1] * e, axis=-1, keepdims=True)
                acc = acc + w * v[r:r + 1]
            parts.append(acc)
        o_c = jnp.concatenate(parts, axis=0)
        if readout:
            x = o_c + of_ref[rs, :]
            y = x * lax.rsqrt(jnp.mean(jnp.square(x), axis=-1, keepdims=True) + 1e-6) * nw_ref[...]
            o_ref[rs, :] = (y * jax.nn.silu(g_ref[rs, :])).astype(o_ref.dtype)
        else:
            o_ref[rs, :] = o_c.astype(o_ref.dtype)
        return st_new

    st = st_scr[...]
    for ci in range(n_chunks):
        st = chunk(ci, st)
    st_scr[...] = st

    @pl.when(step == pl.num_programs(2) - 1)
    def _():
        sfin_ref[0, 0] = st


def _hgrn2_scan_call(p, batch, lb, s0, reverse, z_group, readout_args=None):
    L = p.shape[0] // batch
    tb = min(HG_BLOCK, L)
    n_t = L // tb
    assert L % tb == 0 and tb % CHUNK == 0
    H = HG_HEADS

    def rows(b, h, t):
        return b * n_t + ((n_t - 1 - t) if reverse else t)

    def group(g):
        return pl.BlockSpec((tb, HG_DIM), lambda b, h, t: (rows(b, h, t), g * H + h))

    head_cols = pl.BlockSpec((1, HG_DIM), lambda b, h, t: (0, h))
    state = pl.BlockSpec((1, 1, HG_DIM, HG_DIM), lambda b, h, t: (b, h, 0, 0))
    out_rows = pl.BlockSpec((tb, HG_DIM), lambda b, h, t: (rows(b, h, t), h))
    in_specs = [group(0), group(z_group), group(3), head_cols, state]
    args = [p, p, p, lb, s0]
    readout = readout_args is not None
    if readout:
        norm_w, o_other = readout_args
        in_specs += [group(4), pl.BlockSpec((1, HG_DIM), lambda b, h, t: (0, 0)), out_rows]
        args += [p, norm_w.reshape(1, HG_DIM), o_other]
    return pl.pallas_call(
        functools.partial(_hgrn2_kernel, reverse=reverse, readout=readout),
        grid=(batch, H, n_t),
        in_specs=in_specs,
        out_specs=[out_rows, state],
        out_shape=[jax.ShapeDtypeStruct((batch * L, GROUP_WIDTH), jnp.bfloat16 if readout else jnp.float32),
                   jax.ShapeDtypeStruct((batch, H, HG_DIM, HG_DIM), jnp.float32)],
        scratch_shapes=[pltpu.VMEM((HG_DIM, HG_DIM), jnp.float32)],
        compiler_params=pltpu.CompilerParams(dimension_semantics=("arbitrary", "arbitrary", "arbitrary")),
        name="hgrn2_bwd" if reverse else "hgrn2_fwd",
    )(*args)


def _hgrn2_pallas(p_lat, p_ctx, batch, lb, norm_w, need_ctx):
    zero = jnp.zeros((batch, HG_HEADS, HG_DIM, HG_DIM), jnp.float32)
    lb_f, lb_b = lb[0:1], lb[1:2]
    oc_f, s_f = _hgrn2_scan_call(p_ctx, batch, lb_f, zero, False, 1)
    y_ctx, s_b = _hgrn2_scan_call(p_ctx, batch, lb_b, zero, True, 2, (norm_w, oc_f) if need_ctx else None)
    ol_f, _ = _hgrn2_scan_call(p_lat, batch, lb_f, s_f, False, 1)
    y_lat, _ = _hgrn2_scan_call(p_lat, batch, lb_b, s_b, True, 2, (norm_w, ol_f))
    return y_lat, (y_ctx if need_ctx else None)


def _hyena_filters(L, w1, b1, w2, b2, w3, freq):
    f32 = jnp.float32
    pos = jnp.arange(L, dtype=f32)
    t = pos / (L - 1)
    bands = jnp.linspace(1e-4, HY_BANDS - 1, HY_BANDS, dtype=f32)
    ang = (2.0 * math.pi / L) * pos[:, None] * bands[None, :]
    z = jnp.concatenate([t[:, None], jnp.cos(ang), jnp.sin(ang)], axis=-1)
    freq = freq.astype(f32)
    hid = jnp.sin(freq[0] * (z @ w1.astype(f32) + b1.astype(f32)))
    hid = jnp.sin(freq[1] * (hid @ w2.astype(f32) + b2.astype(f32)))
    h = (hid @ w3.astype(f32)).reshape(L, 2, HY_ORDER, HY_WIDTH)
    deltas = jnp.abs(jnp.linspace(HY_MIN_DECAY, HY_MAX_DECAY, HY_WIDTH, dtype=f32))
    h = h * jnp.exp(-t[:, None] * deltas)[:, None, None, :]
    fwd, bwd = h[:, 0], h[:, 1]
    bwd = bwd.at[0].set(0.0)
    norm = jnp.sum(jnp.abs(fwd), axis=0, keepdims=True) + jnp.sum(jnp.abs(bwd), axis=0, keepdims=True)
    return fwd / norm, bwd / norm


HY_FREQ_TILE = 384
HY_TIME_TILE = 256
HY_ANGLE_SPLIT = 64


def _dft_angle_tables(rows, cols, period):
    f32 = jnp.float32
    r = jnp.arange(rows, dtype=jnp.int32)[:, None]
    ch = jnp.arange(cols // HY_ANGLE_SPLIT, dtype=jnp.int32)[None, :] * HY_ANGLE_SPLIT
    cl = jnp.arange(HY_ANGLE_SPLIT, dtype=jnp.int32)[None, :]
    ah = ((r * ch) % period).astype(f32) * (2.0 * math.pi / period)
    al = ((r * cl) % period).astype(f32) * (2.0 * math.pi / period)
    ca, sa, cb, sb = jnp.cos(ah)[:, :, None], jnp.sin(ah)[:, :, None], jnp.cos(al)[:, None, :], jnp.sin(al)[:, None, :]
    return (ca * cb - sa * sb).reshape(rows, cols), (sa * cb + ca * sb).reshape(rows, cols)


def _hyena_dft_matrices(L):
    N = 2 * L
    mp = -(-(L + 1) // HY_FREQ_TILE) * HY_FREQ_TILE
    bf16 = jnp.bfloat16
    k = jnp.arange(mp)
    live = (k <= L).astype(jnp.float32)
    cf, sf = _dft_angle_tables(mp, L, N)
    ci, si = _dft_angle_tables(L, mp, N)
    w = jnp.where((k == 0) | (k == L), 1.0, 2.0) * live / N
    return ((cf * live[:, None]).astype(bf16), (sf * live[:, None]).astype(bf16),
            (ci * w[None, :]).astype(bf16), (-si * w[None, :]).astype(bf16))


def _dft_pair_kernel(c_ref, s_ref, a_ref, b_ref, oc_ref, os_ref):
    oc_ref[...] = jnp.dot(c_ref[...], a_ref[...], preferred_element_type=jnp.float32)
    os_ref[...] = jnp.dot(s_ref[...], b_ref[...], preferred_element_type=jnp.float32)


def _hyena_filter_spectrum(cf, sf, fwd, bwd):
    mp, L = cf.shape
    C = fwd.shape[1]
    bf16 = jnp.bfloat16
    col = lambda i: (0, 0)
    re, s = pl.pallas_call(
        _dft_pair_kernel,
        grid=(mp // HY_FREQ_TILE,),
        in_specs=[pl.BlockSpec((HY_FREQ_TILE, L), lambda i: (i, 0))] * 2 + [pl.BlockSpec((L, C), col)] * 2,
        out_specs=[pl.BlockSpec((HY_FREQ_TILE, C), lambda i: (i, 0))] * 2,
        out_shape=[jax.ShapeDtypeStruct((mp, C), jnp.float32)] * 2,
        compiler_params=pltpu.CompilerParams(vmem_limit_bytes=48 * 1024 * 1024),
        name="hyena_filter_spectrum",
    )(cf, sf, (fwd + bwd).astype(bf16), (fwd - bwd).astype(bf16))
    return re, -s


def _hyena_fwd_kernel(c_ref, s_ref, z_ref, kre_ref, kim_ref, pre_ref, pim_ref):
    f32 = jnp.float32
    xc = jnp.dot(c_ref[...], z_ref[0], preferred_element_type=f32)
    xs = jnp.dot(s_ref[...], z_ref[0], preferred_element_type=f32)
    pre_ref[0] = (xc * kre_ref[...] + xs * kim_ref[...]).astype(pre_ref.dtype)
    pim_ref[0] = (xc * kim_ref[...] - xs * kre_ref[...]).astype(pim_ref.dtype)


def _hyena_inv_kernel(ci_ref, si_ref, pre_ref, pim_ref, z_ref, d_ref, m_ref, o_ref, ob_ref):
    f32 = jnp.float32
    y = (jnp.dot(ci_ref[...], pre_ref[0], preferred_element_type=f32)
         + jnp.dot(si_ref[...], pim_ref[0], preferred_element_type=f32))
    out = m_ref[0] * (y + z_ref[0] * d_ref[...])
    o_ref[0] = out
    ob_ref[0] = out.astype(ob_ref.dtype)


def _hyena_long_conv(mats, z, z_b, kre, kim, d, mult):
    cf, sf, ci, si = mats
    B, L, C = z.shape
    mp = cf.shape[0]
    tt = min(HY_TIME_TILE, L)
    bf16 = jnp.bfloat16
    pre, pim = pl.pallas_call(
        _hyena_fwd_kernel,
        grid=(mp // HY_FREQ_TILE, B),
        in_specs=[pl.BlockSpec((HY_FREQ_TILE, L), lambda i, b: (i, 0))] * 2
                 + [pl.BlockSpec((1, L, C), lambda i, b: (b, 0, 0))]
                 + [pl.BlockSpec((HY_FREQ_TILE, C), lambda i, b: (i, 0))] * 2,
        out_specs=[pl.BlockSpec((1, HY_FREQ_TILE, C), lambda i, b: (b, i, 0))] * 2,
        out_shape=[jax.ShapeDtypeStruct((B, mp, C), bf16)] * 2,
        compiler_params=pltpu.CompilerParams(vmem_limit_bytes=48 * 1024 * 1024),
        name="hyena_fwd",
    )(cf, sf, z_b, kre, kim)
    row = lambda b, t: (b, t, 0)
    return pl.pallas_call(
        _hyena_inv_kernel,
        grid=(B, L // tt),
        in_specs=[pl.BlockSpec((tt, mp), lambda b, t: (t, 0))] * 2
                 + [pl.BlockSpec((1, mp, C), lambda b, t: (b, 0, 0))] * 2
                 + [pl.BlockSpec((1, tt, C), row), pl.BlockSpec((1, C), lambda b, t: (0, 0)),
                    pl.BlockSpec((1, tt, C), row)],
        out_specs=[pl.BlockSpec((1, tt, C), row)] * 2,
        out_shape=[jax.ShapeDtypeStruct((B, L, C), jnp.float32), jax.ShapeDtypeStruct((B, L, C), bf16)],
        compiler_params=pltpu.CompilerParams(vmem_limit_bytes=48 * 1024 * 1024),
        name="hyena_inv",
    )(ci, si, pre, pim, z, d.reshape(1, C).astype(jnp.float32), mult)


def _hyena_seq(p, mats, conv_w, conv_b, w1, b1, w2, b2, w3, freq, dbias):
    u = _centred_dwconv(p, conv_w, conv_b).astype(jnp.float32)
    v, x1, x2 = jnp.split(u, 3, axis=-1)
    L = p.shape[1]
    fwd, bwd = _hyena_filters(L, w1, b1, w2, b2, w3, freq)
    kre, kim = _hyena_filter_spectrum(mats[0], mats[1], fwd.reshape(L, -1), bwd.reshape(L, -1))
    W = HY_WIDTH
    z, z_b = _hyena_long_conv(mats, v, v.astype(jnp.bfloat16), kre[:, :W], kim[:, :W], dbias[0], x1)
    return _hyena_long_conv(mats, z, z_b, kre[:, W:], kim[:, W:], dbias[1], x2)[1]


def _na_pair_attention(q_pair, key_sets, lane):
    f32 = jnp.float32
    nt = (((1,), (1,)), ((), ()))
    outs = []
    for half in range(2):
        own = (lane >= NA_DIM) if half else (lane < NA_DIM)
        q = jnp.where(own, q_pair, jnp.zeros_like(q_pair))
        scores = []
        for k_pair, _, add, valid in key_sets:
            s = lax.dot_general(q, k_pair, nt, preferred_element_type=f32) * (NA_DIM ** -0.5)
            if add is not None:
                s = jnp.where(valid, s + add[half], MASK_VALUE)
            scores.append(s)
        m = scores[0].max(axis=-1, keepdims=True)
        for s in scores[1:]:
            m = jnp.maximum(m, s.max(axis=-1, keepdims=True))
        den = jnp.zeros_like(m)
        acc = jnp.zeros(q_pair.shape, f32)
        for s, (_, v_pair, _, _) in zip(scores, key_sets):
            p = jnp.exp(s - m)
            den = den + p.sum(axis=-1, keepdims=True)
            acc = acc + jnp.dot(p.astype(jnp.bfloat16), v_pair, preferred_element_type=f32)
        outs.append(acc / den)
    return jnp.where(lane < NA_DIM, outs[0], outs[1])


def _natten_lat_kernel(q_ref, k_ref, v_ref, kc_ref, vc_ref, bias_ref, o_ref, *, rows):
    kh = NA_WIN_ROWS
    r = pl.program_id(1)
    start = pl.multiple_of(jnp.clip(r - kh // 2, 0, rows - kh) * GRID_W, GRID_W)
    nk = kh * GRID_W
    qcol = lax.broadcasted_iota(jnp.int32, (GRID_W, nk), 0)
    kcol = lax.broadcasted_iota(jnp.int32, (GRID_W, nk), 1) % GRID_W
    col_start = jnp.clip(qcol - NA_WIN_COLS // 2, 0, GRID_W - NA_WIN_COLS)
    valid = (kcol >= col_start) & (kcol < col_start + NA_WIN_COLS)
    lane = lax.broadcasted_iota(jnp.int32, (GRID_W, 2 * NA_DIM), 1)
    for hp in range(NA_HEADS // 2):
        cs = slice(hp * 2 * NA_DIM, (hp + 1) * 2 * NA_DIM)
        win = (k_ref[pl.ds(start, nk), cs], v_ref[pl.ds(start, nk), cs],
               (bias_ref[0, 2 * hp], bias_ref[0, 2 * hp + 1]), valid)
        ctx = (kc_ref[:, cs], vc_ref[:, cs], None, None)
        o_ref[:, cs] = _na_pair_attention(q_ref[:, cs], [win, ctx], lane).astype(o_ref.dtype)


def _natten_ctx_kernel(q_ref, k_ref, v_ref, o_ref):
    lane = lax.broadcasted_iota(jnp.int32, (q_ref.shape[0], 2 * NA_DIM), 1)
    for hp in range(NA_HEADS // 2):
        cs = slice(hp * 2 * NA_DIM, (hp + 1) * 2 * NA_DIM)
        o_ref[:, cs] = _na_pair_attention(q_ref[:, cs], [(k_ref[:, cs], v_ref[:, cs], None, None)],
                                          lane).astype(o_ref.dtype)


def _natten_bias_table(rpb, rows):
    kh = NA_WIN_ROWS
    dc = jnp.clip(jnp.arange(GRID_W)[None, :] - jnp.arange(GRID_W)[:, None], 1 - NA_WIN_COLS, NA_WIN_COLS - 1)
    t = rpb.astype(jnp.float32)[:, :, dc + NA_WIN_COLS - 1]
    per_off = [t[:, d:d + kh].transpose(0, 2, 1, 3).reshape(NA_HEADS, GRID_W, kh * GRID_W) for d in range(kh)]
    return jnp.stack(per_off, axis=0)


def _natten_pallas(p_lat, p_ctx, rpb, need_ctx, batch):
    W = GROUP_WIDTH
    L = p_lat.shape[0] // batch
    n_ctx = p_ctx.shape[0] // batch
    rows = L // GRID_W
    kh = NA_WIN_ROWS
    assert rows >= kh and L % GRID_W == 0
    bias = _natten_bias_table(rpb, rows)

    def first_key_row_offset(b, r):
        return (jnp.clip(r - kh // 2, 0, rows - kh) - r + kh - 1, 0, 0, 0)

    y_lat = pl.pallas_call(
        functools.partial(_natten_lat_kernel, rows=rows),
        grid=(batch, rows),
        in_specs=[pl.BlockSpec((GRID_W, W), lambda b, r: (b * rows + r, 0)),
                  pl.BlockSpec((L, W), lambda b, r: (b, 1)),
                  pl.BlockSpec((L, W), lambda b, r: (b, 2)),
                  pl.BlockSpec((n_ctx, W), lambda b, r: (b, 1)),
                  pl.BlockSpec((n_ctx, W), lambda b, r: (b, 2)),
                  pl.BlockSpec((1, NA_HEADS, GRID_W, kh * GRID_W), first_key_row_offset)],
        out_specs=pl.BlockSpec((GRID_W, W), lambda b, r: (b * rows + r, 0)),
        out_shape=jax.ShapeDtypeStruct((batch * L, W), jnp.bfloat16),
        compiler_params=pltpu.CompilerParams(vmem_limit_bytes=40 * 1024 * 1024),
        name="natten_lat",
    )(p_lat, p_lat, p_lat, p_ctx, p_ctx, bias)
    y_ctx = None
    if need_ctx:
        y_ctx = pl.pallas_call(
            _natten_ctx_kernel,
            grid=(batch,),
            in_specs=[pl.BlockSpec((n_ctx, W), lambda b: (b, 0)),
                      pl.BlockSpec((n_ctx, W), lambda b: (b, 1)),
                      pl.BlockSpec((n_ctx, W), lambda b: (b, 2))],
            out_specs=pl.BlockSpec((n_ctx, W), lambda b: (b, 0)),
            out_shape=jax.ShapeDtypeStruct((batch * n_ctx, W), jnp.bfloat16),
            name="natten_ctx",
        )(p_ctx, p_ctx, p_ctx)
    return y_lat, y_ctx


ML_BLOCK = 256
ML_GATES = 4 * ML_HEADS


def _mlstm_kernel(*refs, reverse, readout):
    if readout:
        (q_ref, k_ref, v_ref, gc_ref, gr_ref, c0_ref, n0_ref, m0_ref, og_ref, hf_ref,
         o_ref, cfin_ref, nfin_ref, mfin_ref, c_scr, n_scr, m_scr) = refs
    else:
        (q_ref, k_ref, v_ref, gc_ref, gr_ref, c0_ref, n0_ref, m0_ref,
         o_ref, cfin_ref, nfin_ref, mfin_ref, c_scr, n_scr, m_scr) = refs
    f32, bf16 = jnp.float32, jnp.bfloat16
    C = CHUNK
    n_chunks = q_ref.shape[0] // C
    head = pl.program_id(1)
    step = pl.program_id(2)

    @pl.when(step == 0)
    def _():
        c_scr[...] = c0_ref[0, 0]
        n_scr[...] = n0_ref[0, 0]
        m_scr[...] = m0_ref[0, 0]

    nt = (((1,), (1,)), ((), ()))
    tn = (((0,), (0,)), ((), ()))
    r_i = lax.broadcasted_iota(jnp.int32, (C, C), 0)
    c_i = lax.broadcasted_iota(jnp.int32, (C, C), 1)
    seen = (c_i >= r_i) if reverse else (c_i <= r_i)
    tri = jnp.where(seen, 1.0, 0.0).astype(bf16)
    tri_t = jnp.where((r_i >= c_i) if reverse else (r_i <= c_i), 1.0, 0.0).astype(bf16)
    ig_idx = (2 if reverse else 0) * ML_HEADS + head
    fg_idx = ig_idx + ML_HEADS
    lane = lax.broadcasted_iota(jnp.int32, (C, ML_GATES), 1)
    sub = lax.broadcasted_iota(jnp.int32, (ML_GATES, C), 0)

    def chunk(ci, state):
        ckv, n, m = state
        c = (n_chunks - 1 - ci) if reverse else ci
        rs = slice(c * C, (c + 1) * C)
        q, k, v = q_ref[rs, :], k_ref[rs, :], v_ref[rs, :]
        gates = gc_ref[rs, :]
        lf_cols = jax.nn.log_sigmoid(gates)
        g_cols = _masked_sums(tri, lf_cols)
        g_c = jnp.sum(jnp.where(lane == fg_idx, g_cols, 0.0), axis=1, keepdims=True)
        ig_c = jnp.sum(jnp.where(lane == ig_idx, gates, 0.0), axis=1, keepdims=True)
        gates_r = gr_ref[:, rs]
        ig_r = jnp.sum(jnp.where(sub == ig_idx, gates_r, 0.0), axis=0, keepdims=True)
        lf_r = jnp.sum(jnp.where(sub == fg_idx, jax.nn.log_sigmoid(gates_r), 0.0), axis=0, keepdims=True)
        g_r = sum(jnp.dot(t, tri_t, preferred_element_type=f32) for t in _bf16_terms(lf_r))
        m1 = m[:, 0:1]
        logd = jnp.where(seen, g_c - g_r + ig_r, MASK_VALUE)
        log_inter = g_c + m1
        m_out = jnp.maximum(log_inter, jnp.max(logd, axis=1, keepdims=True))
        dmat = jnp.exp(logd - m_out)
        w_inter = jnp.exp(log_inter - m_out)
        qb = q.astype(bf16)
        sc = lax.dot_general(qb, k.astype(bf16), nt, preferred_element_type=f32) * dmat
        num = (jnp.dot(sc.astype(bf16), v.astype(bf16), preferred_element_type=f32)
               + w_inter * jnp.dot(qb, ckv.astype(bf16), preferred_element_type=f32))
        den = jnp.sum(sc, axis=1, keepdims=True) + w_inter * jnp.sum(q * n, axis=1, keepdims=True)
        h = num / jnp.maximum(jnp.abs(den), jnp.exp(-m_out))
        g_end = g_c[0:1] if reverse else g_c[C - 1:C]
        a = g_end - g_c + ig_c
        m_new = jnp.maximum(g_end + m1, jnp.max(a, axis=0, keepdims=True))
        carry_w = jnp.exp(g_end + m1 - m_new)
        wk = jnp.exp(a - m_new) * k
        ckv_new = carry_w * ckv + lax.dot_general(wk.astype(bf16), v.astype(bf16), tn, preferred_element_type=f32)
        n_new = carry_w * n + jnp.sum(wk, axis=0, keepdims=True)
        if readout:
            o_ref[rs, :] = (jax.nn.sigmoid(og_ref[rs, :]) * (h + hf_ref[rs, :])).astype(o_ref.dtype)
        else:
            o_ref[rs, :] = h.astype(o_ref.dtype)
        return ckv_new, n_new, jnp.broadcast_to(m_new, m.shape)

    state = (c_scr[...], n_scr[...], m_scr[...])
    for ci in range(n_chunks):
        state = chunk(ci, state)
    c_scr[...], n_scr[...], m_scr[...] = state

    @pl.when(step == pl.num_programs(2) - 1)
    def _():
        cfin_ref[0, 0], nfin_ref[0, 0], mfin_ref[0, 0] = state


def _mlstm_scan_call(q, k, v_src, v_col0, gates, gates_t, batch, state0, reverse, readout_args=None):
    L = q.shape[0] // batch
    tb = min(ML_BLOCK, L)
    n_t = L // tb
    assert L % tb == 0 and tb % CHUNK == 0 and tb % 128 == 0
    H = ML_HEADS

    def rows(b, h, t):
        return b * n_t + ((n_t - 1 - t) if reverse else t)

    head_block = pl.BlockSpec((tb, ML_DIM), lambda b, h, t: (rows(b, h, t), h))
    mat_state = pl.BlockSpec((1, 1, ML_DIM, ML_DIM), lambda b, h, t: (b, h, 0, 0))
    vec_state = pl.BlockSpec((1, 1, 1, ML_DIM), lambda b, h, t: (b, h, 0, 0))
    in_specs = [head_block, head_block,
                pl.BlockSpec((tb, ML_DIM), lambda b, h, t: (rows(b, h, t), v_col0 + h)),
                pl.BlockSpec((tb, ML_GATES), lambda b, h, t: (rows(b, h, t), 0)),
                pl.BlockSpec((ML_GATES, tb), lambda b, h, t: (0, rows(b, h, t))),
                mat_state, vec_state, vec_state]
    args = [q, k, v_src, gates, gates_t, *state0]
    readout = readout_args is not None
    if readout:
        og_src, og_col0, h_other = readout_args
        in_specs += [pl.BlockSpec((tb, ML_DIM), lambda b, h, t: (rows(b, h, t), og_col0 + h)), head_block]
        args += [og_src, h_other]
    outs = pl.pallas_call(
        functools.partial(_mlstm_kernel, reverse=reverse, readout=readout),
        grid=(batch, H, n_t),
        in_specs=in_specs,
        out_specs=[head_block, mat_state, vec_state, vec_state],
        out_shape=[jax.ShapeDtypeStruct((batch * L, GROUP_WIDTH), jnp.bfloat16 if readout else jnp.float32),
                   jax.ShapeDtypeStruct((batch, H, ML_DIM, ML_DIM), jnp.float32),
                   jax.ShapeDtypeStruct((batch, H, 1, ML_DIM), jnp.float32),
                   jax.ShapeDtypeStruct((batch, H, 1, ML_DIM), jnp.float32)],
        scratch_shapes=[pltpu.VMEM((ML_DIM, ML_DIM), jnp.float32), pltpu.VMEM((1, ML_DIM), jnp.float32),
                        pltpu.VMEM((1, ML_DIM), jnp.float32)],
        compiler_params=pltpu.CompilerParams(dimension_semantics=("arbitrary", "arbitrary", "arbitrary")),
        name="mlstm_bwd" if reverse else "mlstm_fwd",
    )(*args)
    return outs[0], tuple(outs[1:])


def _mlstm_pallas(pm_lat, pm_ctx, batch, conv_w, conv_b, gate_b, need_ctx):
    W = GROUP_WIDTH
    f32 = jnp.float32

    def prep(pm, rope):
        L = pm.shape[0] // batch
        qk = jax.nn.silu(_centred_dwconv(pm[:, :2 * W].reshape(batch, L, 2 * W), conv_w, conv_b)).astype(f32)
        q = qk[..., :W].reshape(batch, L, ML_HEADS, ML_DIM)
        k = qk[..., W:].reshape(batch, L, ML_HEADS, ML_DIM)
        if rope:
            q, k = _axial_rope(q), _axial_rope(k)
        gates = pm[:, 4 * W:] + gate_b.astype(f32)
        return q.reshape(batch * L, W), (k * ML_DIM ** -0.5).reshape(batch * L, W), gates, gates.T

    zero = (jnp.zeros((batch, ML_HEADS, ML_DIM, ML_DIM), f32), jnp.zeros((batch, ML_HEADS, 1, ML_DIM), f32),
            jnp.zeros((batch, ML_HEADS, 1, ML_DIM), f32))
    v0, o0 = 2 * W // ML_DIM, 3 * W // ML_DIM
    qc, kc, gc, gct = prep(pm_ctx, False)
    hc_f, st_f = _mlstm_scan_call(qc, kc, pm_ctx, v0, gc, gct, batch, zero, False)
    y_ctx, st_b = _mlstm_scan_call(qc, kc, pm_ctx, v0, gc, gct, batch, zero, True,
                                   (pm_ctx, o0, hc_f) if need_ctx else None)
    ql, kl, gl, glt = prep(pm_lat, True)
    hl_f, _ = _mlstm_scan_call(ql, kl, pm_lat, v0, gl, glt, batch, st_f, False)
    y_lat, _ = _mlstm_scan_call(ql, kl, pm_lat, v0, gl, glt, batch, st_b, True, (pm_lat, o0, hl_f))
    return y_lat, (y_ctx if need_ctx else None)


PEER_SUB_DK = PEER_DK // 2
PEER_CAND_ROWS = 16 + 8 + 6 * 8 + 8


def _topk_rows(s, order, payload, k):
    sentinel = jnp.float32(2 ** 24)
    vals, picked = [], []
    for _ in range(k):
        m = s.max(axis=0, keepdims=True)
        first = jnp.min(jnp.where(s == m, order, sentinel), axis=0, keepdims=True)
        hit = order == first
        vals.append(m)
        picked.append(first if payload is None else jnp.max(jnp.where(hit, payload, -1.0), axis=0, keepdims=True))
        s = jnp.where(hit, -jnp.inf, s)
    return jnp.concatenate(vals, axis=0), jnp.concatenate(picked, axis=0)


def _peer_candidates(a0, a1, combine):
    pieces = [combine(a0[0:1], a1), combine(a0[1:2], a1[0:8])]
    pieces += [combine(a0[i:i + 1], a1[0:8]) for i in range(2, 8)]
    pieces.append(combine(a0[8:16], a1[0:1]))
    return jnp.concatenate(pieces, axis=0)


def _peer_topk_kernel(x_ref, sc_ref, sh_ref, wq_ref, sk_ref, h_ref, ids_ref, gate_ref):
    f32 = jnp.float32
    tb = x_ref.shape[0]
    h = x_ref[...] * sc_ref[0] + sh_ref[0]
    h_ref[...] = h.astype(h_ref.dtype)
    q = jnp.dot(h.astype(jnp.bfloat16), wq_ref[...], preferred_element_type=f32).astype(jnp.bfloat16)
    key_order = lax.broadcasted_iota(jnp.int32, (PEER_NKEYS, tb), 0).astype(f32)
    rho = lax.broadcasted_iota(jnp.int32, (PEER_CAND_ROWS, tb), 0)
    ci = jnp.where(rho < 16, 0, jnp.where(rho < 24, 1, jnp.where(rho < 72, 2 + ((rho - 24) >> 3), rho - 64)))
    cj = jnp.where(rho < 16, rho, jnp.where(rho < 24, rho - 16, jnp.where(rho < 72, (rho - 24) & 7, 0)))
    cand_order = (ci * PEER_TOPK + cj).astype(f32)
    cand_valid = (ci + 1) * (cj + 1) <= PEER_TOPK
    nt = (((1,), (1,)), ((), ()))
    for head in range(PEER_HEADS):
        sv, si = [], []
        for half in range(2):
            c0 = (head * 2 + half) * PEER_SUB_DK
            s_t = lax.dot_general(sk_ref[head * 2 + half], q[:, c0:c0 + PEER_SUB_DK], nt,
                                  preferred_element_type=f32)
            v, i = _topk_rows(s_t, key_order, None, PEER_TOPK)
            sv.append(v)
            si.append(i)
        cand_s = _peer_candidates(sv[0], sv[1], lambda a, b: a + b)
        cand_s = jnp.where(cand_valid, cand_s, -jnp.inf)
        cand_id = _peer_candidates(si[0], si[1], lambda a, b: a * PEER_NKEYS + b)
        best_s, ids = _topk_rows(cand_s, cand_order, cand_id, PEER_TOPK)
        e = jnp.exp(best_s - best_s[0:1])
        rs = slice(head * PEER_TOPK, (head + 1) * PEER_TOPK)
        ids_ref[rs, :] = ids.astype(jnp.int32)
        gate_ref[rs, :] = e / e.sum(axis=0, keepdims=True)


def _peer_route(x2d, scale, shift, rows_per_mod, wq, subkeys, tb=256):
    T, D = x2d.shape
    nk = PEER_HEADS * PEER_TOPK
    sk = subkeys.reshape(PEER_HEADS * 2, PEER_NKEYS, PEER_SUB_DK).astype(jnp.bfloat16)
    mod = lambda i: (i * tb // rows_per_mod, 0, 0)
    h, ids_t, gate_t = pl.pallas_call(
        _peer_topk_kernel,
        grid=(T // tb,),
        in_specs=[pl.BlockSpec((tb, D), lambda i: (i, 0)),
                  pl.BlockSpec((1, 1, D), mod),
                  pl.BlockSpec((1, 1, D), mod),
                  pl.BlockSpec(wq.shape, lambda i: (0, 0)),
                  pl.BlockSpec(sk.shape, lambda i: (0, 0, 0))],
        out_specs=[pl.BlockSpec((tb, D), lambda i: (i, 0)),
                   pl.BlockSpec((nk, tb), lambda i: (0, i)),
                   pl.BlockSpec((nk, tb), lambda i: (0, i))],
        out_shape=[jax.ShapeDtypeStruct((T, D), jnp.bfloat16),
                   jax.ShapeDtypeStruct((nk, T), jnp.int32),
                   jax.ShapeDtypeStruct((nk, T), jnp.float32)],
        compiler_params=pltpu.CompilerParams(vmem_limit_bytes=48 * 1024 * 1024),
        name="peer_route",
    )(x2d, scale, shift, wq.astype(jnp.bfloat16), sk)
    return h, ids_t.T, gate_t.T


PEER_MASK_SLAB = PEER_NKEYS + 8
PEER_MASK_UNROLL = 8
PEER_KEYS_PER_STEP = 8
PEER_TOKEN_TILE = 512


def _bf16_bits(x):
    u = lax.bitcast_convert_type(x, jnp.uint32)
    return (u + jnp.uint32(0x7FFF) + ((u >> 16) & jnp.uint32(1))) >> 16


def _peer_expert_kernel(x_ref, ids_ref, gate_ref, u_ref, v_ref, o_ref, w_scr):
    f32, bf16 = jnp.float32, jnp.bfloat16
    j = pl.program_id(1)
    tm = x_ref.shape[0]
    half = tm // 2
    nt = (((1,), (1,)), ((), ()))

    @pl.when(j == 0)
    def _():
        o_ref[...] = jnp.zeros_like(o_ref)
        key_iota = lax.broadcasted_iota(jnp.int32, (PEER_NKEYS, ids_ref.shape[1]), 0)

        def token_mask(t):
            ids = ids_ref[pl.ds(t, 1), :]
            g = gate_ref[pl.ds(t, 1), :]
            first = jnp.where((ids >> 7) == key_iota, 1.0, 0.0).astype(bf16)
            second = jnp.where((ids & (PEER_NKEYS - 1)) == key_iota, g, 0.0).astype(bf16)
            return lax.dot_general(first, second, nt, preferred_element_type=f32)

        def pair_group(i, carry):
            for s in range(PEER_MASK_UNROLL // 2):
                p = i * (PEER_MASK_UNROLL // 2) + s
                packed = _bf16_bits(token_mask(p)) | (_bf16_bits(token_mask(p + half)) << 16)
                w_scr[pl.ds(pl.multiple_of(p * PEER_MASK_SLAB, 8), PEER_NKEYS), :] = packed
            return carry

        lax.fori_loop(0, half // (PEER_MASK_UNROLL // 2), pair_group, 0)

    a = lax.dot_general(x_ref[...], u_ref[...], nt, preferred_element_type=f32)
    cols = []
    for s in range(PEER_KEYS_PER_STEP):
        packed = w_scr[pl.ds(j * PEER_KEYS_PER_STEP + s, half, stride=PEER_MASK_SLAB), :]
        lo = lax.bitcast_convert_type(packed << 16, f32)
        hi = lax.bitcast_convert_type(packed & jnp.uint32(0xFFFF0000), f32)
        cols.append(jnp.concatenate([lo, hi], axis=0))
    w = jnp.concatenate(cols, axis=1)
    gelu = 0.5 * a * (1.0 + lax.erf(a * (2.0 ** -0.5)))
    c = (w * gelu).astype(bf16)
    o_ref[...] += jnp.dot(c, v_ref[...], preferred_element_type=f32)


def _peer_expert(tok, ids, gate, u, v):
    T, D = tok.shape
    tm = min(PEER_TOKEN_TILE, T)
    te = PEER_KEYS_PER_STEP * PEER_NKEYS
    nk = ids.shape[1]
    assert PEER_NKEYS == 128 and T % tm == 0 and PEER_EXPERTS % te == 0 and (tm // 2) % PEER_MASK_UNROLL == 0
    return pl.pallas_call(
        _peer_expert_kernel,
        grid=(T // tm, PEER_EXPERTS // te),
        in_specs=[pl.BlockSpec((tm, D), lambda i, j: (i, 0)),
                  pl.BlockSpec((tm, nk), lambda i, j: (i, 0)),
                  pl.BlockSpec((tm, nk), lambda i, j: (i, 0)),
                  pl.BlockSpec((te, D), lambda i, j: (j, 0)),
                  pl.BlockSpec((te, D), lambda i, j: (j, 0))],
        out_specs=pl.BlockSpec((tm, D), lambda i, j: (i, 0)),
        out_shape=jax.ShapeDtypeStruct((T, D), jnp.float32),
        scratch_shapes=[pltpu.VMEM((tm // 2 * PEER_MASK_SLAB, PEER_NKEYS), jnp.uint32)],
        compiler_params=pltpu.CompilerParams(dimension_semantics=("arbitrary", "arbitrary"),
                                             vmem_limit_bytes=56 * 1024 * 1024),
        name="peer_expert",
    )(tok, ids, gate, u, v)


def _peer(x2d, scale, shift, rows_per_mod, wq, subkeys, u, v):
    tok, ids, gate = _peer_route(x2d, scale, shift, rows_per_mod, wq, subkeys)
    return _peer_expert(tok, ids, gate, u, v)


def _in_proj_kernel(x_ref, sc_ref, sh_ref, w_ref, o_ref, a_scr):
    @pl.when(pl.program_id(1) == 0)
    def _():
        a_scr[...] = (x_ref[...] * sc_ref[0] + sh_ref[0]).astype(a_scr.dtype)

    o_ref[...] = jnp.dot(a_scr[...], w_ref[...], preferred_element_type=jnp.float32).astype(o_ref.dtype)


def _in_proj(x2d, scale, shift, rows_per_mod, w, col0, ncols, out_dtype, tm=1024, tn=512):
    M, K = x2d.shape
    tm = min(tm, M)
    assert col0 % tn == 0 and M % tm == 0
    mod = lambda i, j: (i * tm // rows_per_mod, 0, 0)
    return pl.pallas_call(
        _in_proj_kernel,
        grid=(M // tm, pl.cdiv(ncols, tn)),
        in_specs=[pl.BlockSpec((tm, K), lambda i, j: (i, 0)),
                  pl.BlockSpec((1, 1, K), mod),
                  pl.BlockSpec((1, 1, K), mod),
                  pl.BlockSpec((K, tn), lambda i, j: (0, col0 // tn + j))],
        out_specs=pl.BlockSpec((tm, tn), lambda i, j: (i, j)),
        out_shape=jax.ShapeDtypeStruct((M, ncols), out_dtype),
        scratch_shapes=[pltpu.VMEM((tm, K), jnp.bfloat16)],
        compiler_params=pltpu.CompilerParams(dimension_semantics=("arbitrary", "arbitrary"),
                                             vmem_limit_bytes=48 * 1024 * 1024),
        name="in_proj",
    )(x2d, scale, shift, w)


def _out_proj_ln_kernel(a_ref, b_ref, c_ref, d_ref, w_ref, x_ref, gt_ref, g_ref, bt_ref, o_ref):
    f32 = jnp.float32
    W = GROUP_WIDTH
    y = jnp.zeros(x_ref.shape, f32)
    for g, r in enumerate((a_ref, b_ref, c_ref, d_ref)):
        y = y + jnp.dot(r[...].astype(jnp.bfloat16), w_ref[g * W:(g + 1) * W, :], preferred_element_type=f32)
    z = DEEPNORM_ALPHA * x_ref[...] + gt_ref[0] * y
    mu = jnp.mean(z, axis=-1, keepdims=True)
    var = jnp.mean(jnp.square(z - mu), axis=-1, keepdims=True)
    o_ref[...] = (z - mu) * lax.rsqrt(var + 1e-5) * g_ref[...] + bt_ref[...]


def _residual_ln_kernel(x_ref, y_ref, gt_ref, g_ref, bt_ref, o_ref):
    z = DEEPNORM_ALPHA * x_ref[...] + gt_ref[0] * y_ref[...]
    mu = jnp.mean(z, axis=-1, keepdims=True)
    var = jnp.mean(jnp.square(z - mu), axis=-1, keepdims=True)
    o_ref[...] = (z - mu) * lax.rsqrt(var + 1e-5) * g_ref[...] + bt_ref[...]


def _residual_ln(x2d, y2d, gate, rows_per_mod, ln_g, ln_b, tm=256):
    M, D = x2d.shape
    tm = min(tm, M)
    row = lambda i: (i, 0)
    return pl.pallas_call(
        _residual_ln_kernel,
        grid=(M // tm,),
        in_specs=[pl.BlockSpec((tm, D), row), pl.BlockSpec((tm, D), row),
                  pl.BlockSpec((1, 1, D), lambda i: (i * tm // rows_per_mod, 0, 0)),
                  pl.BlockSpec((1, D), lambda i: (0, 0)), pl.BlockSpec((1, D), lambda i: (0, 0))],
        out_specs=pl.BlockSpec((tm, D), row),
        out_shape=jax.ShapeDtypeStruct((M, D), jnp.float32),
        name="residual_ln",
    )(x2d, y2d, gate, ln_g.reshape(1, D), ln_b.reshape(1, D))


def _out_proj_ln(parts, w, x2d, gate, rows_per_mod, ln_g, ln_b, tm=256):
    M, D = x2d.shape
    W = GROUP_WIDTH
    tm = min(tm, M)
    row = lambda i: (i, 0)
    return pl.pallas_call(
        _out_proj_ln_kernel,
        grid=(M // tm,),
        in_specs=[pl.BlockSpec((tm, W), row)] * 4
                 + [pl.BlockSpec(w.shape, lambda i: (0, 0)),
                    pl.BlockSpec((tm, D), row),
                    pl.BlockSpec((1, 1, D), lambda i: (i * tm // rows_per_mod, 0, 0)),
                    pl.BlockSpec((1, D), lambda i: (0, 0)),
                    pl.BlockSpec((1, D), lambda i: (0, 0))],
        out_specs=pl.BlockSpec((tm, D), row),
        out_shape=jax.ShapeDtypeStruct((M, D), jnp.float32),
        compiler_params=pltpu.CompilerParams(vmem_limit_bytes=48 * 1024 * 1024),
        name="out_proj_ln",
    )(*parts, w, x2d, gate, ln_g.reshape(1, D), ln_b.reshape(1, D))


def kernel(x, c, ctx, c_ctx, w_ada, b_ada, w_in, w_out, ln_g, ln_b, hg_lower_bounds, hg_norm_w,
           hy_conv_w, hy_conv_b, hy_w1, hy_b1, hy_w2, hy_b2, hy_w3, hy_freq, hy_dbias, na_rpb,
           ml_conv_w, ml_conv_b, ml_gate_b, peer_wq, peer_subkeys, peer_u, peer_v):
    f32, bf16 = jnp.float32, jnp.bfloat16
    B, L, D = x.shape
    n_ctx = ctx.shape[1]
    lb_soft = jax.nn.softmax(hg_lower_bounds.astype(f32), axis=0)
    lower_bounds = jnp.cumsum(lb_soft, axis=0) - lb_soft[0]
    cond_lat = jax.nn.silu(c)
    cond_ctx = jax.nn.silu(c_ctx)
    c0_na = HG_COLS + HY_COLS
    c0_ml = c0_na + NA_COLS
    x2 = x.reshape(B * L, D)
    dft_lat = _hyena_dft_matrices(L)
    xc2 = ctx.reshape(B * n_ctx, D)
    for l in range(DEPTH):
        need_ctx = l < DEPTH - 1
        m_lat = (cond_lat @ w_ada[l] + b_ada[l]).reshape(B, 1, 6, D)
        m_ctx = (cond_ctx @ w_ada[l] + b_ada[l]).reshape(1, 1, 6, D)
        ml = [m_lat[:, :, k] for k in range(6)]
        mc = [m_ctx[:, :, k] for k in range(6)]
        w_in_l = w_in[l].astype(bf16)
        w_out_l = w_out[l].astype(bf16)

        def in_proj(t, m, rows_per_mod):
            sc, sh = 1 + m[1], m[0]
            return (_in_proj(t, sc, sh, rows_per_mod, w_in_l, 0, c0_na, f32),
                    _in_proj(t, sc, sh, rows_per_mod, w_in_l, c0_na, NA_COLS, bf16),
                    _in_proj(t, sc, sh, rows_per_mod, w_in_l, c0_ml, ML_COLS, f32))

        ph_l, pn_l, pm_l = in_proj(x2, ml, L)
        ph_c, pn_c, pm_c = in_proj(xc2, mc, B * n_ctx)
        pb_l = ph_l[:, HG_COLS:].reshape(B, L, -1)
        pb_c = ph_c[:, HG_COLS:].reshape(B, n_ctx, -1)
        hy = (hy_conv_w[l], hy_conv_b[l], hy_w1[l], hy_b1[l], hy_w2[l], hy_b2[l], hy_w3[l], hy_freq[l], hy_dbias[l])
        a_l, a_c = _hgrn2_pallas(ph_l, ph_c, B, lower_bounds[l], hg_norm_w[l], need_ctx)
        b_l = _hyena_seq(pb_l, dft_lat, *hy)
        c_l, c_c = _natten_pallas(pn_l, pn_c, na_rpb[l], need_ctx, B)
        d_l, d_c = _mlstm_pallas(pm_l, pm_c, B, ml_conv_w[l], ml_conv_b[l], ml_gate_b[l], need_ctx)
        parts_l = (a_l, b_l.reshape(B * L, -1), c_l, d_l)
        if need_ctx:
            b_c = _hyena_seq(pb_c, _hyena_dft_matrices(n_ctx), *hy)
            parts_c = (a_c, b_c.reshape(B * n_ctx, -1), c_c, d_c)
            xc2 = _out_proj_ln(parts_c, w_out_l, xc2, mc[2], B * n_ctx, ln_g[l, 0], ln_b[l, 0])
        x2 = _out_proj_ln(parts_l, w_out_l, x2, ml[2], L, ln_g[l, 0], ln_b[l, 0])
        u_l, v_l = peer_u[l].astype(bf16), peer_v[l].astype(bf16)
        y_lat = _peer(x2, 1 + ml[4], ml[3], L, peer_wq[l], peer_subkeys[l], u_l, v_l)
        x2 = _residual_ln(x2, y_lat, ml[5], L, ln_g[l, 1], ln_b[l, 1])
        if need_ctx:
            y_ctx = _peer(xc2, 1 + mc[4], mc[3], B * n_ctx, peer_wq[l], peer_subkeys[l], u_l, v_l)
            xc2 = _residual_ln(xc2, y_ctx, mc[5], B * n_ctx, ln_g[l, 1], ln_b[l, 1])
    return x2.reshape(B, L, D)
```

```python
import functools
import math
import jax
import jax.numpy as jnp
from jax import lax
from jax.experimental import pallas as pl
from jax.experimental.pallas import tpu as pltpu

D_MODEL = 2048
BATCH = 4
SEQ = 4096
DEPTH = 2

CTX_LEN = 256
GRID_W = 64
GROUP_WIDTH = D_MODEL // 4
MIX_WIDTH = 4 * GROUP_WIDTH
CHUNK = 64
HG_DIM = 128
HG_HEADS = GROUP_WIDTH // HG_DIM
HG_F_MIN = 1e-30
HY_WIDTH = GROUP_WIDTH
HY_ORDER = 2
HY_SHORT = 3
HY_BANDS = 16
HY_EMB = 1 + 2 * HY_BANDS
HY_HIDDEN = 64
HY_FAST_DECAY = 0.3
HY_SLOW_DECAY = 1.5
HY_DECAY_TARGET = 1e-2
HY_MIN_DECAY = math.log(HY_DECAY_TARGET) / HY_SLOW_DECAY
HY_MAX_DECAY = math.log(HY_DECAY_TARGET) / HY_FAST_DECAY
NA_DIM = 64
NA_HEADS = GROUP_WIDTH // NA_DIM
NA_WIN_ROWS = 8
NA_WIN_COLS = 16
NA_QBLK = 16
NA_KBLK = 32
ML_DIM = 128
ML_HEADS = GROUP_WIDTH // ML_DIM
ML_SHORT = 3
ROPE_THETA = 10000.0
PEER_HEADS = 8
PEER_NKEYS = 128
PEER_EXPERTS = PEER_NKEYS * PEER_NKEYS
PEER_DK = 256
PEER_TOPK = 16
HG_COLS = 5 * GROUP_WIDTH
HY_COLS = 3 * GROUP_WIDTH
NA_COLS = 3 * GROUP_WIDTH
ML_COLS = 4 * GROUP_WIDTH + 4 * ML_HEADS
IN_WIDTH = HG_COLS + HY_COLS + NA_COLS + ML_COLS
DEEPNORM_ALPHA = (2 * DEPTH) ** 0.25
DEEPNORM_BETA = (8 * DEPTH) ** -0.25
MASK_VALUE = -1e30


def _centred_dwconv(x, w, b):
    k, L = w.shape[0], x.shape[1]
    xp = jnp.pad(x, ((0, 0), (k // 2, k // 2), (0, 0)))
    y = sum(xp[:, j:j + L] * w[j].astype(x.dtype) for j in range(k))
    return y + b.astype(x.dtype)


def _axial_rope(x):
    L, dh = x.shape[1], x.shape[-1]
    half, quarter = dh // 2, dh // 4
    t = jnp.arange(L)
    inv_freq = ROPE_THETA ** (-jnp.arange(quarter, dtype=jnp.float32) / quarter)

    def rot(xa, pos):
        ang = pos.astype(jnp.float32)[:, None] * inv_freq
        cos = jnp.cos(ang)[None, :, None, :]
        sin = jnp.sin(ang)[None, :, None, :]
        x1, x2 = xa[..., :quarter], xa[..., quarter:]
        return jnp.concatenate([x1 * cos - x2 * sin, x2 * cos + x1 * sin], axis=-1)

    return jnp.concatenate([rot(x[..., :half], t // GRID_W), rot(x[..., half:], t % GRID_W)], axis=-1)


HG_SUB = 16
HG_BLOCK = 256


def _bf16_terms(x):
    f32, bf16 = jnp.float32, jnp.bfloat16
    x_hi = x.astype(bf16)
    r1 = x - x_hi.astype(f32)
    x_mid = r1.astype(bf16)
    return x_hi, x_mid, (r1 - x_mid.astype(f32)).astype(bf16)


def _masked_sums(mask, x):
    return sum(jnp.dot(mask, t, preferred_element_type=jnp.float32) for t in _bf16_terms(x))


def _hgrn2_kernel(*refs, reverse, readout):
    if readout:
        q_ref, z_ref, v_ref, lb_ref, s0_ref, g_ref, nw_ref, of_ref, o_ref, sfin_ref, st_scr = refs
    else:
        q_ref, z_ref, v_ref, lb_ref, s0_ref, o_ref, sfin_ref, st_scr = refs
    f32, bf16 = jnp.float32, jnp.bfloat16
    C, S = CHUNK, HG_SUB
    nsub = C // S
    n_chunks = q_ref.shape[0] // C
    step = pl.program_id(2)

    @pl.when(step == 0)
    def _():
        st_scr[...] = s0_ref[0, 0]

    nt = (((1,), (1,)), ((), ()))
    tn = (((0,), (0,)), ((), ()))
    r_i = lax.broadcasted_iota(jnp.int32, (C, C), 0)
    c_i = lax.broadcasted_iota(jnp.int32, (C, C), 1)
    tri = jnp.where((c_i >= r_i) if reverse else (c_i <= r_i), 1.0, 0.0).astype(bf16)
    row_in_sub = lax.broadcasted_iota(jnp.int32, (S, HG_DIM), 0)
    lb = lb_ref[...]

    def chunk(ci, st):
        c = (n_chunks - 1 - ci) if reverse else ci
        rs = slice(c * C, (c + 1) * C)
        z, q, v = z_ref[rs, :], q_ref[rs, :], v_ref[rs, :]
        neg = jax.nn.sigmoid(-z)
        f = jax.nn.sigmoid(z) + lb * neg
        lf = jnp.log(jnp.maximum(f, HG_F_MIN))
        k = (1.0 - lb) * neg
        cum = _masked_sums(tri, lf)
        total = cum[0:1] if reverse else cum[C - 1:C]
        o = lax.dot_general((q * jnp.exp(cum)).astype(bf16), st.astype(bf16), nt, preferred_element_type=f32)
        kd_end = k * jnp.exp(total - cum)
        st_new = jnp.exp(total) * st + lax.dot_general(v.astype(bf16), kd_end.astype(bf16), tn,
                                                       preferred_element_type=f32)
        parts = []
        for i in range(nsub):
            ts = slice(i * S, (i + 1) * S)
            q_i, cum_i = q[ts], cum[ts]
            acc = o[ts]
            ps = slice((i + 1) * S, C) if reverse else slice(0, i * S)
            if ps.stop > ps.start:
                ref = cum[(i + 1) * S:(i + 1) * S + 1] if reverse else cum[i * S - 1:i * S]
                a = lax.dot_general((q_i * jnp.exp(cum_i - ref)).astype(bf16),
                                    (k[ps] * jnp.exp(ref - cum[ps])).astype(bf16), nt, preferred_element_type=f32)
                acc = acc + jnp.dot(a.astype(bf16), v[ps].astype(bf16), preferred_element_type=f32)
            for s in range(S):
                r = i * S + s
                seen = (row_in_sub <= s) if reverse else (row_in_sub >= s)
                e = jnp.exp(jnp.where(seen, cum_i - cum[r:r + 1], MASK_VALUE))
                w = jnp.sum(q_i * k[r:r + 1] * e, axis=-1, keepdims=True)
                acc = acc + w * v[r:r + 1]
            parts.append(acc)
        o_c = jnp.concatenate(parts, axis=0)
        if readout:
            x = o_c + of_ref[rs, :]
            y = x * lax.rsqrt(jnp.mean(jnp.square(x), axis=-1, keepdims=True) + 1e-6) * nw_ref[...]
            o_ref[rs, :] = (y * jax.nn.silu(g_ref[rs, :])).astype(o_ref.dtype)
        else:
            o_ref[rs, :] = o_c.astype(o_ref.dtype)
        return st_new

    st = st_scr[...]
    for ci in range(n_chunks):
        st = chunk(ci, st)
    st_scr[...] = st

    @pl.when(step == pl.num_programs(2) - 1)
    def _():
        sfin_ref[0, 0] = st


def _hgrn2_scan_call(p, batch, lb, s0, reverse, z_group, readout_args=None):
    L = p.shape[0] // batch
    tb = min(HG_BLOCK, L)
    n_t = L // tb
    assert L % tb == 0 and tb % CHUNK == 0
    H = HG_HEADS

    def rows(b, h, t):
        return b * n_t + ((n_t - 1 - t) if reverse else t)

    def group(g):
        return pl.BlockSpec((tb, HG_DIM), lambda b, h, t: (rows(b, h, t), g * H + h))

    head_cols = pl.BlockSpec((1, HG_DIM), lambda b, h, t: (0, h))
    state = pl.BlockSpec((1, 1, HG_DIM, HG_DIM), lambda b, h, t: (b, h, 0, 0))
    out_rows = pl.BlockSpec((tb, HG_DIM), lambda b, h, t: (rows(b, h, t), h))
    in_specs = [group(0), group(z_group), group(3), head_cols, state]
    args = [p, p, p, lb, s0]
    readout = readout_args is not None
    if readout:
        norm_w, o_other = readout_args
        in_specs += [group(4), pl.BlockSpec((1, HG_DIM), lambda b, h, t: (0, 0)), out_rows]
        args += [p, norm_w.reshape(1, HG_DIM), o_other]
    return pl.pallas_call(
        functools.partial(_hgrn2_kernel, reverse=reverse, readout=readout),
        grid=(batch, H, n_t),
        in_specs=in_specs,
        out_specs=[out_rows, state],
        out_shape=[jax.ShapeDtypeStruct((batch * L, GROUP_WIDTH), jnp.bfloat16 if readout else jnp.float32),
                   jax.ShapeDtypeStruct((batch, H, HG_DIM, HG_DIM), jnp.float32)],
        scratch_shapes=[pltpu.VMEM((HG_DIM, HG_DIM), jnp.float32)],
        compiler_params=pltpu.CompilerParams(dimension_semantics=("arbitrary", "arbitrary", "arbitrary")),
        name="hgrn2_bwd" if reverse else "hgrn2_fwd",
    )(*args)


def _hgrn2_pallas(p_lat, p_ctx, batch, lb, norm_w, need_ctx):
    zero = jnp.zeros((batch, HG_HEADS, HG_DIM, HG_DIM), jnp.float32)
    lb_f, lb_b = lb[0:1], lb[1:2]
    oc_f, s_f = _hgrn2_scan_call(p_ctx, batch, lb_f, zero, False, 1)
    y_ctx, s_b = _hgrn2_scan_call(p_ctx, batch, lb_b, zero, True, 2, (norm_w, oc_f) if need_ctx else None)
    ol_f, _ = _hgrn2_scan_call(p_lat, batch, lb_f, s_f, False, 1)
    y_lat, _ = _hgrn2_scan_call(p_lat, batch, lb_b, s_b, True, 2, (norm_w, ol_f))
    return y_lat, (y_ctx if need_ctx else None)


def _hyena_filters(L, w1, b1, w2, b2, w3, freq):
    f32 = jnp.float32
    pos = jnp.arange(L, dtype=f32)
    t = pos / (L - 1)
    bands = jnp.linspace(1e-4, HY_BANDS - 1, HY_BANDS, dtype=f32)
    ang = (2.0 * math.pi / L) * pos[:, None] * bands[None, :]
    z = jnp.concatenate([t[:, None], jnp.cos(ang), jnp.sin(ang)], axis=-1)
    freq = freq.astype(f32)
    hid = jnp.sin(freq[0] * (z @ w1.astype(f32) + b1.astype(f32)))
    hid = jnp.sin(freq[1] * (hid @ w2.astype(f32) + b2.astype(f32)))
    h = (hid @ w3.astype(f32)).reshape(L, 2, HY_ORDER, HY_WIDTH)
    deltas = jnp.abs(jnp.linspace(HY_MIN_DECAY, HY_MAX_DECAY, HY_WIDTH, dtype=f32))
    h = h * jnp.exp(-t[:, None] * deltas)[:, None, None, :]
    fwd, bwd = h[:, 0], h[:, 1]
    bwd = bwd.at[0].set(0.0)
    norm = jnp.sum(jnp.abs(fwd), axis=0, keepdims=True) + jnp.sum(jnp.abs(bwd), axis=0, keepdims=True)
    return fwd / norm, bwd / norm


HY_FREQ_TILE = 384
HY_TIME_TILE = 256
HY_ANGLE_SPLIT = 64


def _dft_angle_tables(rows, cols, period):
    f32 = jnp.float32
    r = jnp.arange(rows, dtype=jnp.int32)[:, None]
    ch = jnp.arange(cols // HY_ANGLE_SPLIT, dtype=jnp.int32)[None, :] * HY_ANGLE_SPLIT
    cl = jnp.arange(HY_ANGLE_SPLIT, dtype=jnp.int32)[None, :]
    ah = ((r * ch) % period).astype(f32) * (2.0 * math.pi / period)
    al = ((r * cl) % period).astype(f32) * (2.0 * math.pi / period)
    ca, sa, cb, sb = jnp.cos(ah)[:, :, None], jnp.sin(ah)[:, :, None], jnp.cos(al)[:, None, :], jnp.sin(al)[:, None, :]
    return (ca * cb - sa * sb).reshape(rows, cols), (sa * cb + ca * sb).reshape(rows, cols)


def _hyena_dft_matrices(L):
    N = 2 * L
    mp = -(-(L + 1) // HY_FREQ_TILE) * HY_FREQ_TILE
    bf16 = jnp.bfloat16
    k = jnp.arange(mp)
    live = (k <= L).astype(jnp.float32)
    cf, sf = _dft_angle_tables(mp, L, N)
    ci, si = _dft_angle_tables(L, mp, N)
    w = jnp.where((k == 0) | (k == L), 1.0, 2.0) * live / N
    return ((cf * live[:, None]).astype(bf16), (sf * live[:, None]).astype(bf16),
            (ci * w[None, :]).astype(bf16), (-si * w[None, :]).astype(bf16))


def _dft_pair_kernel(c_ref, s_ref, a_ref, b_ref, oc_ref, os_ref):
    oc_ref[...] = jnp.dot(c_ref[...], a_ref[...], preferred_element_type=jnp.float32)
    os_ref[...] = jnp.dot(s_ref[...], b_ref[...], preferred_element_type=jnp.float32)


def _hyena_filter_spectrum(cf, sf, fwd, bwd):
    mp, L = cf.shape
    C = fwd.shape[1]
    bf16 = jnp.bfloat16
    col = lambda i: (0, 0)
    re, s = pl.pallas_call(
        _dft_pair_kernel,
        grid=(mp // HY_FREQ_TILE,),
        in_specs=[pl.BlockSpec((HY_FREQ_TILE, L), lambda i: (i, 0))] * 2 + [pl.BlockSpec((L, C), col)] * 2,
        out_specs=[pl.BlockSpec((HY_FREQ_TILE, C), lambda i: (i, 0))] * 2,
        out_shape=[jax.ShapeDtypeStruct((mp, C), jnp.float32)] * 2,
        compiler_params=pltpu.CompilerParams(vmem_limit_bytes=48 * 1024 * 1024),
        name="hyena_filter_spectrum",
    )(cf, sf, (fwd + bwd).astype(bf16), (fwd - bwd).astype(bf16))
    return re, -s


def _hyena_fwd_kernel(c_ref, s_ref, z_ref, kre_ref, kim_ref, pre_ref, pim_ref):
    f32 = jnp.float32
    xc = jnp.dot(c_ref[...], z_ref[0], preferred_element_type=f32)
    xs = jnp.dot(s_ref[...], z_ref[0], preferred_element_type=f32)
    pre_ref[0] = (xc * kre_ref[...] + xs * kim_ref[...]).astype(pre_ref.dtype)
    pim_ref[0] = (xc * kim_ref[...] - xs * kre_ref[...]).astype(pim_ref.dtype)


def _hyena_inv_kernel(ci_ref, si_ref, pre_ref, pim_ref, z_ref, d_ref, m_ref, o_ref, ob_ref):
    f32 = jnp.float32
    y = (jnp.dot(ci_ref[...], pre_ref[0], preferred_element_type=f32)
         + jnp.dot(si_ref[...], pim_ref[0], preferred_element_type=f32))
    out = m_ref[0] * (y + z_ref[0] * d_ref[...])
    o_ref[0] = out
    ob_ref[0] = out.astype(ob_ref.dtype)


def _hyena_long_conv(mats, z, z_b, kre, kim, d, mult):
    cf, sf, ci, si = mats
    B, L, C = z.shape
    mp = cf.shape[0]
    tt = min(HY_TIME_TILE, L)
    bf16 = jnp.bfloat16
    pre, pim = pl.pallas_call(
        _hyena_fwd_kernel,
        grid=(mp // HY_FREQ_TILE, B),
        in_specs=[pl.BlockSpec((HY_FREQ_TILE, L), lambda i, b: (i, 0))] * 2
                 + [pl.BlockSpec((1, L, C), lambda i, b: (b, 0, 0))]
                 + [pl.BlockSpec((HY_FREQ_TILE, C), lambda i, b: (i, 0))] * 2,
        out_specs=[pl.BlockSpec((1, HY_FREQ_TILE, C), lambda i, b: (b, i, 0))] * 2,
        out_shape=[jax.ShapeDtypeStruct((B, mp, C), bf16)] * 2,
        compiler_params=pltpu.CompilerParams(vmem_limit_bytes=48 * 1024 * 1024),
        name="hyena_fwd",
    )(cf, sf, z_b, kre, kim)
    row = lambda b, t: (b, t, 0)
    return pl.pallas_call(
        _hyena_inv_kernel,
        grid=(B, L // tt),
        in_specs=[pl.BlockSpec((tt, mp), lambda b, t: (t, 0))] * 2
                 + [pl.BlockSpec((1, mp, C), lambda b, t: (b, 0, 0))] * 2
                 + [pl.BlockSpec((1, tt, C), row), pl.BlockSpec((1, C), lambda b, t: (0, 0)),
                    pl.BlockSpec((1, tt, C), row)],
        out_specs=[pl.BlockSpec((1, tt, C), row)] * 2,
        out_shape=[jax.ShapeDtypeStruct((B, L, C), jnp.float32), jax.ShapeDtypeStruct((B, L, C), bf16)],
        compiler_params=pltpu.CompilerParams(vmem_limit_bytes=48 * 1024 * 1024),
        name="hyena_inv",
    )(ci, si, pre, pim, z, d.reshape(1, C).astype(jnp.float32), mult)


def _hyena_seq(p, mats, conv_w, conv_b, w1, b1, w2, b2, w3, freq, dbias):
    u = _centred_dwconv(p, conv_w, conv_b).astype(jnp.float32)
    v, x1, x2 = jnp.split(u, 3, axis=-1)
    L = p.shape[1]
    fwd, bwd = _hyena_filters(L, w1, b1, w2, b2, w3, freq)
    kre, kim = _hyena_filter_spectrum(mats[0], mats[1], fwd.reshape(L, -1), bwd.reshape(L, -1))
    W = HY_WIDTH
    z, z_b = _hyena_long_conv(mats, v, v.astype(jnp.bfloat16), kre[:, :W], kim[:, :W], dbias[0], x1)
    return _hyena_long_conv(mats, z, z_b, kre[:, W:], kim[:, W:], dbias[1], x2)[1]


def _na_pair_attention(q_pair, key_sets, lane):
    f32 = jnp.float32
    nt = (((1,), (1,)), ((), ()))
    outs = []
    for half in range(2):
        own = (lane >= NA_DIM) if half else (lane < NA_DIM)
        q = jnp.where(own, q_pair, jnp.zeros_like(q_pair))
        scores = []
        for k_pair, _, add, valid in key_sets:
            s = lax.dot_general(q, k_pair, nt, preferred_element_type=f32) * (NA_DIM ** -0.5)
            if add is not None:
                s = jnp.where(valid, s + add[half], MASK_VALUE)
            scores.append(s)
        m = scores[0].max(axis=-1, keepdims=True)
        for s in scores[1:]:
            m = jnp.maximum(m, s.max(axis=-1, keepdims=True))
        den = jnp.zeros_like(m)
        acc = jnp.zeros(q_pair.shape, f32)
        for s, (_, v_pair, _, _) in zip(scores, key_sets):
            p = jnp.exp(s - m)
            den = den + p.sum(axis=-1, keepdims=True)
            acc = acc + jnp.dot(p.astype(jnp.bfloat16), v_pair, preferred_element_type=f32)
        outs.append(acc / den)
    return jnp.where(lane < NA_DIM, outs[0], outs[1])


def _natten_lat_kernel(q_ref, k_ref, v_ref, kc_ref, vc_ref, bias_ref, o_ref, *, rows):
    kh = NA_WIN_ROWS
    r = pl.program_id(1)
    start = pl.multiple_of(jnp.clip(r - kh // 2, 0, rows - kh) * GRID_W, GRID_W)
    nk = kh * GRID_W
    qcol = lax.broadcasted_iota(jnp.int32, (GRID_W, nk), 0)
    kcol = lax.broadcasted_iota(jnp.int32, (GRID_W, nk), 1) % GRID_W
    col_start = jnp.clip(qcol - NA_WIN_COLS // 2, 0, GRID_W - NA_WIN_COLS)
    valid = (kcol >= col_start) & (kcol < col_start + NA_WIN_COLS)
    lane = lax.broadcasted_iota(jnp.int32, (GRID_W, 2 * NA_DIM), 1)
    for hp in range(NA_HEADS // 2):
        cs = slice(hp * 2 * NA_DIM, (hp + 1) * 2 * NA_DIM)
        win = (k_ref[pl.ds(start, nk), cs], v_ref[pl.ds(start, nk), cs],
               (bias_ref[0, 2 * hp], bias_ref[0, 2 * hp + 1]), valid)
        ctx = (kc_ref[:, cs], vc_ref[:, cs], None, None)
        o_ref[:, cs] = _na_pair_attention(q_ref[:, cs], [win, ctx], lane).astype(o_ref.dtype)


def _natten_ctx_kernel(q_ref, k_ref, v_ref, o_ref):
    lane = lax.broadcasted_iota(jnp.int32, (q_ref.shape[0], 2 * NA_DIM), 1)
    for hp in range(NA_HEADS // 2):
        cs = slice(hp * 2 * NA_DIM, (hp + 1) * 2 * NA_DIM)
        o_ref[:, cs] = _na_pair_attention(q_ref[:, cs], [(k_ref[:, cs], v_ref[:, cs], None, None)],
                                          lane).astype(o_ref.dtype)


def _natten_bias_table(rpb, rows):
    kh = NA_WIN_ROWS
    dc = jnp.clip(jnp.arange(GRID_W)[None, :] - jnp.arange(GRID_W)[:, None], 1 - NA_WIN_COLS, NA_WIN_COLS - 1)
    t = rpb.astype(jnp.float32)[:, :, dc + NA_WIN_COLS - 1]
    per_off = [t[:, d:d + kh].transpose(0, 2, 1, 3).reshape(NA_HEADS, GRID_W, kh * GRID_W) for d in range(kh)]
    return jnp.stack(per_off, axis=0)


def _natten_pallas(p_lat, p_ctx, rpb, need_ctx, batch):
    W = GROUP_WIDTH
    L = p_lat.shape[0] // batch
    n_ctx = p_ctx.shape[0] // batch
    rows = L // GRID_W
    kh = NA_WIN_ROWS
    assert rows >= kh and L % GRID_W == 0
    bias = _natten_bias_table(rpb, rows)

    def first_key_row_offset(b, r):
        return (jnp.clip(r - kh // 2, 0, rows - kh) - r + kh - 1, 0, 0, 0)

    y_lat = pl.pallas_call(
        functools.partial(_natten_lat_kernel, rows=rows),
        grid=(batch, rows),
        in_specs=[pl.BlockSpec((GRID_W, W), lambda b, r: (b * rows + r, 0)),
                  pl.BlockSpec((L, W), lambda b, r: (b, 1)),
                  pl.BlockSpec((L, W), lambda b, r: (b, 2)),
                  pl.BlockSpec((n_ctx, W), lambda b, r: (b, 1)),
                  pl.BlockSpec((n_ctx, W), lambda b, r: (b, 2)),
                  pl.BlockSpec((1, NA_HEADS, GRID_W, kh * GRID_W), first_key_row_offset)],
        out_specs=pl.BlockSpec((GRID_W, W), lambda b, r: (b * rows + r, 0)),
        out_shape=jax.ShapeDtypeStruct((batch * L, W), jnp.bfloat16),
        compiler_params=pltpu.CompilerParams(vmem_limit_bytes=40 * 1024 * 1024),
        name="natten_lat",
    )(p_lat, p_lat, p_lat, p_ctx, p_ctx, bias)
    y_ctx = None
    if need_ctx:
        y_ctx = pl.pallas_call(
            _natten_ctx_kernel,
            grid=(batch,),
            in_specs=[pl.BlockSpec((n_ctx, W), lambda b: (b, 0)),
                      pl.BlockSpec((n_ctx, W), lambda b: (b, 1)),
                      pl.BlockSpec((n_ctx, W), lambda b: (b, 2))],
            out_specs=pl.BlockSpec((n_ctx, W), lambda b: (b, 0)),
            out_shape=jax.ShapeDtypeStruct((batch * n_ctx, W), jnp.bfloat16),
            name="natten_ctx",
        )(p_ctx, p_ctx, p_ctx)
    return y_lat, y_ctx


ML_BLOCK = 256
ML_GATES = 4 * ML_HEADS


def _mlstm_kernel(*refs, reverse, readout):
    if readout:
        (q_ref, k_ref, v_ref, gc_ref, gr_ref, c0_ref, n0_ref, m0_ref, og_ref, hf_ref,
         o_ref, cfin_ref, nfin_ref, mfin_ref, c_scr, n_scr, m_scr) = refs
    else:
        (q_ref, k_ref, v_ref, gc_ref, gr_ref, c0_ref, n0_ref, m0_ref,
         o_ref, cfin_ref, nfin_ref, mfin_ref, c_scr, n_scr, m_scr) = refs
    f32, bf16 = jnp.float32, jnp.bfloat16
    C = CHUNK
    n_chunks = q_ref.shape[0] // C
    head = pl.program_id(1)
    step = pl.program_id(2)

    @pl.when(step == 0)
    def _():
        c_scr[...] = c0_ref[0, 0]
        n_scr[...] = n0_ref[0, 0]
        m_scr[...] = m0_ref[0, 0]

    nt = (((1,), (1,)), ((), ()))
    tn = (((0,), (0,)), ((), ()))
    r_i = lax.broadcasted_iota(jnp.int32, (C, C), 0)
    c_i = lax.broadcasted_iota(jnp.int32, (C, C), 1)
    seen = (c_i >= r_i) if reverse else (c_i <= r_i)
    tri = jnp.where(seen, 1.0, 0.0).astype(bf16)
    tri_t = jnp.where((r_i >= c_i) if reverse else (r_i <= c_i), 1.0, 0.0).astype(bf16)
    ig_idx = (2 if reverse else 0) * ML_HEADS + head
    fg_idx = ig_idx + ML_HEADS
    lane = lax.broadcasted_iota(jnp.int32, (C, ML_GATES), 1)
    sub = lax.broadcasted_iota(jnp.int32, (ML_GATES, C), 0)

    def chunk(ci, state):
        ckv, n, m = state
        c = (n_chunks - 1 - ci) if reverse else ci
        rs = slice(c * C, (c + 1) * C)
        q, k, v = q_ref[rs, :], k_ref[rs, :], v_ref[rs, :]
        gates = gc_ref[rs, :]
        lf_cols = jax.nn.log_sigmoid(gates)
        g_cols = _masked_sums(tri, lf_cols)
        g_c = jnp.sum(jnp.where(lane == fg_idx, g_cols, 0.0), axis=1, keepdims=True)
        ig_c = jnp.sum(jnp.where(lane == ig_idx, gates, 0.0), axis=1, keepdims=True)
        gates_r = gr_ref[:, rs]
        ig_r = jnp.sum(jnp.where(sub == ig_idx, gates_r, 0.0), axis=0, keepdims=True)
        lf_r = jnp.sum(jnp.where(sub == fg_idx, jax.nn.log_sigmoid(gates_r), 0.0), axis=0, keepdims=True)
        g_r = sum(jnp.dot(t, tri_t, preferred_element_type=f32) for t in _bf16_terms(lf_r))
        m1 = m[:, 0:1]
        logd = jnp.where(seen, g_c - g_r + ig_r, MASK_VALUE)
        log_inter = g_c + m1
        m_out = jnp.maximum(log_inter, jnp.max(logd, axis=1, keepdims=True))
        dmat = jnp.exp(logd - m_out)
        w_inter = jnp.exp(log_inter - m_out)
        qb = q.astype(bf16)
        sc = lax.dot_general(qb, k.astype(bf16), nt, preferred_element_type=f32) * dmat
        num = (jnp.dot(sc.astype(bf16), v.astype(bf16), preferred_element_type=f32)
               + w_inter * jnp.dot(qb, ckv.astype(bf16), preferred_element_type=f32))
        den = jnp.sum(sc, axis=1, keepdims=True) + w_inter * jnp.sum(q * n, axis=1, keepdims=True)
        h = num / jnp.maximum(jnp.abs(den), jnp.exp(-m_out))
        g_end = g_c[0:1] if reverse else g_c[C - 1:C]
        a = g_end - g_c + ig_c
        m_new = jnp.maximum(g_end + m1, jnp.max(a, axis=0, keepdims=True))
        carry_w = jnp.exp(g_end + m1 - m_new)
        wk = jnp.exp(a - m_new) * k
        ckv_new = carry_w * ckv + lax.dot_general(wk.astype(bf16), v.astype(bf16), tn, preferred_element_type=f32)
        n_new = carry_w * n + jnp.sum(wk, axis=0, keepdims=True)
        if readout:
            o_ref[rs, :] = (jax.nn.sigmoid(og_ref[rs, :]) * (h + hf_ref[rs, :])).astype(o_ref.dtype)
        else:
            o_ref[rs, :] = h.astype(o_ref.dtype)
        return ckv_new, n_new, jnp.broadcast_to(m_new, m.shape)

    state = (c_scr[...], n_scr[...], m_scr[...])
    for ci in range(n_chunks):
        state = chunk(ci, state)
    c_scr[...], n_scr[...], m_scr[...] = state

    @pl.when(step == pl.num_programs(2) - 1)
    def _():
        cfin_ref[0, 0], nfin_ref[0, 0], mfin_ref[0, 0] = state


def _mlstm_scan_call(q, k, v_src, v_col0, gates, gates_t, batch, state0, reverse, readout_args=None):
    L = q.shape[0] // batch
    tb = min(ML_BLOCK, L)
    n_t = L // tb
    assert L % tb == 0 and tb % CHUNK == 0 and tb % 128 == 0
    H = ML_HEADS

    def rows(b, h, t):
        return b * n_t + ((n_t - 1 - t) if reverse else t)

    head_block = pl.BlockSpec((tb, ML_DIM), lambda b, h, t: (rows(b, h, t), h))
    mat_state = pl.BlockSpec((1, 1, ML_DIM, ML_DIM), lambda b, h, t: (b, h, 0, 0))
    vec_state = pl.BlockSpec((1, 1, 1, ML_DIM), lambda b, h, t: (b, h, 0, 0))
    in_specs = [head_block, head_block,
                pl.BlockSpec((tb, ML_DIM), lambda b, h, t: (rows(b, h, t), v_col0 + h)),
                pl.BlockSpec((tb, ML_GATES), lambda b, h, t: (rows(b, h, t), 0)),
                pl.BlockSpec((ML_GATES, tb), lambda b, h, t: (0, rows(b, h, t))),
                mat_state, vec_state, vec_state]
    args = [q, k, v_src, gates, gates_t, *state0]
    readout = readout_args is not None
    if readout:
        og_src, og_col0, h_other = readout_args
        in_specs += [pl.BlockSpec((tb, ML_DIM), lambda b, h, t: (rows(b, h, t), og_col0 + h)), head_block]
        args += [og_src, h_other]
    outs = pl.pallas_call(
        functools.partial(_mlstm_kernel, reverse=reverse, readout=readout),
        grid=(batch, H, n_t),
        in_specs=in_specs,
        out_specs=[head_block, mat_state, vec_state, vec_state],
        out_shape=[jax.ShapeDtypeStruct((batch * L, GROUP_WIDTH), jnp.bfloat16 if readout else jnp.float32),
                   jax.ShapeDtypeStruct((batch, H, ML_DIM, ML_DIM), jnp.float32),
                   jax.ShapeDtypeStruct((batch, H, 1, ML_DIM), jnp.float32),
                   jax.ShapeDtypeStruct((batch, H, 1, ML_DIM), jnp.float32)],
        scratch_shapes=[pltpu.VMEM((ML_DIM, ML_DIM), jnp.float32), pltpu.VMEM((1, ML_DIM), jnp.float32),
                        pltpu.VMEM((1, ML_DIM), jnp.float32)],
        compiler_params=pltpu.CompilerParams(dimension_semantics=("arbitrary", "arbitrary", "arbitrary")),
        name="mlstm_bwd" if reverse else "mlstm_fwd",
    )(*args)
    return outs[0], tuple(outs[1:])


def _mlstm_pallas(pm_lat, pm_ctx, batch, conv_w, conv_b, gate_b, need_ctx):
    W = GROUP_WIDTH
    f32 = jnp.float32

    def prep(pm, rope):
        L = pm.shape[0] // batch
        qk = jax.nn.silu(_centred_dwconv(pm[:, :2 * W].reshape(batch, L, 2 * W), conv_w, conv_b)).astype(f32)
        q = qk[..., :W].reshape(batch, L, ML_HEADS, ML_DIM)
        k = qk[..., W:].reshape(batch, L, ML_HEADS, ML_DIM)
        if rope:
            q, k = _axial_rope(q), _axial_rope(k)
        gates = pm[:, 4 * W:] + gate_b.astype(f32)
        return q.reshape(batch * L, W), (k * ML_DIM ** -0.5).reshape(batch * L, W), gates, gates.T

    zero = (jnp.zeros((batch, ML_HEADS, ML_DIM, ML_DIM), f32), jnp.zeros((batch, ML_HEADS, 1, ML_DIM), f32),
            jnp.zeros((batch, ML_HEADS, 1, ML_DIM), f32))
    v0, o0 = 2 * W // ML_DIM, 3 * W // ML_DIM
    qc, kc, gc, gct = prep(pm_ctx, False)
    hc_f, st_f = _mlstm_scan_call(qc, kc, pm_ctx, v0, gc, gct, batch, zero, False)
    y_ctx, st_b = _mlstm_scan_call(qc, kc, pm_ctx, v0, gc, gct, batch, zero, True,
                                   (pm_ctx, o0, hc_f) if need_ctx else None)
    ql, kl, gl, glt = prep(pm_lat, True)
    hl_f, _ = _mlstm_scan_call(ql, kl, pm_lat, v0, gl, glt, batch, st_f, False)
    y_lat, _ = _mlstm_scan_call(ql, kl, pm_lat, v0, gl, glt, batch, st_b, True, (pm_lat, o0, hl_f))
    return y_lat, (y_ctx if need_ctx else None)


PEER_SUB_DK = PEER_DK // 2
PEER_CAND_ROWS = 16 + 8 + 6 * 8 + 8


def _topk_rows(s, order, payload, k):
    sentinel = jnp.float32(2 ** 24)
    vals, picked = [], []
    for _ in range(k):
        m = s.max(axis=0, keepdims=True)
        first = jnp.min(jnp.where(s == m, order, sentinel), axis=0, keepdims=True)
        hit = order == first
        vals.append(m)
        picked.append(first if payload is None else jnp.max(jnp.where(hit, payload, -1.0), axis=0, keepdims=True))
        s = jnp.where(hit, -jnp.inf, s)
    return jnp.concatenate(vals, axis=0), jnp.concatenate(picked, axis=0)


def _peer_candidates(a0, a1, combine):
    pieces = [combine(a0[0:1], a1), combine(a0[1:2], a1[0:8])]
    pieces += [combine(a0[i:i + 1], a1[0:8]) for i in range(2, 8)]
    pieces.append(combine(a0[8:16], a1[0:1]))
    return jnp.concatenate(pieces, axis=0)


def _peer_topk_kernel(x_ref, sc_ref, sh_ref, wq_ref, sk_ref, h_ref, ids_ref, gate_ref):
    f32 = jnp.float32
    tb = x_ref.shape[0]
    h = x_ref[...] * sc_ref[0] + sh_ref[0]
    h_ref[...] = h.astype(h_ref.dtype)
    q = jnp.dot(h.astype(jnp.bfloat16), wq_ref[...], preferred_element_type=f32).astype(jnp.bfloat16)
    key_order = lax.broadcasted_iota(jnp.int32, (PEER_NKEYS, tb), 0).astype(f32)
    rho = lax.broadcasted_iota(jnp.int32, (PEER_CAND_ROWS, tb), 0)
    ci = jnp.where(rho < 16, 0, jnp.where(rho < 24, 1, jnp.where(rho < 72, 2 + ((rho - 24) >> 3), rho - 64)))
    cj = jnp.where(rho < 16, rho, jnp.where(rho < 24, rho - 16, jnp.where(rho < 72, (rho - 24) & 7, 0)))
    cand_order = (ci * PEER_TOPK + cj).astype(f32)
    cand_valid = (ci + 1) * (cj + 1) <= PEER_TOPK
    nt = (((1,), (1,)), ((), ()))
    for head in range(PEER_HEADS):
        sv, si = [], []
        for half in range(2):
            c0 = (head * 2 + half) * PEER_SUB_DK
            s_t = lax.dot_general(sk_ref[head * 2 + half], q[:, c0:c0 + PEER_SUB_DK], nt,
                                  preferred_element_type=f32)
            v, i = _topk_rows(s_t, key_order, None, PEER_TOPK)
            sv.append(v)
            si.append(i)
        cand_s = _peer_candidates(sv[0], sv[1], lambda a, b: a + b)
        cand_s = jnp.where(cand_valid, cand_s, -jnp.inf)
        cand_id = _peer_candidates(si[0], si[1], lambda a, b: a * PEER_NKEYS + b)
        best_s, ids = _topk_rows(cand_s, cand_order, cand_id, PEER_TOPK)
        e = jnp.exp(best_s - best_s[0:1])
        rs = slice(head * PEER_TOPK, (head + 1) * PEER_TOPK)
        ids_ref[rs, :] = ids.astype(jnp.int32)
        gate_ref[rs, :] = e / e.sum(axis=0, keepdims=True)


def _peer_route(x2d, scale, shift, rows_per_mod, wq, subkeys, tb=256):
    T, D = x2d.shape
    nk = PEER_HEADS * PEER_TOPK
    sk = subkeys.reshape(PEER_HEADS * 2, PEER_NKEYS, PEER_SUB_DK).astype(jnp.bfloat16)
    mod = lambda i: (i * tb // rows_per_mod, 0, 0)
    h, ids_t, gate_t = pl.pallas_call(
        _peer_topk_kernel,
        grid=(T // tb,),
        in_specs=[pl.BlockSpec((tb, D), lambda i: (i, 0)),
                  pl.BlockSpec((1, 1, D), mod),
                  pl.BlockSpec((1, 1, D), mod),
                  pl.BlockSpec(wq.shape, lambda i: (0, 0)),
                  pl.BlockSpec(sk.shape, lambda i: (0, 0, 0))],
        out_specs=[pl.BlockSpec((tb, D), lambda i: (i, 0)),
                   pl.BlockSpec((nk, tb), lambda i: (0, i)),
                   pl.BlockSpec((nk, tb), lambda i: (0, i))],
        out_shape=[jax.ShapeDtypeStruct((T, D), jnp.bfloat16),
                   jax.ShapeDtypeStruct((nk, T), jnp.int32),
                   jax.ShapeDtypeStruct((nk, T), jnp.float32)],
        compiler_params=pltpu.CompilerParams(vmem_limit_bytes=48 * 1024 * 1024),
        name="peer_route",
    )(x2d, scale, shift, wq.astype(jnp.bfloat16), sk)
    return h, ids_t.T, gate_t.T


PEER_MASK_SLAB = PEER_NKEYS + 8
PEER_MASK_UNROLL = 32
PEER_KEYS_PER_STEP = 8
PEER_TOKEN_TILE = 512


def _bf16_bits(x):
    u = lax.bitcast_convert_type(x, jnp.uint32)
    return (u + jnp.uint32(0x7FFF) + ((u >> 16) & jnp.uint32(1))) >> 16


def _peer_expert_kernel(x_ref, ids_ref, gate_ref, u_ref, v_ref, o_ref, w_scr):
    f32, bf16 = jnp.float32, jnp.bfloat16
    j = pl.program_id(1)
    tm = x_ref.shape[0]
    half = tm // 2
    nt = (((1,), (1,)), ((), ()))

    @pl.when(j == 0)
    def _():
        o_ref[...] = jnp.zeros_like(o_ref)
        key_iota = lax.broadcasted_iota(jnp.int32, (PEER_NKEYS, ids_ref.shape[1]), 0)

        def token_mask(t):
            ids = ids_ref[pl.ds(t, 1), :]
            g = gate_ref[pl.ds(t, 1), :]
            first = jnp.where((ids >> 7) == key_iota, 1.0, 0.0).astype(bf16)
            second = jnp.where((ids & (PEER_NKEYS - 1)) == key_iota, g, 0.0).astype(bf16)
            return lax.dot_general(first, second, nt, preferred_element_type=f32)

        def pair_group(i, carry):
            for s in range(PEER_MASK_UNROLL // 2):
                p = i * (PEER_MASK_UNROLL // 2) + s
                packed = _bf16_bits(token_mask(p)) | (_bf16_bits(token_mask(p + half)) << 16)
                w_scr[pl.ds(pl.multiple_of(p * PEER_MASK_SLAB, 8), PEER_NKEYS), :] = packed
            return carry

        lax.fori_loop(0, half // (PEER_MASK_UNROLL // 2), pair_group, 0)

    a = lax.dot_general(x_ref[...], u_ref[...], nt, preferred_element_type=f32)
    cols = []
    for s in range(PEER_KEYS_PER_STEP):
        packed = w_scr[pl.ds(j * PEER_KEYS_PER_STEP + s, half, stride=PEER_MASK_SLAB), :]
        lo = lax.bitcast_convert_type(packed << 16, f32)
        hi = lax.bitcast_convert_type(packed & jnp.uint32(0xFFFF0000), f32)
        cols.append(jnp.concatenate([lo, hi], axis=0))
    w = jnp.concatenate(cols, axis=1)
    gelu = 0.5 * a * (1.0 + lax.erf(a * (2.0 ** -0.5)))
    c = (w * gelu).astype(bf16)
    o_ref[...] += jnp.dot(c, v_ref[...], preferred_element_type=f32)


def _peer_expert(tok, ids, gate, u, v):
    T, D = tok.shape
    tm = min(PEER_TOKEN_TILE, T)
    te = PEER_KEYS_PER_STEP * PEER_NKEYS
    nk = ids.shape[1]
    assert PEER_NKEYS == 128 and T % tm == 0 and PEER_EXPERTS % te == 0 and (tm // 2) % PEER_MASK_UNROLL == 0
    return pl.pallas_call(
        _peer_expert_kernel,
        grid=(T // tm, PEER_EXPERTS // te),
        in_specs=[pl.BlockSpec((tm, D), lambda i, j: (i, 0)),
                  pl.BlockSpec((tm, nk), lambda i, j: (i, 0)),
                  pl.BlockSpec((tm, nk), lambda i, j: (i, 0)),
                  pl.BlockSpec((te, D), lambda i, j: (j, 0)),
                  pl.BlockSpec((te, D), lambda i, j: (j, 0))],
        out_specs=pl.BlockSpec((tm, D), lambda i, j: (i, 0)),
        out_shape=jax.ShapeDtypeStruct((T, D), jnp.float32),
        scratch_shapes=[pltpu.VMEM((tm // 2 * PEER_MASK_SLAB, PEER_NKEYS), jnp.uint32)],
        compiler_params=pltpu.CompilerParams(dimension_semantics=("arbitrary", "arbitrary"),
                                             vmem_limit_bytes=56 * 1024 * 1024),
        name="peer_expert",
    )(tok, ids, gate, u, v)


def _peer(x2d, scale, shift, rows_per_mod, wq, subkeys, u, v):
    tok, ids, gate = _peer_route(x2d, scale, shift, rows_per_mod, wq, subkeys)
    return _peer_expert(tok, ids, gate, u, v)


def _in_proj_kernel(x_ref, sc_ref, sh_ref, w_ref, o_ref, a_scr):
    @pl.when(pl.program_id(1) == 0)
    def _():
        a_scr[...] = (x_ref[...] * sc_ref[0] + sh_ref[0]).astype(a_scr.dtype)

    o_ref[...] = jnp.dot(a_scr[...], w_ref[...], preferred_element_type=jnp.float32).astype(o_ref.dtype)


def _in_proj(x2d, scale, shift, rows_per_mod, w, col0, ncols, out_dtype, tm=1024, tn=512):
    M, K = x2d.shape
    tm = min(tm, M)
    assert col0 % tn == 0 and M % tm == 0
    mod = lambda i, j: (i * tm // rows_per_mod, 0, 0)
    return pl.pallas_call(
        _in_proj_kernel,
        grid=(M // tm, pl.cdiv(ncols, tn)),
        in_specs=[pl.BlockSpec((tm, K), lambda i, j: (i, 0)),
                  pl.BlockSpec((1, 1, K), mod),
                  pl.BlockSpec((1, 1, K), mod),
                  pl.BlockSpec((K, tn), lambda i, j: (0, col0 // tn + j))],
        out_specs=pl.BlockSpec((tm, tn), lambda i, j: (i, j)),
        out_shape=jax.ShapeDtypeStruct((M, ncols), out_dtype),
        scratch_shapes=[pltpu.VMEM((tm, K), jnp.bfloat16)],
        compiler_params=pltpu.CompilerParams(dimension_semantics=("arbitrary", "arbitrary"),
                                             vmem_limit_bytes=48 * 1024 * 1024),
        name="in_proj",
    )(x2d, scale, shift, w)


def _out_proj_ln_kernel(a_ref, b_ref, c_ref, d_ref, w_ref, x_ref, gt_ref, g_ref, bt_ref, o_ref):
    f32 = jnp.float32
    W = GROUP_WIDTH
    y = jnp.zeros(x_ref.shape, f32)
    for g, r in enumerate((a_ref, b_ref, c_ref, d_ref)):
        y = y + jnp.dot(r[...].astype(jnp.bfloat16), w_ref[g * W:(g + 1) * W, :], preferred_element_type=f32)
    z = DEEPNORM_ALPHA * x_ref[...] + gt_ref[0] * y
    mu = jnp.mean(z, axis=-1, keepdims=True)
    var = jnp.mean(jnp.square(z - mu), axis=-1, keepdims=True)
    o_ref[...] = (z - mu) * lax.rsqrt(var + 1e-5) * g_ref[...] + bt_ref[...]


def _residual_ln_kernel(x_ref, y_ref, gt_ref, g_ref, bt_ref, o_ref):
    z = DEEPNORM_ALPHA * x_ref[...] + gt_ref[0] * y_ref[...]
    mu = jnp.mean(z, axis=-1, keepdims=True)
    var = jnp.mean(jnp.square(z - mu), axis=-1, keepdims=True)
    o_ref[...] = (z - mu) * lax.rsqrt(var + 1e-5) * g_ref[...] + bt_ref[...]


def _residual_ln(x2d, y2d, gate, rows_per_mod, ln_g, ln_b, tm=256):
    M, D = x2d.shape
    tm = min(tm, M)
    row = lambda i: (i, 0)
    return pl.pallas_call(
        _residual_ln_kernel,
        grid=(M // tm,),
        in_specs=[pl.BlockSpec((tm, D), row), pl.BlockSpec((tm, D), row),
                  pl.BlockSpec((1, 1, D), lambda i: (i * tm // rows_per_mod, 0, 0)),
                  pl.BlockSpec((1, D), lambda i: (0, 0)), pl.BlockSpec((1, D), lambda i: (0, 0))],
        out_specs=pl.BlockSpec((tm, D), row),
        out_shape=jax.ShapeDtypeStruct((M, D), jnp.float32),
        name="residual_ln",
    )(x2d, y2d, gate, ln_g.reshape(1, D), ln_b.reshape(1, D))


def _out_proj_ln(parts, w, x2d, gate, rows_per_mod, ln_g, ln_b, tm=256):
    M, D = x2d.shape
    W = GROUP_WIDTH
    tm = min(tm, M)
    row = lambda i: (i, 0)
    return pl.pallas_call(
        _out_proj_ln_kernel,
        grid=(M // tm,),
        in_specs=[pl.BlockSpec((tm, W), row)] * 4
                 + [pl.BlockSpec(w.shape, lambda i: (0, 0)),
                    pl.BlockSpec((tm, D), row),
                    pl.BlockSpec((1, 1, D), lambda i: (i * tm // rows_per_mod, 0, 0)),
                    pl.BlockSpec((1, D), lambda i: (0, 0)),
                    pl.BlockSpec((1, D), lambda i: (0, 0))],
        out_specs=pl.BlockSpec((tm, D), row),
        out_shape=jax.ShapeDtypeStruct((M, D), jnp.float32),
        compiler_params=pltpu.CompilerParams(vmem_limit_bytes=48 * 1024 * 1024),
        name="out_proj_ln",
    )(*parts, w, x2d, gate, ln_g.reshape(1, D), ln_b.reshape(1, D))


def kernel(x, c, ctx, c_ctx, w_ada, b_ada, w_in, w_out, ln_g, ln_b, hg_lower_bounds, hg_norm_w,
           hy_conv_w, hy_conv_b, hy_w1, hy_b1, hy_w2, hy_b2, hy_w3, hy_freq, hy_dbias, na_rpb,
           ml_conv_w, ml_conv_b, ml_gate_b, peer_wq, peer_subkeys, peer_u, peer_v):
    f32, bf16 = jnp.float32, jnp.bfloat16
    B, L, D = x.shape
    n_ctx = ctx.shape[1]
    lb_soft = jax.nn.softmax(hg_lower_bounds.astype(f32), axis=0)
    lower_bounds = jnp.cumsum(lb_soft, axis=0) - lb_soft[0]
    cond_lat = jax.nn.silu(c)
    cond_ctx = jax.nn.silu(c_ctx)
    c0_na = HG_COLS + HY_COLS
    c0_ml = c0_na + NA_COLS
    x2 = x.reshape(B * L, D)
    dft_lat = _hyena_dft_matrices(L)
    xc2 = ctx.reshape(B * n_ctx, D)
    for l in range(DEPTH):
        need_ctx = l < DEPTH - 1
        m_lat = (cond_lat @ w_ada[l] + b_ada[l]).reshape(B, 1, 6, D)
        m_ctx = (cond_ctx @ w_ada[l] + b_ada[l]).reshape(1, 1, 6, D)
        ml = [m_lat[:, :, k] for k in range(6)]
        mc = [m_ctx[:, :, k] for k in range(6)]
        w_in_l = w_in[l].astype(bf16)
        w_out_l = w_out[l].astype(bf16)

        def in_proj(t, m, rows_per_mod):
            sc, sh = 1 + m[1], m[0]
            return (_in_proj(t, sc, sh, rows_per_mod, w_in_l, 0, c0_na, f32),
                    _in_proj(t, sc, sh, rows_per_mod, w_in_l, c0_na, NA_COLS, bf16),
                    _in_proj(t, sc, sh, rows_per_mod, w_in_l, c0_ml, ML_COLS, f32))

        ph_l, pn_l, pm_l = in_proj(x2, ml, L)
        ph_c, pn_c, pm_c = in_proj(xc2, mc, B * n_ctx)
        pb_l = ph_l[:, HG_COLS:].reshape(B, L, -1)
        pb_c = ph_c[:, HG_COLS:].reshape(B, n_ctx, -1)
        hy = (hy_conv_w[l], hy_conv_b[l], hy_w1[l], hy_b1[l], hy_w2[l], hy_b2[l], hy_w3[l], hy_freq[l], hy_dbias[l])
        a_l, a_c = _hgrn2_pallas(ph_l, ph_c, B, lower_bounds[l], hg_norm_w[l], need_ctx)
        b_l = _hyena_seq(pb_l, dft_lat, *hy)
        c_l, c_c = _natten_pallas(pn_l, pn_c, na_rpb[l], need_ctx, B)
        d_l, d_c = _mlstm_pallas(pm_l, pm_c, B, ml_conv_w[l], ml_conv_b[l], ml_gate_b[l], need_ctx)
        parts_l = (a_l, b_l.reshape(B * L, -1), c_l, d_l)
        if need_ctx:
            b_c = _hyena_seq(pb_c, _hyena_dft_matrices(n_ctx), *hy)
            parts_c = (a_c, b_c.reshape(B * n_ctx, -1), c_c, d_c)
            xc2 = _out_proj_ln(parts_c, w_out_l, xc2, mc[2], B * n_ctx, ln_g[l, 0], ln_b[l, 0])
        x2 = _out_proj_ln(parts_l, w_out_l, x2, ml[2], L, ln_g[l, 0], ln_b[l, 0])
        u_l, v_l = peer_u[l].astype(bf16), peer_v[l].astype(bf16)
        y_lat = _peer(x2, 1 + ml[4], ml[3], L, peer_wq[l], peer_subkeys[l], u_l, v_l)
        x2 = _residual_ln(x2, y_lat, ml[5], L, ln_g[l, 1], ln_b[l, 1])
        if need_ctx:
            y_ctx = _peer(xc2, 1 + mc[4], mc[3], B * n_ctx, peer_wq[l], peer_subkeys[l], u_l, v_l)
            xc2 = _residual_ln(xc2, y_ctx, mc[5], B * n_ctx, ln_g[l, 1], ln_b[l, 1])
    return x2.reshape(B, L, D)
```
